```python
import jax, jax.numpy as jnp
from jax import lax
import numpy as np

D_MODEL = 2048
BATCH = 4
SEQ = 4096
DEPTH = 1

HEAD_DIM = 64
N_FOX_HEADS = D_MODEL // (2 * HEAD_DIM)
N_SWA_HEADS = D_MODEL // (2 * HEAD_DIM)
N_SWA_KV_HEADS = N_SWA_HEADS // 4
SWA_GROUP = N_SWA_HEADS // N_SWA_KV_HEADS
MIX_WIDTH = (N_FOX_HEADS + N_SWA_HEADS) * HEAD_DIM
WINDOW = 128
Q_BLOCK = 128
N_MEM = 256
N_CROSS_HEADS = 4
CROSS_HEAD_DIM = D_MODEL // 16
N_EXPERTS = 32
TOP_K = 4
D_EXPERT = D_MODEL
SWIGLU_ALPHA = 1.702
SWIGLU_LIMIT = 7.0
EXPERT_BLOCK = 256
RMS_EPS = 1e-5
FOX_W = N_FOX_HEADS * HEAD_DIM
SWA_Q_W = N_SWA_HEADS * HEAD_DIM
SWA_KV_W = N_SWA_KV_HEADS * HEAD_DIM
IN_COLS = 3 * FOX_W + N_FOX_HEADS + SWA_Q_W + 2 * SWA_KV_W
CROSS_W = N_CROSS_HEADS * CROSS_HEAD_DIM

kernel_name = "hybrid_fox_swa_sink_moe_layer"


def rmsnorm(x, g):
    xf = x.astype(jnp.float32)
    y = xf * lax.rsqrt(jnp.mean(xf * xf, axis=-1, keepdims=True) + RMS_EPS)
    return (y * g.astype(jnp.float32)).astype(x.dtype)


def alibi_slopes(n):
    start = 2.0 ** (-8.0 / n)
    return np.array([start ** (i + 1) for i in range(n)], dtype=np.float32)


def forgetting_attention(q, k, v, log_f):
    B, S, H, d = q.shape
    nb = S // Q_BLOCK
    c = jnp.cumsum(log_f.astype(jnp.float32), axis=1)
    c_keys = c.transpose(0, 2, 1)
    kpos = jnp.arange(S)
    qb = q.reshape(B, nb, Q_BLOCK, H, d).transpose(1, 0, 2, 3, 4)
    cb = c.reshape(B, nb, Q_BLOCK, H).transpose(1, 0, 3, 2)
    scale = d ** -0.5

    def one_block(args):
        qi, ci, i = args
        s = jnp.einsum('bqhd,bkhd->bhqk', qi, k).astype(jnp.float32) * scale
        s = s + ci[..., None] - c_keys[:, :, None, :]
        qpos = i * Q_BLOCK + jnp.arange(Q_BLOCK)
        s = jnp.where(kpos[None, :] <= qpos[:, None], s, -jnp.inf)
        p = jax.nn.softmax(s, axis=-1).astype(v.dtype)
        return jnp.einsum('bhqk,bkhd->bqhd', p, v)

    out = lax.map(one_block, (qb, cb, jnp.arange(nb)))
    return out.transpose(1, 0, 2, 3, 4).reshape(B, S, H * d)


def sliding_window_sink_attention(q, k, v, sinks):
    B, S, Hkv, G, d = q.shape
    nb = S // WINDOW
    qb = q.reshape(B, nb, WINDOW, Hkv, G, d)

    def band(t):
        tb = t.reshape(B, nb, WINDOW, Hkv, d)
        prev = jnp.concatenate([jnp.zeros_like(tb[:, :1]), tb[:, :-1]], axis=1)
        return jnp.concatenate([prev, tb], axis=2)

    kb, vb = band(k), band(v)
    s = jnp.einsum('bnqhgd,bnkhd->bnhgqk', qb, kb).astype(jnp.float32) * (d ** -0.5)
    qi = jnp.arange(WINDOW)[:, None]
    kj = jnp.arange(2 * WINDOW)[None, :]
    rel = qi - kj + WINDOW
    kabs = jnp.arange(nb)[:, None] * WINDOW - WINDOW + kj
    valid = ((rel >= 0) & (rel < WINDOW))[None] & (kabs >= 0)[:, None, :]
    slopes = jnp.asarray(alibi_slopes(Hkv * G)).reshape(Hkv, G)
    s = s - slopes[:, :, None, None] * rel.astype(jnp.float32)
    s = jnp.where(valid[None, :, None, None], s, -jnp.inf)
    sink = sinks.astype(jnp.float32).reshape(Hkv, G)[None, None, :, :, None, None]
    m = jnp.maximum(jnp.max(s, axis=-1, keepdims=True), sink)
    p = jnp.exp(s - m)
    denom = jnp.sum(p, axis=-1, keepdims=True) + jnp.exp(sink - m)
    p = (p / denom).astype(v.dtype)
    out = jnp.einsum('bnhgqk,bnkhd->bnqhgd', p, vb)
    return out.reshape(B, S, Hkv * G * d)


def head_group_mixer(h, w_in, b_forget, swa_sinks, w_out):
    B, S, _ = h.shape
    proj = h @ w_in
    o1 = FOX_W
    o2 = 2 * FOX_W
    o3 = 3 * FOX_W
    o4 = o3 + N_FOX_HEADS
    o5 = o4 + SWA_Q_W
    o6 = o5 + SWA_KV_W
    fq, fk, fv, ff, sq, sk, sv = jnp.split(proj, [o1, o2, o3, o4, o5, o6], axis=-1)
    fshape = (B, S, N_FOX_HEADS, HEAD_DIM)
    log_f = jax.nn.log_sigmoid((ff + b_forget).astype(jnp.float32))
    fox = forgetting_attention(fq.reshape(fshape), fk.reshape(fshape), fv.reshape(fshape), log_f)
    swa = sliding_window_sink_attention(
        sq.reshape(B, S, N_SWA_KV_HEADS, SWA_GROUP, HEAD_DIM),
        sk.reshape(B, S, N_SWA_KV_HEADS, HEAD_DIM),
        sv.reshape(B, S, N_SWA_KV_HEADS, HEAD_DIM),
        swa_sinks)
    return jnp.concatenate([fox, swa], axis=-1) @ w_out


def memory_cross_attention(h, mem_n, w_cq, w_ckv, w_co):
    B, S, _ = h.shape
    M = mem_n.shape[1]
    q = (h @ w_cq).reshape(B, S, N_CROSS_HEADS, CROSS_HEAD_DIM)
    kv = (mem_n @ w_ckv).reshape(B, M, 2, N_CROSS_HEADS, CROSS_HEAD_DIM)
    k, v = kv[:, :, 0], kv[:, :, 1]
    s = jnp.einsum('bqhd,bkhd->bhqk', q, k).astype(jnp.float32) * (CROSS_HEAD_DIM ** -0.5)
    p = jax.nn.softmax(s, axis=-1).astype(v.dtype)
    o = jnp.einsum('bhqk,bkhd->bqhd', p, v).reshape(B, S, CROSS_W)
    return o @ w_co


def clamped_swiglu(u):
    x_glu = jnp.minimum(u[..., ::2], SWIGLU_LIMIT)
    x_lin = jnp.clip(u[..., 1::2], -SWIGLU_LIMIT, SWIGLU_LIMIT)
    return x_glu * jax.nn.sigmoid(SWIGLU_ALPHA * x_glu) * (x_lin + 1.0)


def routed_moe(h, w_router, b_router, w_mlp1, b_mlp1, w_mlp2, b_mlp2):
    B, S, D = h.shape
    T = B * S
    ht = h.reshape(T, D)
    logits = (ht @ w_router + b_router).astype(jnp.float32)
    top_vals, top_idx = lax.top_k(logits, TOP_K)
    gate = jax.nn.softmax(top_vals, axis=-1)
    A = T * TOP_K
    flat_e = top_idx.reshape(A)
    flat_tok = jnp.arange(A, dtype=jnp.int32) // TOP_K
    flat_gate = gate.reshape(A)
    order = jnp.argsort(flat_e)
    se = flat_e[order]
    counts = jnp.bincount(flat_e, length=N_EXPERTS)
    starts = jnp.cumsum(counts) - counts
    padded = (counts + EXPERT_BLOCK - 1) // EXPERT_BLOCK * EXPERT_BLOCK
    pends = jnp.cumsum(padded)
    pstarts = pends - padded
    dest = pstarts[se] + jnp.arange(A, dtype=jnp.int32) - starts[se]
    n_blocks = A // EXPERT_BLOCK + N_EXPERTS
    P = n_blocks * EXPERT_BLOCK
    row_tok = jnp.full((P,), T, jnp.int32).at[dest].set(flat_tok[order])
    row_gate = jnp.zeros((P,), jnp.float32).at[dest].set(flat_gate[order])
    block_e = jnp.minimum(
        jnp.searchsorted(pends, jnp.arange(n_blocks, dtype=jnp.int32) * EXPERT_BLOCK, side='right'),
        N_EXPERTS - 1)
    xrows = jnp.take(ht, row_tok, axis=0, mode='clip').reshape(n_blocks, EXPERT_BLOCK, D)

    def expert_block(args):
        xb, e = args
        u = xb @ w_mlp1[e] + b_mlp1[e]
        return clamped_swiglu(u) @ w_mlp2[e] + b_mlp2[e]

    yrows = lax.map(expert_block, (xrows, block_e)).reshape(P, D)
    out = jnp.zeros((T, D), h.dtype).at[row_tok].add(
        yrows * row_gate[:, None].astype(h.dtype), mode='drop')
    return out.reshape(B, S, D)


def setup_inputs(seed: int = 0) -> dict:
    key = jax.random.key(seed)
    ks = jax.random.split(key, 24)
    f32 = jnp.float32

    def dense(k, shape, fan_in):
        return jax.random.normal(k, shape, f32) * (fan_in ** -0.5)

    def gain(k):
        return 1.0 + 0.05 * jax.random.normal(k, (DEPTH, D_MODEL), f32)

    L, D, E, I = DEPTH, D_MODEL, N_EXPERTS, D_EXPERT
    return {
        "x": jax.random.normal(ks[0], (BATCH, SEQ, D), f32),
        "mem": jax.random.normal(ks[1], (BATCH, N_MEM, D), f32),
        "g_mix": gain(ks[2]),
        "w_in": dense(ks[3], (L, D, IN_COLS), D),
        "b_forget": 3.0 + 0.5 * jax.random.normal(ks[4], (L, N_FOX_HEADS), f32),
        "swa_sinks": 0.5 * jax.random.normal(ks[5], (L, N_SWA_HEADS), f32),
        "w_out": dense(ks[6], (L, MIX_WIDTH, D), MIX_WIDTH),
        "g_cross": gain(ks[7]),
        "g_mem": gain(ks[8]),
        "w_cq": dense(ks[9], (L, D, CROSS_W), D),
        "w_ckv": dense(ks[10], (L, D, 2 * CROSS_W), D),
        "w_co": dense(ks[11], (L, CROSS_W, D), CROSS_W),
        "g_moe": gain(ks[12]),
        "w_router": dense(ks[13], (L, D, E), D),
        "b_router": 0.01 * jax.random.normal(ks[14], (L, E), f32),
        "w_mlp1": dense(ks[15], (L, E, D, 2 * I), D),
        "b_mlp1": 0.01 * jax.random.normal(ks[16], (L, E, 2 * I), f32),
        "w_mlp2": dense(ks[17], (L, E, I, D), I),
        "b_mlp2": 0.01 * jax.random.normal(ks[18], (L, E, D), f32),
        "g_final": 1.0 + 0.05 * jax.random.normal(ks[19], (D,), f32),
    }


def reference(x, mem, g_mix, w_in, b_forget, swa_sinks, w_out, g_cross, g_mem,
              w_cq, w_ckv, w_co, g_moe, w_router, b_router, w_mlp1, b_mlp1,
              w_mlp2, b_mlp2, g_final):
    for l in range(DEPTH):
        x = x + head_group_mixer(rmsnorm(x, g_mix[l]), w_in[l], b_forget[l], swa_sinks[l], w_out[l])
        x = x + memory_cross_attention(rmsnorm(x, g_cross[l]), rmsnorm(mem, g_mem[l]),
                                       w_cq[l], w_ckv[l], w_co[l])
        x = x + routed_moe(rmsnorm(x, g_moe[l]), w_router[l], b_router[l],
                           w_mlp1[l], b_mlp1[l], w_mlp2[l], b_mlp2[l])
    return rmsnorm(x, g_final)
```

```python
import functools

import numpy as np
import jax
import jax.numpy as jnp
from jax import lax
from jax.experimental import pallas as pl
from jax.experimental.pallas import tpu as pltpu

F32 = jnp.float32
BF16 = jnp.bfloat16

RMS_EPS = 1e-5
HEAD_DIM = 64
WINDOW = 128
SWA_GROUP = 4
N_CROSS_HEADS = 4
TOP_K = 4
SWIGLU_ALPHA = 1.702
SWIGLU_LIMIT = 7.0

LANES = 128
MXU_DIM = 256
VMEM_LIMIT_BYTES = 56 * 2**20
NEG_BIG = -1e30

MOE_ROW_TILE = 256
MOE_ROWS_PER_ITEM = 1024
MOE_COL_CHUNK = 1024


def _cparams(*sem):
    return pltpu.CompilerParams(dimension_semantics=sem, vmem_limit_bytes=VMEM_LIMIT_BYTES)


def _rmsnorm(x, g):
    ms = jnp.mean(x * x, axis=-1, keepdims=True)
    return x * lax.rsqrt(ms + RMS_EPS) * g


def _dot_nt(a, b):
    return lax.dot_general(a, b, (((1,), (1,)), ((), ())), preferred_element_type=F32)


def _norm_matmul_body(x_ref, g_ref, w_ref, o_ref, h_ref):
    @pl.when(pl.program_id(1) == 0)
    def _():
        h_ref[...] = _rmsnorm(x_ref[...], g_ref[...]).astype(h_ref.dtype)

    o_ref[...] = jnp.dot(h_ref[...], w_ref[...], preferred_element_type=F32).astype(o_ref.dtype)


def _norm_matmul(x2d, g, w, out_dtype, bm, bn):
    m, d = x2d.shape
    n = w.shape[1]
    bm, bn = min(bm, m), min(bn, n)
    return pl.pallas_call(
        _norm_matmul_body,
        grid=(m // bm, n // bn),
        in_specs=[
            pl.BlockSpec((bm, d), lambda i, j: (i, 0)),
            pl.BlockSpec((1, d), lambda i, j: (0, 0)),
            pl.BlockSpec((d, bn), lambda i, j: (0, j)),
        ],
        out_specs=pl.BlockSpec((bm, bn), lambda i, j: (i, j)),
        out_shape=jax.ShapeDtypeStruct((m, n), out_dtype),
        scratch_shapes=[pltpu.VMEM((bm, d), BF16)],
        compiler_params=_cparams("parallel", "arbitrary"),
    )(x2d, g.reshape(1, d), w)


def _fox_gate_body(x_ref, g_ref, wt_ref, b_ref, c_ref, carry_ref):
    @pl.when(pl.program_id(1) == 0)
    def _():
        carry_ref[...] = jnp.zeros_like(carry_ref)

    bs = x_ref.shape[1]
    h = _rmsnorm(x_ref[0], g_ref[...]).astype(BF16)
    z = _dot_nt(wt_ref[...], h) + b_ref[...]
    logf = jnp.minimum(z, 0.0) - jnp.log1p(jnp.exp(-jnp.abs(z)))
    r = lax.broadcasted_iota(jnp.int32, (bs, bs), 0)
    c = lax.broadcasted_iota(jnp.int32, (bs, bs), 1)
    tri = (r <= c).astype(F32)
    cs = jnp.dot(logf, tri, preferred_element_type=F32, precision=lax.Precision.HIGHEST)
    cs = cs + carry_ref[:, :1]
    c_ref[0] = cs
    carry_ref[...] = jnp.broadcast_to(cs[:, bs - 1:bs], carry_ref.shape)


def _fox_gate(x, g, w_ff_t, b_forget, bs):
    b, s, d = x.shape
    nh = w_ff_t.shape[0]
    bs = min(bs, s)
    return pl.pallas_call(
        _fox_gate_body,
        grid=(b, s // bs),
        in_specs=[
            pl.BlockSpec((1, bs, d), lambda i, j: (i, j, 0)),
            pl.BlockSpec((1, d), lambda i, j: (0, 0)),
            pl.BlockSpec((nh, d), lambda i, j: (0, 0)),
            pl.BlockSpec((nh, 1), lambda i, j: (0, 0)),
        ],
        out_specs=pl.BlockSpec((1, nh, bs), lambda i, j: (i, 0, j)),
        out_shape=jax.ShapeDtypeStruct((b, nh, s), F32),
        scratch_shapes=[pltpu.VMEM((nh, LANES), F32)],
        compiler_params=_cparams("parallel", "arbitrary"),
    )(x, g.reshape(1, d), w_ff_t, b_forget.reshape(nh, 1)).reshape(b, nh, 1, s)


def _fox_attn_body(q_ref, k_ref, v_ref, c0_ref, c1_ref, o_ref, m_ref, l_ref, acc_ref):
    i = pl.program_id(2)
    j = pl.program_id(3)
    tq = q_ref.shape[0]
    tk = k_ref.shape[0]
    lane = lax.broadcasted_iota(jnp.int32, (1, LANES), 1)

    @pl.when(j == 0)
    def _():
        m_ref[...] = jnp.full_like(m_ref, NEG_BIG)
        l_ref[...] = jnp.zeros_like(l_ref)
        acc_ref[...] = jnp.zeros_like(acc_ref)

    @pl.when(j <= i)
    def _():
        q = q_ref[...]
        k = k_ref[...]
        v = v_ref[...]
        row = lax.broadcasted_iota(jnp.int32, (tq, tk), 0) + i * tq
        col = lax.broadcasted_iota(jnp.int32, (tq, tk), 1) + j * tk
        causal = col <= row
        for hh, c_ref in enumerate((c0_ref, c1_ref)):
            qh = jnp.where(lane // HEAD_DIM == hh, q, jnp.zeros_like(q)) * (HEAD_DIM ** -0.5)
            s = _dot_nt(qh.astype(BF16), k) - c_ref[0, 0]
            s = jnp.where(causal, s, NEG_BIG)
            m_prev = m_ref[hh][:, :1]
            m_new = jnp.maximum(m_prev, jnp.max(s, axis=-1, keepdims=True))
            alpha = jnp.exp(m_prev - m_new)
            p = jnp.exp(s - m_new)
            l_new = alpha * l_ref[hh][:, :1] + jnp.sum(p, axis=-1, keepdims=True)
            acc_ref[hh] = alpha * acc_ref[hh] + jnp.dot(p.astype(BF16), v, preferred_element_type=F32)
            m_ref[hh] = jnp.broadcast_to(m_new, (tq, LANES))
            l_ref[hh] = jnp.broadcast_to(l_new, (tq, LANES))

    @pl.when(j == i)
    def _():
        o0 = acc_ref[0] / l_ref[0][:, :1]
        o1 = acc_ref[1] / l_ref[1][:, :1]
        o_ref[...] = jnp.where(lane < HEAD_DIM, o0, o1).astype(o_ref.dtype)


def _fox_attention(qkv, c, batch, seq, n_pairs, tq):
    tq = min(tq, seq)
    nq = seq // tq

    def q_map(b, p, i, j):
        return (b * nq + i, p)

    def k_map(b, p, i, j):
        return (b * nq + jnp.minimum(j, i), n_pairs + p)

    def v_map(b, p, i, j):
        return (b * nq + jnp.minimum(j, i), 2 * n_pairs + p)

    def c_map(hh):
        return lambda b, p, i, j: (b, 2 * p + hh, 0, jnp.minimum(j, i))

    return pl.pallas_call(
        _fox_attn_body,
        grid=(batch, n_pairs, nq, nq),
        in_specs=[
            pl.BlockSpec((tq, LANES), q_map),
            pl.BlockSpec((tq, LANES), k_map),
            pl.BlockSpec((tq, LANES), v_map),
            pl.BlockSpec((1, 1, 1, tq), c_map(0)),
            pl.BlockSpec((1, 1, 1, tq), c_map(1)),
        ],
        out_specs=pl.BlockSpec((tq, LANES), q_map),
        out_shape=jax.ShapeDtypeStruct((batch * seq, n_pairs * LANES), BF16),
        scratch_shapes=[
            pltpu.VMEM((2, tq, LANES), F32),
            pltpu.VMEM((2, tq, LANES), F32),
            pltpu.VMEM((2, tq, LANES), F32),
        ],
        compiler_params=_cparams("parallel", "parallel", "parallel", "arbitrary"),
    )(qkv, qkv, qkv, c, c)


def _swa_body(slopes_ref, sinks_ref, q_ref, kp_ref, kc_ref, vp_ref, vc_ref, o_ref):
    pr = pl.program_id(1)
    n = pl.program_id(2)
    w = q_ref.shape[0]
    lane = lax.broadcasted_iota(jnp.int32, (1, LANES), 1)
    q = q_ref[...]
    k = jnp.concatenate([kp_ref[...], kc_ref[...]], axis=0)
    v = jnp.concatenate([vp_ref[...], vc_ref[...]], axis=0)
    qi = lax.broadcasted_iota(jnp.int32, (w, 2 * w), 0)
    kj = lax.broadcasted_iota(jnp.int32, (w, 2 * w), 1)
    rel = qi - kj + w
    valid = (rel >= 0) & (rel < w) & (n * w - w + kj >= 0)
    relf = rel.astype(F32)
    outs = []
    for hh in range(2):
        slope = slopes_ref[2 * pr + hh]
        sink = sinks_ref[2 * pr + hh]
        qh = jnp.where(lane // HEAD_DIM == hh, q, jnp.zeros_like(q))
        s = _dot_nt(qh, k) * (HEAD_DIM ** -0.5) - slope * relf
        s = jnp.where(valid, s, NEG_BIG)
        m = jnp.maximum(jnp.max(s, axis=-1, keepdims=True), sink)
        p = jnp.exp(s - m)
        denom = jnp.sum(p, axis=-1, keepdims=True) + jnp.exp(sink - m)
        p = (p / denom).astype(BF16)
        outs.append(jnp.dot(p, v, preferred_element_type=F32))
    o_ref[...] = jnp.where(lane < HEAD_DIM, outs[0], outs[1]).astype(o_ref.dtype)


def _swa_attention(proj, slopes, sinks, batch, seq, n_pairs, n_kv):
    nb = seq // WINDOW

    def q_map(b, p, n):
        return (b * nb + n, p)

    def kv_map(base, prev):
        def f(b, p, n):
            blk = jnp.maximum(n - 1, 0) if prev else n
            return (b * nb + blk, base + p // 2)
        return f

    smem = pl.BlockSpec(memory_space=pltpu.SMEM)
    blk = (WINDOW, LANES)
    return pl.pallas_call(
        _swa_body,
        grid=(batch, n_pairs, nb),
        in_specs=[
            smem, smem,
            pl.BlockSpec(blk, q_map),
            pl.BlockSpec(blk, kv_map(n_pairs, True)),
            pl.BlockSpec(blk, kv_map(n_pairs, False)),
            pl.BlockSpec(blk, kv_map(n_pairs + n_kv, True)),
            pl.BlockSpec(blk, kv_map(n_pairs + n_kv, False)),
        ],
        out_specs=pl.BlockSpec(blk, q_map),
        out_shape=jax.ShapeDtypeStruct((batch * seq, n_pairs * LANES), BF16),
        compiler_params=_cparams("parallel", "parallel", "parallel"),
    )(slopes, sinks, proj, proj, proj, proj, proj)


def _out_proj_body(a1_ref, a2_ref, w_ref, x_ref, o_ref):
    a = jnp.concatenate([a1_ref[...], a2_ref[...]], axis=1)
    o_ref[...] = x_ref[...] + jnp.dot(a, w_ref[...], preferred_element_type=F32)


def _out_proj(a1, a2, w, x2d, bm, bn):
    m, d = x2d.shape
    k1, k2 = a1.shape[1], a2.shape[1]
    bm, bn = min(bm, m), min(bn, d)
    return pl.pallas_call(
        _out_proj_body,
        grid=(m // bm, d // bn),
        in_specs=[
            pl.BlockSpec((bm, k1), lambda i, j: (i, 0)),
            pl.BlockSpec((bm, k2), lambda i, j: (i, 0)),
            pl.BlockSpec((k1 + k2, bn), lambda i, j: (0, j)),
            pl.BlockSpec((bm, bn), lambda i, j: (i, j)),
        ],
        out_specs=pl.BlockSpec((bm, bn), lambda i, j: (i, j)),
        out_shape=jax.ShapeDtypeStruct((m, d), F32),
        compiler_params=_cparams("parallel", "parallel"),
    )(a1, a2, w, x2d)


def _cross_body(x_ref, g_ref, wq_ref, kv_ref, wo_ref, o_ref):
    x = x_ref[...]
    h = _rmsnorm(x, g_ref[...]).astype(BF16)
    q = jnp.dot(h, wq_ref[...], preferred_element_type=F32).astype(BF16)
    cw = wq_ref.shape[1]
    hd = cw // N_CROSS_HEADS
    outs = []
    for hh in range(N_CROSS_HEADS):
        qh = q[:, hh * hd:(hh + 1) * hd]
        kh = kv_ref[:, hh * hd:(hh + 1) * hd]
        vh = kv_ref[:, cw + hh * hd:cw + (hh + 1) * hd]
        s = _dot_nt(qh, kh) * (hd ** -0.5)
        e = jnp.exp(s - jnp.max(s, axis=-1, keepdims=True))
        p = (e / jnp.sum(e, axis=-1, keepdims=True)).astype(BF16)
        outs.append(jnp.dot(p, vh, preferred_element_type=F32))
    o = jnp.concatenate(outs, axis=1).astype(BF16)
    o_ref[...] = x + jnp.dot(o, wo_ref[...], preferred_element_type=F32)


def _cross_block(x2d, g, wq, kv, wo, seq, n_mem, bm):
    m, d = x2d.shape
    bm = min(bm, seq)
    per_b = seq // bm
    cw = wq.shape[1]
    return pl.pallas_call(
        _cross_body,
        grid=(m // bm,),
        in_specs=[
            pl.BlockSpec((bm, d), lambda i: (i, 0)),
            pl.BlockSpec((1, d), lambda i: (0, 0)),
            pl.BlockSpec((d, cw), lambda i: (0, 0)),
            pl.BlockSpec((n_mem, 2 * cw), lambda i: (i // per_b, 0)),
            pl.BlockSpec((cw, d), lambda i: (0, 0)),
        ],
        out_specs=pl.BlockSpec((bm, d), lambda i: (i, 0)),
        out_shape=jax.ShapeDtypeStruct((m, d), F32),
        compiler_params=_cparams("parallel"),
    )(x2d, g.reshape(1, d), wq, kv, wo)


def _router_body(x_ref, g_ref, w_ref, b_ref, idx_ref, gate_ref, rank_ref, cnt_ref, carry_ref):
    @pl.when(pl.program_id(0) == 0)
    def _():
        carry_ref[...] = jnp.zeros_like(carry_ref)

    bm = x_ref.shape[0]
    h = _rmsnorm(x_ref[...], g_ref[...]).astype(BF16)
    logits = jnp.dot(h, w_ref[...], preferred_element_type=F32) + b_ref[...]
    lane = lax.broadcasted_iota(jnp.int32, (bm, LANES), 1)
    lane_f = lane.astype(F32)
    work = logits
    self32 = jnp.zeros((bm, LANES), F32)
    vals, picks, hots = [], [], []
    for _ in range(TOP_K):
        mk = jnp.max(work, axis=-1, keepdims=True)
        pick = jnp.min(jnp.where(work == mk, lane_f, float(LANES)), axis=-1, keepdims=True)
        hot = lane_f == pick
        vals.append(mk)
        picks.append(pick.astype(jnp.int32))
        hots.append(hot)
        self32 = jnp.where(hot, 1.0, self32)
        work = jnp.where(hot, -jnp.inf, work)
    es = [jnp.exp(v - vals[0]) for v in vals]
    denom = es[0] + es[1] + es[2] + es[3]

    r = lax.broadcasted_iota(jnp.int32, (bm, bm), 0)
    c = lax.broadcasted_iota(jnp.int32, (bm, bm), 1)
    tri = (c < r).astype(BF16)
    rank =jnp.dot(tri, self32.astype(BF16), preferred_element_type=F32) + carry_ref[0:1, :]
    carry_new = carry_ref[0:1, :] + jnp.sum(self32, axis=0, keepdims=True)
    carry_ref[...] = jnp.broadcast_to(carry_new, carry_ref.shape)

    idx_o = jnp.zeros((bm, LANES), jnp.int32)
    gate_o = jnp.zeros((bm, LANES), F32)
    rank_o = jnp.zeros((bm, LANES), jnp.int32)
    for kk in range(TOP_K):
        rk = jnp.sum(jnp.where(hots[kk], rank, 0.0), axis=-1, keepdims=True).astype(jnp.int32)
        idx_o = jnp.where(lane == kk, picks[kk], idx_o)
        gate_o = jnp.where(lane == kk, es[kk] / denom, gate_o)
        rank_o = jnp.where(lane == kk, rk, rank_o)
    idx_ref[...] = idx_o
    gate_ref[...] = gate_o
    rank_ref[...] = rank_o
    cnt_ref[...] = jnp.broadcast_to(carry_new, cnt_ref.shape).astype(jnp.int32)


def _router(x2d, g, w_pad, b_pad, bm):
    m, d = x2d.shape
    bm = min(bm, m)
    tok = pl.BlockSpec((bm, LANES), lambda i: (i, 0))
    return pl.pallas_call(
        _router_body,
        grid=(m // bm,),
        in_specs=[
            pl.BlockSpec((bm, d), lambda i: (i, 0)),
            pl.BlockSpec((1, d), lambda i: (0, 0)),
            pl.BlockSpec((d, LANES), lambda i: (0, 0)),
            pl.BlockSpec((1, LANES), lambda i: (0, 0)),
        ],
        out_specs=[tok, tok, tok, pl.BlockSpec((8, LANES), lambda i: (0, 0))],
        out_shape=[
            jax.ShapeDtypeStruct((m, LANES), jnp.int32),
            jax.ShapeDtypeStruct((m, LANES), F32),
            jax.ShapeDtypeStruct((m, LANES), jnp.int32),
            jax.ShapeDtypeStruct((8, LANES), jnp.int32),
        ],
        scratch_shapes=[pltpu.VMEM((8, LANES), F32)],
        compiler_params=_cparams("arbitrary"),
    )(x2d, g.reshape(1, d), w_pad, b_pad)


def _dispatch_body(dest_ref, pend_ref, padded_ref, x_ref, xs_ref, z_ref, sem, zsem, *, bm):
    i = pl.program_id(0)
    n_exp = pend_ref.shape[0]
    tm = z_ref.shape[0]

    def zero_copy(e):
        return pltpu.make_async_copy(z_ref, xs_ref.at[pl.ds(pend_ref[e] - tm, tm)], zsem)

    @pl.when(i == 0)
    def _():
        z_ref[...] = jnp.zeros_like(z_ref)
        for e in range(n_exp):
            @pl.when(padded_ref[e] > 0)
            def _():
                zero_copy(e).start()
        for e in range(n_exp):
            @pl.when(padded_ref[e] > 0)
            def _():
                zero_copy(e).wait()

    def row_copy(r, kk):
        tok = i * bm + r
        d = dest_ref[tok * TOP_K + kk]
        return pltpu.make_async_copy(x_ref.at[tok], xs_ref.at[d], sem)

    def issue(r, carry):
        for kk in range(TOP_K):
            row_copy(r, kk).start()
        return carry

    def drain(r, carry):
        for kk in range(TOP_K):
            row_copy(r, kk).wait()
        return carry

    lax.fori_loop(0, bm, issue, 0)
    lax.fori_loop(0, bm, drain, 0)


def _dispatch(dest_flat, pend, padded, x3d, n_rows, bm):
    m, _, d = x3d.shape
    bm = min(bm, m)
    grid_spec = pltpu.PrefetchScalarGridSpec(
        num_scalar_prefetch=3,
        grid=(m // bm,),
        in_specs=[pl.BlockSpec(memory_space=pl.ANY)],
        out_specs=pl.BlockSpec(memory_space=pl.ANY),
        scratch_shapes=[
            pltpu.VMEM((MOE_ROW_TILE, 1, d), F32),
            pltpu.SemaphoreType.DMA,
            pltpu.SemaphoreType.DMA,
        ],
    )
    return pl.pallas_call(
        functools.partial(_dispatch_body, bm=bm),
        grid_spec=grid_spec,
        out_shape=jax.ShapeDtypeStruct((n_rows, 1, d), F32),
        compiler_params=_cparams("arbitrary"),
    )(dest_flat, pend, padded, x3d)


def _moe_body(ie_ref, istart_ref, insub_ref, xs_ref, g_ref, w1_ref, b1_ref, w2_ref, b2_ref, perm_ref,
              ys_ref, xbf_ref, yacc_ref, stage_ref, flat_ref, w1p_ref, w2b_ref, sem):
    it = pl.program_id(0)
    c = pl.program_id(1)
    n_chunks = pl.num_programs(1)
    nsub = insub_ref[it]
    start = istart_ref[it]
    tm = stage_ref.shape[0]
    tc = w1_ref.shape[2]
    half = MXU_DIM // 2

    def rows(s):
        return pl.ds(pl.multiple_of(s * tm, tm), tm)

    @pl.when(nsub > 0)
    def _():
        @pl.when(c == 0)
        def _():
            def load(s, carry):
                cp = pltpu.make_async_copy(xs_ref.at[pl.ds(start + s * tm, tm)], stage_ref, sem)
                cp.start()
                cp.wait()
                flat_ref[...] = stage_ref[...].reshape(flat_ref.shape)
                xbf_ref[rows(s), :] = _rmsnorm(flat_ref[...], g_ref[...]).astype(BF16)
                yacc_ref[rows(s), :] = jnp.zeros((tm, yacc_ref.shape[1]), F32)
                return carry
            lax.fori_loop(0, nsub, load, 0)

        for b in range(tc // MXU_DIM):
            blk = w1_ref[0, :, b * MXU_DIM:(b + 1) * MXU_DIM].astype(BF16)
            w1p_ref[:, b * MXU_DIM:(b + 1) * MXU_DIM] = jnp.dot(
                blk, perm_ref[...], preferred_element_type=F32).astype(BF16)
        w2b_ref[...] = w2_ref[0].astype(BF16)

        def compute(s, carry):
            u = jnp.dot(xbf_ref[rows(s), :], w1p_ref[...], preferred_element_type=F32) + b1_ref[0]
            parts = []
            for b in range(tc // MXU_DIM):
                glu = jnp.minimum(u[:, b * MXU_DIM:b * MXU_DIM + half], SWIGLU_LIMIT)
                lin = jnp.clip(u[:, b * MXU_DIM + half:(b + 1) * MXU_DIM], -SWIGLU_LIMIT, SWIGLU_LIMIT)
                sig = 1.0 / (1.0 + jnp.exp(-SWIGLU_ALPHA * glu))
                parts.append((glu * sig * (lin + 1.0)).astype(BF16))
            a = jnp.concatenate(parts, axis=1)
            yacc_ref[rows(s), :] += jnp.dot(a, w2b_ref[...], preferred_element_type=F32)
            return carry
        lax.fori_loop(0, nsub, compute, 0)

        @pl.when(c == n_chunks - 1)
        def _():
            def store(s, carry):
                flat_ref[...] = yacc_ref[rows(s), :] + b2_ref[0]
                stage_ref[...] = flat_ref[...].reshape(stage_ref.shape)
                cp =pltpu.make_async_copy(stage_ref, ys_ref.at[pl.ds(start + s * tm, tm)], sem)
                cp.start()
                cp.wait()
                return carry
            lax.fori_loop(0, nsub, store, 0)


def _moe_mlp(item_e, item_start, item_nsub, xs, g, w1, b1p, w2, b2, perm):
    n_rows, _, d = xs.shape
    n_exp, _, two_i = w1.shape
    tc = min(MOE_COL_CHUNK, two_i)
    n_chunks = two_i // tc
    n_items = item_e.shape[0]
    last = n_chunks - 1

    def chunk(c, it, insub):
        return jnp.where(insub[it] > 0, c, last)

    grid_spec = pltpu.PrefetchScalarGridSpec(
        num_scalar_prefetch=3,
        grid=(n_items, n_chunks),
        in_specs=[
            pl.BlockSpec(memory_space=pl.ANY),
            pl.BlockSpec((1, d), lambda it, c, ie, ist, ins: (0, 0)),
            pl.BlockSpec((1, d, tc), lambda it, c, ie, ist, ins: (ie[it], 0, chunk(c, it, ins))),
            pl.BlockSpec((1, 1, tc), lambda it, c, ie, ist, ins: (ie[it], 0, chunk(c, it, ins))),
            pl.BlockSpec((1, tc // 2, d), lambda it, c, ie, ist, ins: (ie[it], chunk(c, it, ins), 0)),
            pl.BlockSpec((1, 1, d), lambda it, c, ie, ist, ins: (ie[it], 0, 0)),
            pl.BlockSpec((MXU_DIM, MXU_DIM), lambda it, c, ie, ist, ins: (0, 0)),
        ],
        out_specs=pl.BlockSpec(memory_space=pl.ANY),
        scratch_shapes=[
            pltpu.VMEM((MOE_ROWS_PER_ITEM, d), BF16),
            pltpu.VMEM((MOE_ROWS_PER_ITEM, d), F32),
            pltpu.VMEM((MOE_ROW_TILE, 1, d), F32),
            pltpu.VMEM((MOE_ROW_TILE, d), F32),
            pltpu.VMEM((d, tc), BF16),
            pltpu.VMEM((tc // 2, d), BF16),
            pltpu.SemaphoreType.DMA,
        ],
    )
    return pl.pallas_call(
        _moe_body,
        grid_spec=grid_spec,
        out_shape=jax.ShapeDtypeStruct((n_rows, 1, d), F32),
        compiler_params=_cparams("arbitrary", "arbitrary"),
    )(item_e, item_start, item_nsub, xs, g.reshape(1, d), w1, b1p, w2, b2, perm)


def _combine_body(dest_ref, x_ref, gate_ref, g_ref, ys_ref, o_ref, b0, b1, b2, b3, flat_ref, sem, *, final_norm):
    i = pl.program_id(0)
    bm = x_ref.shape[0]
    bufs = (b0, b1, b2, b3)

    def row_copy(r, kk):
        d = dest_ref[(i * bm + r) * TOP_K + kk]
        return pltpu.make_async_copy(ys_ref.at[d], bufs[kk].at[r], sem)

    def issue(r, carry):
        for kk in range(TOP_K):
            row_copy(r, kk).start()
        return carry

    def drain(r, carry):
        for kk in range(TOP_K):
            row_copy(r, kk).wait()
        return carry

    lax.fori_loop(0, bm, issue, 0)
    lax.fori_loop(0, bm, drain, 0)
    acc = x_ref[...]
    for kk in range(TOP_K):
        flat_ref[...] = bufs[kk][...].reshape(flat_ref.shape)
        acc = acc + gate_ref[:, kk:kk + 1] * flat_ref[...]
    o_ref[...] = _rmsnorm(acc, g_ref[...]) if final_norm else acc


def _combine(dest_flat, x2d, gate4, g, ys, bm, final_norm):
    m, d = x2d.shape
    bm = min(bm, m)
    grid_spec = pltpu.PrefetchScalarGridSpec(
        num_scalar_prefetch=1,
        grid=(m // bm,),
        in_specs=[
            pl.BlockSpec((bm, d), lambda i, *_: (i, 0)),
            pl.BlockSpec((bm, LANES), lambda i, *_: (i, 0)),
            pl.BlockSpec((1, d), lambda i, *_: (0, 0)),
            pl.BlockSpec(memory_space=pl.ANY),
        ],
        out_specs=pl.BlockSpec((bm, d), lambda i, *_: (i, 0)),
        scratch_shapes=[pltpu.VMEM((bm, 1, d), F32)] * TOP_K + [pltpu.VMEM((bm, d), F32), pltpu.SemaphoreType.DMA],
    )
    return pl.pallas_call(
        functools.partial(_combine_body, final_norm=final_norm),
        grid_spec=grid_spec,
        out_shape=jax.ShapeDtypeStruct((m, d), F32),
        compiler_params=_cparams("arbitrary"),
    )(dest_flat, x2d, gate4, g.reshape(1, d), ys)


def _routing_tables(idx4, rank4, counts, n_assign, n_exp):
    tm, rpi = MOE_ROW_TILE, MOE_ROWS_PER_ITEM
    padded = (counts + tm - 1) // tm * tm
    pend = jnp.cumsum(padded)
    pstart = pend - padded
    dest = (pstart[idx4] + rank4).astype(jnp.int32).reshape(-1)

    n_rows = n_assign + n_exp * tm
    n_items = n_rows // rpi + n_exp
    per_e = (padded + rpi - 1) // rpi
    icum = jnp.cumsum(per_e)
    ioff = icum - per_e
    total = icum[-1]
    ii = jnp.arange(n_items, dtype=jnp.int32)
    e_of = jnp.minimum(jnp.searchsorted(icum, jnp.minimum(ii, total - 1), side="right"), n_exp - 1)
    e_of = e_of.astype(jnp.int32)
    local = ii - ioff[e_of]
    valid = ii < total
    start = jnp.where(valid, pstart[e_of] + local * rpi, 0).astype(jnp.int32)
    nsub = jnp.where(valid, jnp.clip((padded[e_of] - local * rpi) // tm, 0, rpi // tm), 0).astype(jnp.int32)
    return dest, pend.astype(jnp.int32), padded.astype(jnp.int32), e_of, start, nsub, n_rows


def _alibi_slopes(n):
    start = 2.0 ** (-8.0 / n)
    return np.array([start ** (i + 1) for i in range(n)], dtype=np.float32)


def _glu_lin_permutation():
    p = np.zeros((MXU_DIM, MXU_DIM), np.float32)
    half = MXU_DIM // 2
    for j in range(half):
        p[2 * j, j] = 1.0
        p[2 * j + 1, half + j] = 1.0
    return p


def kernel(x, mem, g_mix, w_in, b_forget, swa_sinks, w_out, g_cross, g_mem, w_cq, w_ckv, w_co, g_moe,
           w_router, b_router, w_mlp1, b_mlp1, w_mlp2, b_mlp2, g_final):
    batch, seq, d = x.shape
    n_mem = mem.shape[1]
    depth = w_in.shape[0]
    n_fox = d // (2 * HEAD_DIM)
    n_swa = d // (2 * HEAD_DIM)
    n_kv = n_swa // SWA_GROUP
    fox_w = n_fox * HEAD_DIM
    swa_q_w = n_swa * HEAD_DIM
    swa_kv_w = n_kv * HEAD_DIM
    o3 = 3 * fox_w
    o4 = o3 + n_fox
    o5 = o4 + swa_q_w
    o6 = o5 + swa_kv_w
    n_exp = w_router.shape[2]
    t = batch * seq
    slopes = jnp.asarray(_alibi_slopes(n_swa))
    perm = jnp.asarray(_glu_lin_permutation(), BF16)

    def dup_heads(wk):
        wk = wk.reshape(d, n_kv, 1, HEAD_DIM)
        return jnp.broadcast_to(wk, (d, n_kv, 2, HEAD_DIM)).reshape(d, n_kv * LANES)

    x2d = x.reshape(t, d)
    for l in range(depth):
        wl = w_in[l]
        w_fox = wl[:, :o3].astype(BF16)
        w_ff_t = wl[:, o3:o4].T.astype(BF16)
        w_swa = jnp.concatenate(
            [wl[:, o4:o5], dup_heads(wl[:, o5:o6]), dup_heads(wl[:, o6:])], axis=1).astype(BF16)

        qkv = _norm_matmul(x2d, g_mix[l], w_fox, BF16, 1024, 1024)
        swa_proj = _norm_matmul(x2d, g_mix[l], w_swa, BF16, 1024, 1024)
        c = _fox_gate(x2d.reshape(batch, seq, d), g_mix[l], w_ff_t, b_forget[l], 512)
        fox = _fox_attention(qkv, c, batch, seq, fox_w // LANES, 512)
        swa = _swa_attention(swa_proj, slopes, swa_sinks[l], batch, seq, swa_q_w // LANES, n_kv)
        x2d = _out_proj(fox, swa, w_out[l].astype(BF16), x2d, 1024, 1024)

        kv = _norm_matmul(mem.reshape(batch * n_mem, d), g_mem[l], w_ckv[l].astype(BF16), BF16, 1024, 1024)
        x2d = _cross_block(x2d, g_cross[l], w_cq[l].astype(BF16), kv, w_co[l].astype(BF16), seq, n_mem, 512)

        w_r = jnp.zeros((d, LANES), BF16).at[:, :n_exp].set(w_router[l].astype(BF16))
        b_r = jnp.full((1, LANES), NEG_BIG, F32).at[0, :n_exp].set(b_router[l])
        idx4, gate4, rank4, cnt = _router(x2d, g_moe[l], w_r, b_r, 512)
        dest, pend, padded, item_e, item_start, item_nsub, n_rows = _routing_tables(
            idx4[:, :TOP_K], rank4[:, :TOP_K], cnt[0, :n_exp], t * TOP_K, n_exp)
        xs = _dispatch(dest, pend, padded, x2d.reshape(t, 1, d), n_rows, 256)
        two_i = w_mlp1.shape[3]
        b1p = b_mlp1[l].reshape(n_exp, two_i // MXU_DIM, MXU_DIM // 2, 2)
        b1p = b1p.transpose(0, 1, 3, 2).reshape(n_exp, 1, two_i)
        ys = _moe_mlp(item_e, item_start, item_nsub, xs, g_moe[l], w_mlp1[l], b1p, w_mlp2[l],
                      b_mlp2[l].reshape(n_exp, 1, d), perm)
        x2d = _combine(dest, x2d, gate4, g_final, ys, 256, final_norm=(l == depth - 1))
    return x2d.reshape(batch, seq, d)
```

```python
import functools

import numpy as np
import jax
import jax.numpy as jnp
from jax import lax
from jax.experimental import pallas as pl
from jax.experimental.pallas import tpu as pltpu

F32 = jnp.float32
BF16 = jnp.bfloat16

RMS_EPS = 1e-5
HEAD_DIM = 64
WINDOW = 128
SWA_GROUP = 4
N_CROSS_HEADS = 4
TOP_K = 4
SWIGLU_ALPHA = 1.702
SWIGLU_LIMIT = 7.0

LANES = 128
MXU_DIM = 256
VMEM_LIMIT_BYTES = 56 * 2**20
NEG_BIG = -1e30

FOX_TILE = 512
SWA_WINDOWS_PER_STEP = 8
MOE_ROW_TILE = 256
MOE_ROWS_PER_ITEM = 2048
MOE_COL_CHUNK = 512


def _cparams(*sem):
    return pltpu.CompilerParams(dimension_semantics=sem, vmem_limit_bytes=VMEM_LIMIT_BYTES)


def _rmsnorm(x, g):
    ms = jnp.mean(x * x, axis=-1, keepdims=True)
    return x * lax.rsqrt(ms + RMS_EPS) * g


def _dot_nt(a, b):
    return lax.dot_general(a, b, (((1,), (1,)), ((), ())), preferred_element_type=F32)


def _norm_matmul_body(x_ref, g_ref, w_ref, o_ref, h_ref, *, transposed):
    @pl.when(pl.program_id(1) == 0)
    def _():
        h_ref[...] = _rmsnorm(x_ref[...], g_ref[...]).astype(h_ref.dtype)

    if transposed:
        o_ref[...] = _dot_nt(w_ref[...], h_ref[...]).astype(o_ref.dtype)
    else:
        o_ref[...] = jnp.dot(h_ref[...], w_ref[...], preferred_element_type=F32).astype(o_ref.dtype)


def _norm_matmul(x2d, g, w, out_dtype, bm, bn, name, transposed=False):
    m, d = x2d.shape
    n = w.shape[0] if transposed else w.shape[1]
    bm, bn = min(bm, m), min(bn, n)
    if transposed:
        w_spec = pl.BlockSpec((bn, d), lambda i, j: (j, 0))
        o_spec = pl.BlockSpec((bn, bm), lambda i, j: (j, i))
        o_shape = (n, m)
    else:
        w_spec = pl.BlockSpec((d, bn), lambda i, j: (0, j))
        o_spec = pl.BlockSpec((bm, bn), lambda i, j: (i, j))
        o_shape = (m, n)
    return pl.pallas_call(
        functools.partial(_norm_matmul_body, transposed=transposed),
        grid=(m // bm, n // bn),
        in_specs=[pl.BlockSpec((bm, d), lambda i, j: (i, 0)), pl.BlockSpec((1, d), lambda i, j: (0, 0)), w_spec],
        out_specs=o_spec,
        out_shape=jax.ShapeDtypeStruct(o_shape, out_dtype),
        scratch_shapes=[pltpu.VMEM((bm, d), BF16)],
        compiler_params=_cparams("parallel", "arbitrary"),
        name=name,
    )(x2d, g.reshape(1, d), w)


def _fox_gate_body(x_ref, g_ref, w_ref, b_ref, c_ref, carry_ref):
    @pl.when(pl.program_id(1) == 0)
    def _():
        carry_ref[...] = jnp.zeros_like(carry_ref)

    bs = x_ref.shape[1]
    h = _rmsnorm(x_ref[0], g_ref[...]).astype(BF16)
    z = jnp.dot(h, w_ref[...], preferred_element_type=F32) + b_ref[...]
    logf = jnp.minimum(z, 0.0) - jnp.log1p(jnp.exp(-jnp.abs(z)))
    r = lax.broadcasted_iota(jnp.int32, (bs, bs), 0)
    c = lax.broadcasted_iota(jnp.int32, (bs, bs), 1)
    tri = (c <= r).astype(F32)
    cs = jnp.dot(tri, logf, preferred_element_type=F32, precision=lax.Precision.HIGHEST)
    cs = cs + carry_ref[0:1, :]
    c_ref[0] = cs
    carry_ref[...] = jnp.broadcast_to(cs[bs - 1:bs, :], carry_ref.shape)


def _fox_gate(x, g, w_pad, b_pad, bs):
    b, s, d = x.shape
    bs = min(bs, s)
    return pl.pallas_call(
        _fox_gate_body,
        grid=(b, s // bs),
        in_specs=[
            pl.BlockSpec((1, bs, d), lambda i, j: (i, j, 0)),
            pl.BlockSpec((1, d), lambda i, j: (0, 0)),
            pl.BlockSpec((d, LANES), lambda i, j: (0, 0)),
            pl.BlockSpec((1, LANES), lambda i, j: (0, 0)),
        ],
        out_specs=pl.BlockSpec((1, bs, LANES), lambda i, j: (i, j, 0)),
        out_shape=jax.ShapeDtypeStruct((b, s, LANES), F32),
        scratch_shapes=[pltpu.VMEM((8, LANES), F32)],
        compiler_params=_cparams("parallel", "arbitrary"),
        name="fox_gate",
    )(x, g.reshape(1, d), w_pad, b_pad)


def _fox_attn_body(itab_ref, jtab_ref, q_ref, k_ref, vt_ref, c_ref, o_ref, m_ref, l_ref, acc_ref):
    p = pl.program_id(1)
    st = pl.program_id(2)
    i = itab_ref[st]
    j = jtab_ref[st]
    tq = q_ref.shape[0]
    tk = k_ref.shape[0]
    lane = lax.broadcasted_iota(jnp.int32, (1, LANES), 1)

    @pl.when(j == 0)
    def _():
        m_ref[...] = jnp.full_like(m_ref, NEG_BIG)
        l_ref[...] = jnp.zeros_like(l_ref)
        acc_ref[...] = jnp.zeros_like(acc_ref)

    def step(masked):
        q = q_ref[...]
        k = k_ref[...]
        vt = vt_ref[...]
        cblk = c_ref[0]
        if masked:
            key = lax.broadcasted_iota(jnp.int32, (tk, tq), 0)
            qry = lax.broadcasted_iota(jnp.int32, (tk, tq), 1)
            causal = key <= qry
        for hh in range(2):
            qh = (jnp.where(lane // HEAD_DIM == hh, q, jnp.zeros_like(q)) * (HEAD_DIM ** -0.5)).astype(BF16)
            c_col = jnp.sum(jnp.where(lane == 2 * p + hh, cblk, 0.0), axis=1, keepdims=True)
            s = _dot_nt(k, qh) - c_col
            if masked:
                s = jnp.where(causal, s, NEG_BIG)
            m_prev = m_ref[hh]
            m_new = jnp.maximum(m_prev, jnp.max(s, axis=0, keepdims=True))
            alpha = jnp.exp(m_prev - m_new)
            pt = jnp.exp(s - m_new)
            l_ref[hh] = alpha * l_ref[hh] + jnp.sum(pt, axis=0, keepdims=True)
            acc_ref[hh] = alpha * acc_ref[hh] + jnp.dot(vt, pt.astype(BF16), preferred_element_type=F32)
            m_ref[hh] = m_new

    @pl.when(j < i)
    def _():
        step(False)

    @pl.when(j == i)
    def _():
        step(True)
        row = lax.broadcasted_iota(jnp.int32, (LANES, 1), 0)
        ot = jnp.where(row < HEAD_DIM, acc_ref[0] / l_ref[0], acc_ref[1] / l_ref[1])
        o_ref[...] = ot.T.astype(o_ref.dtype)


def _fox_attention(qk, vt, c, batch, seq, n_pairs, tile):
    tile = min(tile, seq)
    nq = seq // tile
    pairs = [(i, j) for i in range(nq) for j in range(i + 1)]
    itab = jnp.asarray([ij[0] for ij in pairs], jnp.int32)
    jtab = jnp.asarray([ij[1] for ij in pairs], jnp.int32)
    grid_spec = pltpu.PrefetchScalarGridSpec(
        num_scalar_prefetch=2,
        grid=(batch, n_pairs, len(pairs)),
        in_specs=[
            pl.BlockSpec((tile, LANES), lambda b, p, s, it, jt: (b * nq + it[s], p)),
            pl.BlockSpec((tile, LANES), lambda b, p, s, it, jt: (b * nq + jt[s], n_pairs + p)),
            pl.BlockSpec((LANES, tile), lambda b, p, s, it, jt: (p, b * nq + jt[s])),
            pl.BlockSpec((1, tile, LANES), lambda b, p, s, it, jt: (b, jt[s], 0)),
        ],
        out_specs=pl.BlockSpec((tile, LANES), lambda b, p, s, it, jt: (b * nq + it[s], p)),
        scratch_shapes=[
            pltpu.VMEM((2, 1, tile), F32),
            pltpu.VMEM((2, 1, tile), F32),
            pltpu.VMEM((2, LANES, tile), F32),
        ],
    )
    return pl.pallas_call(
        _fox_attn_body,
        grid_spec=grid_spec,
        out_shape=jax.ShapeDtypeStruct((batch * seq, n_pairs * LANES), BF16),
        compiler_params=_cparams("parallel", "parallel", "arbitrary"),
        name="fox_attention",
    )(itab, jtab, qk, qk, vt, c)


def _swa_body(slopes_ref, sinks_ref, q_ref, kp_ref, kc_ref, vp_ref, vc_ref, o_ref):
    pr = pl.program_id(1)
    n = pl.program_id(2)
    w = WINDOW
    n_win = q_ref.shape[0] // w
    lane = lax.broadcasted_iota(jnp.int32, (1, LANES), 1)
    row = lax.broadcasted_iota(jnp.int32, (2 * w, 1), 0)
    head1 = row >= w
    slope = jnp.where(head1, slopes_ref[2 * pr + 1], slopes_ref[2 * pr])
    sink = jnp.where(head1, sinks_ref[2 * pr + 1], sinks_ref[2 * pr])
    qi = lax.broadcasted_iota(jnp.int32, (2 * w, 2 * w), 0) % w
    kj = lax.broadcasted_iota(jnp.int32, (2 * w, 2 * w), 1)
    rel = qi - kj + w
    in_window = (rel >= 0) & (rel < w)
    bias = slope * rel.astype(F32)
    for wi in range(n_win):
        qw = q_ref[wi * w:(wi + 1) * w, :]
        if wi == 0:
            k = jnp.concatenate([kp_ref[...], kc_ref[0:w, :]], axis=0)
            v = jnp.concatenate([vp_ref[...], vc_ref[0:w, :]], axis=0)
            valid = in_window & (kj >= jnp.where(n > 0, 0, w))
        else:
            k = kc_ref[(wi - 1) * w:(wi + 1) * w, :]
            v = vc_ref[(wi - 1) * w:(wi + 1) * w, :]
            valid = in_window
        q2 = jnp.concatenate([jnp.where(lane < HEAD_DIM, qw, jnp.zeros_like(qw)),
                              jnp.where(lane >= HEAD_DIM, qw, jnp.zeros_like(qw))], axis=0)
        s = _dot_nt(q2, k) * (HEAD_DIM ** -0.5) - bias
        s = jnp.where(valid, s, NEG_BIG)
        m = jnp.maximum(jnp.max(s, axis=-1, keepdims=True), sink)
        p = jnp.exp(s - m)
        denom = jnp.sum(p, axis=-1, keepdims=True) + jnp.exp(sink - m)
        o = jnp.dot((p / denom).astype(BF16), v, preferred_element_type=F32)
        o_ref[wi * w:(wi + 1) * w, :] = jnp.where(lane < HEAD_DIM, o[:w], o[w:]).astype(o_ref.dtype)


def _swa_attention(proj, slopes, sinks, batch, seq, n_pairs, n_kv):
    n_win = min(SWA_WINDOWS_PER_STEP, seq // WINDOW)
    blk = n_win * WINDOW
    nb = seq // blk
    wins = seq // WINDOW

    def q_map(b, p, n):
        return (b * nb + n, p)

    def cur_map(base):
        return lambda b, p, n: (b * nb + n, base + p // 2)

    def prev_map(base):
        return lambda b, p, n: (b * wins + jnp.maximum(n * n_win - 1, 0), base + p // 2)

    smem = pl.BlockSpec(memory_space=pltpu.SMEM)
    return pl.pallas_call(
        _swa_body,
        grid=(batch, n_pairs, nb),
        in_specs=[
            smem, smem,
            pl.BlockSpec((blk, LANES), q_map),
            pl.BlockSpec((WINDOW, LANES), prev_map(n_pairs)),
            pl.BlockSpec((blk, LANES), cur_map(n_pairs)),
            pl.BlockSpec((WINDOW, LANES), prev_map(n_pairs + n_kv)),
            pl.BlockSpec((blk, LANES), cur_map(n_pairs + n_kv)),
        ],
        out_specs=pl.BlockSpec((blk, LANES), q_map),
        out_shape=jax.ShapeDtypeStruct((batch * seq, n_pairs * LANES), BF16),
        compiler_params=_cparams("parallel", "parallel", "parallel"),
        name="swa_attention",
    )(slopes, sinks, proj, proj, proj, proj, proj)


def _out_proj_body(a1_ref, a2_ref, w_ref, x_ref, o_ref):
    a = jnp.concatenate([a1_ref[...], a2_ref[...]], axis=1)
    o_ref[...] = x_ref[...] + jnp.dot(a, w_ref[...], preferred_element_type=F32)


def _out_proj(a1, a2, w, x2d, bm, bn):
    m, d = x2d.shape
    k1, k2 = a1.shape[1], a2.shape[1]
    bm, bn = min(bm, m), min(bn, d)
    return pl.pallas_call(
        _out_proj_body,
        grid=(m // bm, d // bn),
        in_specs=[
            pl.BlockSpec((bm, k1), lambda i, j: (i, 0)),
            pl.BlockSpec((bm, k2), lambda i, j: (i, 0)),
            pl.BlockSpec((k1 + k2, bn), lambda i, j: (0, j)),
            pl.BlockSpec((bm, bn), lambda i, j: (i, j)),
        ],
        out_specs=pl.BlockSpec((bm, bn), lambda i, j: (i, j)),
        out_shape=jax.ShapeDtypeStruct((m, d), F32),
        compiler_params=_cparams("parallel", "parallel"),
        name="out_proj",
    )(a1, a2, w, x2d)


def _cross_body(x_ref, g_ref, wq_ref, kv_ref, wo_ref, o_ref, o3_ref):
    x = x_ref[...]
    h = _rmsnorm(x, g_ref[...]).astype(BF16)
    q = jnp.dot(h, wq_ref[...], preferred_element_type=F32).astype(BF16)
    cw = wq_ref.shape[1]
    hd = cw // N_CROSS_HEADS
    outs = []
    for hh in range(N_CROSS_HEADS):
        qh = q[:, hh * hd:(hh + 1) * hd]
        kh = kv_ref[:, hh * hd:(hh + 1) * hd]
        vh = kv_ref[:, cw + hh * hd:cw + (hh + 1) * hd]
        s = _dot_nt(qh, kh) * (hd ** -0.5)
        e = jnp.exp(s - jnp.max(s, axis=-1, keepdims=True))
        p = (e / jnp.sum(e, axis=-1, keepdims=True)).astype(BF16)
        outs.append(jnp.dot(p, vh, preferred_element_type=F32))
    o = jnp.concatenate(outs, axis=1).astype(BF16)
    o_ref[...] = x + jnp.dot(o, wo_ref[...], preferred_element_type=F32)
    o3_ref[...] = o_ref[...].reshape(o3_ref.shape)


def _cross_block(x2d, g, wq, kv, wo, seq, n_mem, bm):
    m, d = x2d.shape
    bm = min(bm, seq)
    per_b = seq // bm
    cw = wq.shape[1]
    return pl.pallas_call(
        _cross_body,
        grid=(m // bm,),
        in_specs=[
            pl.BlockSpec((bm, d), lambda i: (i, 0)),
            pl.BlockSpec((1, d), lambda i: (0, 0)),
            pl.BlockSpec((d, cw), lambda i: (0, 0)),
            pl.BlockSpec((n_mem, 2 * cw), lambda i: (i // per_b, 0)),
            pl.BlockSpec((cw, d), lambda i: (0, 0)),
        ],
        out_specs=[pl.BlockSpec((bm, d), lambda i: (i, 0)), pl.BlockSpec((bm, 1, d), lambda i: (i, 0, 0))],
        out_shape=[jax.ShapeDtypeStruct((m, d), F32), jax.ShapeDtypeStruct((m, 1, d), F32)],
        compiler_params=_cparams("parallel"),
        name="cross_block",
    )(x2d, g.reshape(1, d), wq, kv, wo)


def _router_body(x_ref, g_ref, w_ref, b_ref, idx_ref, gate_ref, rank_ref, cnt_ref, carry_ref):
    @pl.when(pl.program_id(0) == 0)
    def _():
        carry_ref[...] = jnp.zeros_like(carry_ref)

    bm = x_ref.shape[0]
    h = _rmsnorm(x_ref[...], g_ref[...]).astype(BF16)
    logits = jnp.dot(h, w_ref[...], preferred_element_type=F32) + b_ref[...]
    lane = lax.broadcasted_iota(jnp.int32, (bm, LANES), 1)
    lane_f = lane.astype(F32)
    work = logits
    self32 = jnp.zeros((bm, LANES), F32)
    vals, picks, hots = [], [], []
    for _ in range(TOP_K):
        mk = jnp.max(work, axis=-1, keepdims=True)
        pick = jnp.min(jnp.where(work == mk, lane_f, float(LANES)), axis=-1, keepdims=True)
        hot = lane_f == pick
        vals.append(mk)
        picks.append(pick.astype(jnp.int32))
        hots.append(hot)
        self32 = jnp.where(hot, 1.0, self32)
        work = jnp.where(hot, -jnp.inf, work)
    es = [jnp.exp(v - vals[0]) for v in vals]
    denom = es[0] + es[1] + es[2] + es[3]

    r = lax.broadcasted_iota(jnp.int32, (bm, bm), 0)
    c = lax.broadcasted_iota(jnp.int32, (bm, bm), 1)
    tri = (c < r).astype(BF16)
    rank = jnp.dot(tri, self32.astype(BF16), preferred_element_type=F32) + carry_ref[0:1, :]
    carry_new = carry_ref[0:1, :] + jnp.sum(self32, axis=0, keepdims=True)
    carry_ref[...] = jnp.broadcast_to(carry_new, carry_ref.shape)

    idx_o = jnp.zeros((bm, LANES), jnp.int32)
    gate_o = jnp.zeros((bm, LANES), F32)
    rank_o = jnp.zeros((bm, LANES), jnp.int32)
    for kk in range(TOP_K):
        rk = jnp.sum(jnp.where(hots[kk], rank, 0.0), axis=-1, keepdims=True).astype(jnp.int32)
        idx_o = jnp.where(lane == kk, picks[kk], idx_o)
        gate_o = jnp.where(lane == kk, es[kk] / denom, gate_o)
        rank_o = jnp.where(lane == kk, rk, rank_o)
    idx_ref[...] = idx_o
    gate_ref[...] = gate_o
    rank_ref[...] = rank_o
    cnt_ref[...] = jnp.broadcast_to(carry_new, cnt_ref.shape).astype(jnp.int32)


def _router(x2d, g, w_pad, b_pad, bm):
    m, d = x2d.shape
    bm = min(bm, m)
    tok = pl.BlockSpec((bm, LANES), lambda i: (i, 0))
    return pl.pallas_call(
        _router_body,
        grid=(m // bm,),
        in_specs=[
            pl.BlockSpec((bm, d), lambda i: (i, 0)),
            pl.BlockSpec((1, d), lambda i: (0, 0)),
            pl.BlockSpec((d, LANES), lambda i: (0, 0)),
            pl.BlockSpec((1, LANES), lambda i: (0, 0)),
        ],
        out_specs=[tok, tok, tok, pl.BlockSpec((8, LANES), lambda i: (0, 0))],
        out_shape=[
            jax.ShapeDtypeStruct((m, LANES), jnp.int32),
            jax.ShapeDtypeStruct((m, LANES), F32),
            jax.ShapeDtypeStruct((m, LANES), jnp.int32),
            jax.ShapeDtypeStruct((8, LANES), jnp.int32),
        ],
        scratch_shapes=[pltpu.VMEM((8, LANES), F32)],
        compiler_params=_cparams("arbitrary"),
        name="router",
    )(x2d, g.reshape(1, d), w_pad, b_pad)


def _dispatch_body(dest_ref, pend_ref, padded_ref, x_ref, xs_ref, z_ref, sem, zsem):
    i = pl.program_id(0)
    bm = x_ref.shape[0]
    n_exp = pend_ref.shape[0]
    tm = z_ref.shape[0]

    def zero_copy(e):
        return pltpu.make_async_copy(z_ref, xs_ref.at[pl.ds(pend_ref[e] - tm, tm)], zsem)

    @pl.when(i == 0)
    def _():
        z_ref[...] = jnp.zeros_like(z_ref)
        for e in range(n_exp):
            @pl.when(padded_ref[e] > 0)
            def _():
                zero_copy(e).start()
        for e in range(n_exp):
            @pl.when(padded_ref[e] > 0)
            def _():
                zero_copy(e).wait()

    def row_copy(r, kk):
        d = dest_ref[(i * bm + r) * TOP_K + kk]
        return pltpu.make_async_copy(x_ref.at[r], xs_ref.at[d], sem)

    def issue(r, carry):
        for kk in range(TOP_K):
            row_copy(r, kk).start()
        return carry

    def drain(r, carry):
        for kk in range(TOP_K):
            row_copy(r, kk).wait()
        return carry

    lax.fori_loop(0, bm, issue, 0)
    lax.fori_loop(0, bm, drain, 0)


def _dispatch(dest_flat, pend, padded, x3d, n_rows, bm):
    m, _, d = x3d.shape
    bm = min(bm, m)
    grid_spec = pltpu.PrefetchScalarGridSpec(
        num_scalar_prefetch=3,
        grid=(m // bm,),
        in_specs=[pl.BlockSpec((bm, 1, d), lambda i, *_: (i, 0, 0))],
        out_specs=pl.BlockSpec(memory_space=pl.ANY),
        scratch_shapes=[
            pltpu.VMEM((MOE_ROW_TILE, 1, d), F32),
            pltpu.SemaphoreType.DMA,
            pltpu.SemaphoreType.DMA,
        ],
    )
    return pl.pallas_call(
        _dispatch_body,
        grid_spec=grid_spec,
        out_shape=jax.ShapeDtypeStruct((n_rows, 1, d), F32),
        compiler_params=_cparams("arbitrary"),
        name="moe_dispatch",
    )(dest_flat, pend, padded, x3d)


def _moe_body(ie_ref, istart_ref, insub_ref, xs_ref, g_ref, w1_ref, b1_ref, w2_ref, b2_ref, perm_ref,
              ys_ref, xbf_ref, yacc_ref, st0_ref, st1_ref, flat_ref, w1p_ref, w2b_ref, sems):
    it = pl.program_id(0)
    c = pl.program_id(1)
    n_chunks = pl.num_programs(1)
    nsub = insub_ref[it]
    start = istart_ref[it]
    tm = flat_ref.shape[0]
    tc = w1_ref.shape[2]
    half = MXU_DIM // 2
    stage = (st0_ref, st1_ref)
    n_pairs = (nsub + 1) // 2

    def rows(s):
        return pl.ds(pl.multiple_of(s * tm, tm), tm)

    def load_copy(s, k):
        return pltpu.make_async_copy(xs_ref.at[pl.ds(start + s * tm, tm)], stage[k], sems.at[k])

    def store_copy(s, k):
        return pltpu.make_async_copy(stage[k], ys_ref.at[pl.ds(start + s * tm, tm)], sems.at[k])

    def consume(s, k):
        flat_ref[...] = stage[k][...].reshape(flat_ref.shape)
        xbf_ref[rows(s), :] = _rmsnorm(flat_ref[...], g_ref[...]).astype(BF16)
        yacc_ref[rows(s), :] = jnp.zeros((tm, yacc_ref.shape[1]), F32)

    def produce(s, k):
        flat_ref[...] = yacc_ref[rows(s), :] + b2_ref[0]
        stage[k][...] = flat_ref[...].reshape(stage[k].shape)

    @pl.when(nsub > 0)
    def _():
        @pl.when(c == 0)
        def _():
            load_copy(0, 0).start()

            def load_pair(sp, carry):
                s0 = 2 * sp
                s1 = s0 + 1

                @pl.when(s1 < nsub)
                def _():
                    load_copy(s1, 1).start()
                load_copy(s0, 0).wait()
                consume(s0, 0)

                @pl.when(s1 < nsub)
                def _():
                    @pl.when(s1 + 1 < nsub)
                    def _():
                        load_copy(s1 + 1, 0).start()
                    load_copy(s1, 1).wait()
                    consume(s1, 1)
                return carry
            lax.fori_loop(0, n_pairs, load_pair, 0)

        for b in range(tc // MXU_DIM):
            blk = w1_ref[0, :, b * MXU_DIM:(b + 1) * MXU_DIM].astype(BF16)
            w1p_ref[:, b * MXU_DIM:(b + 1) * MXU_DIM] = jnp.dot(
                blk, perm_ref[...], preferred_element_type=F32).astype(BF16)
        w2b_ref[...] = w2_ref[0].astype(BF16)

        def compute(s, carry):
            u = jnp.dot(xbf_ref[rows(s), :], w1p_ref[...], preferred_element_type=F32) + b1_ref[0]
            parts = []
            for b in range(tc // MXU_DIM):
                glu = jnp.minimum(u[:, b * MXU_DIM:b * MXU_DIM + half], SWIGLU_LIMIT)
                lin = jnp.clip(u[:, b * MXU_DIM + half:(b + 1) * MXU_DIM], -SWIGLU_LIMIT, SWIGLU_LIMIT)
                sig = 1.0 / (1.0 + jnp.exp(-SWIGLU_ALPHA * glu))
                parts.append((glu * sig * (lin + 1.0)).astype(BF16))
            a = jnp.concatenate(parts, axis=1)
            yacc_ref[rows(s), :] += jnp.dot(a, w2b_ref[...], preferred_element_type=F32)
            return carry
        lax.fori_loop(0, nsub, compute, 0)

        @pl.when(c == n_chunks - 1)
        def _():
            def store_pair(sp, carry):
                s0 = 2 * sp
                s1 = s0 + 1

                @pl.when(sp > 0)
                def _():
                    store_copy(s0 - 2, 0).wait()
                produce(s0, 0)
                store_copy(s0, 0).start()

                @pl.when(s1 < nsub)
                def _():
                    @pl.when(sp > 0)
                    def _():
                        store_copy(s1 - 2, 1).wait()
                    produce(s1, 1)
                    store_copy(s1, 1).start()
                return carry
            lax.fori_loop(0, n_pairs, store_pair, 0)
            store_copy(2 * (n_pairs - 1), 0).wait()

            @pl.when(nsub > 1)
            def _():
                store_copy(2 * ((nsub - 2) // 2) + 1, 1).wait()


def _moe_mlp(item_e, item_start, item_nsub, xs, g, w1, b1p, w2, b2, perm):
    n_rows, _, d = xs.shape
    n_exp, _, two_i = w1.shape
    tc = min(MOE_COL_CHUNK, two_i)
    n_chunks = two_i // tc
    n_items = item_e.shape[0]
    last = n_chunks - 1

    def chunk(c, it, insub):
        return jnp.where(insub[it] > 0, c, last)

    grid_spec = pltpu.PrefetchScalarGridSpec(
        num_scalar_prefetch=3,
        grid=(n_items, n_chunks),
        in_specs=[
            pl.BlockSpec(memory_space=pl.ANY),
            pl.BlockSpec((1, d), lambda it, c, ie, ist, ins: (0, 0)),
            pl.BlockSpec((1, d, tc), lambda it, c, ie, ist, ins: (ie[it], 0, chunk(c, it, ins))),
            pl.BlockSpec((1, 1, tc), lambda it, c, ie, ist, ins: (ie[it], 0, chunk(c, it, ins))),
            pl.BlockSpec((1, tc // 2, d), lambda it, c, ie, ist, ins: (ie[it], chunk(c, it, ins), 0)),
            pl.BlockSpec((1, 1, d), lambda it, c, ie, ist, ins: (ie[it], 0, 0)),
            pl.BlockSpec((MXU_DIM, MXU_DIM), lambda it, c, ie, ist, ins: (0, 0)),
        ],
        out_specs=pl.BlockSpec(memory_space=pl.ANY),
        scratch_shapes=[
            pltpu.VMEM((MOE_ROWS_PER_ITEM, d), BF16),
            pltpu.VMEM((MOE_ROWS_PER_ITEM, d), F32),
            pltpu.VMEM((MOE_ROW_TILE, 1, d), F32),
            pltpu.VMEM((MOE_ROW_TILE, 1, d), F32),
            pltpu.VMEM((MOE_ROW_TILE, d), F32),
            pltpu.VMEM((d, tc), BF16),
            pltpu.VMEM((tc // 2, d), BF16),
            pltpu.SemaphoreType.DMA((2,)),
        ],
    )
    return pl.pallas_call(
        _moe_body,
        grid_spec=grid_spec,
        out_shape=jax.ShapeDtypeStruct((n_rows, 1, d), F32),
        compiler_params=_cparams("arbitrary", "arbitrary"),
        name="moe_mlp",
    )(item_e, item_start, item_nsub, xs, g.reshape(1, d), w1, b1p, w2, b2, perm)


def _combine_body(dest_ref, x_ref, gate_ref, g_ref, ys_ref, o_ref, b0, b1, b2, b3, flat_ref, sem, *, final_norm):
    i = pl.program_id(0)
    bm = x_ref.shape[0]
    bufs = (b0, b1, b2, b3)

    def row_copy(r, kk):
        d = dest_ref[(i * bm + r) * TOP_K + kk]
        return pltpu.make_async_copy(ys_ref.at[d], bufs[kk].at[r], sem)

    def issue(r, carry):
        for kk in range(TOP_K):
            row_copy(r, kk).start()
        return carry

    def drain(r, carry):
        for kk in range(TOP_K):
            row_copy(r, kk).wait()
        return carry

    lax.fori_loop(0, bm, issue, 0)
    lax.fori_loop(0, bm, drain, 0)
    acc = x_ref[...]
    for kk in range(TOP_K):
        flat_ref[...] = bufs[kk][...].reshape(flat_ref.shape)
        acc = acc + gate_ref[:, kk:kk + 1] * flat_ref[...]
    o_ref[...] = _rmsnorm(acc, g_ref[...]) if final_norm else acc


def _combine(dest_flat, x2d, gate4, g, ys, bm, final_norm):
    m, d = x2d.shape
    bm = min(bm, m)
    grid_spec = pltpu.PrefetchScalarGridSpec(
        num_scalar_prefetch=1,
        grid=(m // bm,),
        in_specs=[
            pl.BlockSpec((bm, d), lambda i, *_: (i, 0)),
            pl.BlockSpec((bm, LANES), lambda i, *_: (i, 0)),
            pl.BlockSpec((1, d), lambda i, *_: (0, 0)),
            pl.BlockSpec(memory_space=pl.ANY),
        ],
        out_specs=pl.BlockSpec((bm, d), lambda i, *_: (i, 0)),
        scratch_shapes=[pltpu.VMEM((bm, 1, d), F32)] * TOP_K + [pltpu.VMEM((bm, d), F32), pltpu.SemaphoreType.DMA],
    )
    return pl.pallas_call(
        functools.partial(_combine_body, final_norm=final_norm),
        grid_spec=grid_spec,
        out_shape=jax.ShapeDtypeStruct((m, d), F32),
        compiler_params=_cparams("arbitrary"),
        name="moe_combine",
    )(dest_flat, x2d, gate4, g.reshape(1, d), ys)


def _routing_tables(idx4, rank4, counts, n_assign, n_exp):
    tm, rpi = MOE_ROW_TILE, MOE_ROWS_PER_ITEM
    padded = (counts + tm - 1) // tm * tm
    pend = jnp.cumsum(padded)
    pstart = pend - padded
    dest = (pstart[idx4] + rank4).astype(jnp.int32).reshape(-1)

    n_rows = n_assign + n_exp * tm
    n_items = n_rows // rpi + n_exp
    per_e = (padded + rpi - 1) // rpi
    icum = jnp.cumsum(per_e)
    ioff = icum - per_e
    total = icum[-1]
    ii = jnp.arange(n_items, dtype=jnp.int32)
    ic = jnp.minimum(ii, total - 1)
    e_of = jnp.sum((icum[None, :] <= ic[:, None]).astype(jnp.int32), axis=1)
    e_of = jnp.minimum(e_of, n_exp - 1)
    local = ii - ioff[e_of]
    valid = ii < total
    start = jnp.where(valid, pstart[e_of] + local * rpi, 0).astype(jnp.int32)
    nsub = jnp.where(valid, jnp.clip((padded[e_of] - local * rpi) // tm, 0, rpi // tm), 0).astype(jnp.int32)
    return dest, pend.astype(jnp.int32), padded.astype(jnp.int32), e_of, start, nsub, n_rows


def _alibi_slopes(n):
    start = 2.0 ** (-8.0 / n)
    return np.array([start ** (i + 1) for i in range(n)], dtype=np.float32)


def _glu_lin_permutation():
    p = np.zeros((MXU_DIM, MXU_DIM), np.float32)
    half = MXU_DIM // 2
    for j in range(half):
        p[2 * j, j] = 1.0
        p[2 * j + 1, half + j] = 1.0
    return p


def kernel(x, mem, g_mix, w_in, b_forget, swa_sinks, w_out, g_cross, g_mem, w_cq, w_ckv, w_co, g_moe,
           w_router, b_router, w_mlp1, b_mlp1, w_mlp2, b_mlp2, g_final):
    batch, seq, d = x.shape
    n_mem = mem.shape[1]
    depth = w_in.shape[0]
    n_fox = d // (2 * HEAD_DIM)
    n_swa = d // (2 * HEAD_DIM)
    n_kv = n_swa // SWA_GROUP
    fox_w = n_fox * HEAD_DIM
    swa_q_w = n_swa * HEAD_DIM
    swa_kv_w = n_kv * HEAD_DIM
    o2 = 2 * fox_w
    o3 = 3 * fox_w
    o4 = o3 + n_fox
    o5 = o4 + swa_q_w
    o6 = o5 + swa_kv_w
    n_exp = w_router.shape[2]
    t = batch * seq
    slopes = jnp.asarray(_alibi_slopes(n_swa))
    perm = jnp.asarray(_glu_lin_permutation(), BF16)

    def dup_heads(wk):
        wk = wk.reshape(d, n_kv, 1, HEAD_DIM)
        return jnp.broadcast_to(wk, (d, n_kv, 2, HEAD_DIM)).reshape(d, n_kv * LANES)

    x2d = x.reshape(t, d)
    for l in range(depth):
        wl = w_in[l]
        w_qk = wl[:, :o2].astype(BF16)
        w_vt = wl[:, o2:o3].T.astype(BF16)
        w_ff = jnp.zeros((d, LANES), BF16).at[:, :n_fox].set(wl[:, o3:o4].astype(BF16))
        b_ff = jnp.zeros((1, LANES), F32).at[0, :n_fox].set(b_forget[l])
        w_swa = jnp.concatenate(
            [wl[:, o4:o5], dup_heads(wl[:, o5:o6]), dup_heads(wl[:, o6:])], axis=1).astype(BF16)

        qk = _norm_matmul(x2d, g_mix[l], w_qk, BF16, 1024, 1024, "fox_qk_proj")
        vt = _norm_matmul(x2d, g_mix[l], w_vt, BF16, 1024, 1024, "fox_vt_proj", transposed=True)
        swa_proj = _norm_matmul(x2d, g_mix[l], w_swa, BF16, 1024, 1024, "swa_proj")
        c = _fox_gate(x2d.reshape(batch, seq, d), g_mix[l], w_ff, b_ff, 512)
        fox = _fox_attention(qk, vt, c, batch, seq, fox_w // LANES, FOX_TILE)
        swa = _swa_attention(swa_proj, slopes, swa_sinks[l], batch, seq, swa_q_w // LANES, n_kv)
        x2d = _out_proj(fox, swa, w_out[l].astype(BF16), x2d, 1024, 1024)

        kv = _norm_matmul(mem.reshape(batch * n_mem, d), g_mem[l], w_ckv[l].astype(BF16), BF16, 1024, 1024,
                          "mem_kv_proj")
        x2d, x3d = _cross_block(x2d, g_cross[l], w_cq[l].astype(BF16), kv, w_co[l].astype(BF16), seq, n_mem, 512)

        w_r = jnp.zeros((d, LANES), BF16).at[:, :n_exp].set(w_router[l].astype(BF16))
        b_r = jnp.full((1, LANES), NEG_BIG, F32).at[0, :n_exp].set(b_router[l])
        idx4, gate4, rank4, cnt = _router(x2d, g_moe[l], w_r, b_r, 512)
        dest, pend, padded, item_e, item_start, item_nsub, n_rows = _routing_tables(
            idx4[:, :TOP_K], rank4[:, :TOP_K], cnt[0, :n_exp], t * TOP_K, n_exp)
        xs = _dispatch(dest, pend, padded, x3d, n_rows, 256)
        two_i = w_mlp1.shape[3]
        b1p = b_mlp1[l].reshape(n_exp, two_i // MXU_DIM, MXU_DIM // 2, 2)
        b1p = b1p.transpose(0, 1, 3, 2).reshape(n_exp, 1, two_i)
        ys = _moe_mlp(item_e, item_start, item_nsub, xs, g_moe[l], w_mlp1[l], b1p, w_mlp2[l],
                      b_mlp2[l].reshape(n_exp, 1, d), perm)
        x2d = _combine(dest, x2d, gate4, g_final, ys, 256, final_norm=(l == depth - 1))
    return x2d.reshape(batch, seq, d)
```

```python
import functools

import numpy as np
import jax
import jax.numpy as jnp
from jax import lax
from jax.experimental import pallas as pl
from jax.experimental.pallas import tpu as pltpu

F32 = jnp.float32
BF16 = jnp.bfloat16

RMS_EPS = 1e-5
HEAD_DIM = 64
WINDOW = 128
SWA_GROUP = 4
N_CROSS_HEADS = 4
TOP_K = 4
SWIGLU_ALPHA = 1.702
SWIGLU_LIMIT = 7.0

LANES = 128
MXU_DIM = 256
VMEM_LIMIT_BYTES = 56 * 2**20
NEG_BIG = -1e30

FOX_TILE = 512
SWA_WINDOWS_PER_STEP = 8
MOE_ROW_TILE = 256
MOE_ROWS_PER_ITEM = 2304
MOE_COL_CHUNK = 512


def _cparams(*sem):
    return pltpu.CompilerParams(dimension_semantics=sem, vmem_limit_bytes=VMEM_LIMIT_BYTES)


def _rmsnorm(x, g):
    ms = jnp.mean(x * x, axis=-1, keepdims=True)
    return x * lax.rsqrt(ms + RMS_EPS) * g


def _dot_nt(a, b):
    return lax.dot_general(a, b, (((1,), (1,)), ((), ())), preferred_element_type=F32)


def _norm_matmul_body(x_ref, g_ref, w_ref, o_ref, h_ref, *, transposed):
    @pl.when(pl.program_id(1) == 0)
    def _():
        h_ref[...] = _rmsnorm(x_ref[...], g_ref[...]).astype(h_ref.dtype)

    if transposed:
        o_ref[...] = _dot_nt(w_ref[...], h_ref[...]).astype(o_ref.dtype)
    else:
        o_ref[...] = jnp.dot(h_ref[...], w_ref[...], preferred_element_type=F32).astype(o_ref.dtype)


def _norm_matmul(x2d, g, w, out_dtype, bm, bn, name, transposed=False):
    m, d = x2d.shape
    n = w.shape[0] if transposed else w.shape[1]
    bm, bn = min(bm, m), min(bn, n)
    if transposed:
        w_spec = pl.BlockSpec((bn, d), lambda i, j: (j, 0))
        o_spec = pl.BlockSpec((bn, bm), lambda i, j: (j, i))
        o_shape = (n, m)
    else:
        w_spec = pl.BlockSpec((d, bn), lambda i, j: (0, j))
        o_spec = pl.BlockSpec((bm, bn), lambda i, j: (i, j))
        o_shape = (m, n)
    return pl.pallas_call(
        functools.partial(_norm_matmul_body, transposed=transposed),
        grid=(m // bm, n // bn),
        in_specs=[pl.BlockSpec((bm, d), lambda i, j: (i, 0)), pl.BlockSpec((1, d), lambda i, j: (0, 0)), w_spec],
        out_specs=o_spec,
        out_shape=jax.ShapeDtypeStruct(o_shape, out_dtype),
        scratch_shapes=[pltpu.VMEM((bm, d), BF16)],
        compiler_params=_cparams("parallel", "arbitrary"),
        name=name,
    )(x2d, g.reshape(1, d), w)


def _fox_gate_body(x_ref, g_ref, w_ref, b_ref, c_ref, carry_ref):
    @pl.when(pl.program_id(1) == 0)
    def _():
        carry_ref[...] = jnp.zeros_like(carry_ref)

    bs = x_ref.shape[1]
    h = _rmsnorm(x_ref[0], g_ref[...]).astype(BF16)
    z = jnp.dot(h, w_ref[...], preferred_element_type=F32) + b_ref[...]
    logf = jnp.minimum(z, 0.0) - jnp.log1p(jnp.exp(-jnp.abs(z)))
    r = lax.broadcasted_iota(jnp.int32, (bs, bs), 0)
    c = lax.broadcasted_iota(jnp.int32, (bs, bs), 1)
    tri = (c <= r).astype(F32)
    cs = jnp.dot(tri, logf, preferred_element_type=F32, precision=lax.Precision.HIGHEST)
    cs = cs + carry_ref[0:1, :]
    c_ref[0] = cs
    carry_ref[...] = jnp.broadcast_to(cs[bs - 1:bs, :], carry_ref.shape)


def _fox_gate(x, g, w_pad, b_pad, bs):
    b, s, d = x.shape
    bs = min(bs, s)
    return pl.pallas_call(
        _fox_gate_body,
        grid=(b, s // bs),
        in_specs=[
            pl.BlockSpec((1, bs, d), lambda i, j: (i, j, 0)),
            pl.BlockSpec((1, d), lambda i, j: (0, 0)),
            pl.BlockSpec((d, LANES), lambda i, j: (0, 0)),
            pl.BlockSpec((1, LANES), lambda i, j: (0, 0)),
        ],
        out_specs=pl.BlockSpec((1, bs, LANES), lambda i, j: (i, j, 0)),
        out_shape=jax.ShapeDtypeStruct((b, s, LANES), F32),
        scratch_shapes=[pltpu.VMEM((8, LANES), F32)],
        compiler_params=_cparams("parallel", "arbitrary"),
        name="fox_gate",
    )(x, g.reshape(1, d), w_pad, b_pad)


def _fox_attn_body(itab_ref, jtab_ref, q_ref, k_ref, vt_ref, c_ref, o_ref, m_ref, l_ref, acc_ref):
    p = pl.program_id(1)
    st = pl.program_id(2)
    i = itab_ref[st]
    j = jtab_ref[st]
    tq = q_ref.shape[0]
    tk = k_ref.shape[0]
    lane = lax.broadcasted_iota(jnp.int32, (1, LANES), 1)

    @pl.when(j == 0)
    def _():
        m_ref[...] = jnp.full_like(m_ref, NEG_BIG)
        l_ref[...] = jnp.zeros_like(l_ref)
        acc_ref[...] = jnp.zeros_like(acc_ref)

    def step(masked):
        q = q_ref[...]
        k = k_ref[...]
        vt = vt_ref[...]
        cblk = c_ref[0]
        if masked:
            key = lax.broadcasted_iota(jnp.int32, (tk, tq), 0)
            qry = lax.broadcasted_iota(jnp.int32, (tk, tq), 1)
            causal = key <= qry
        for hh in range(2):
            qh = (jnp.where(lane // HEAD_DIM == hh, q, jnp.zeros_like(q)) * (HEAD_DIM ** -0.5)).astype(BF16)
            c_col = jnp.sum(jnp.where(lane == 2 * p + hh, cblk, 0.0), axis=1, keepdims=True)
            s = _dot_nt(k, qh) - c_col
            if masked:
                s = jnp.where(causal, s, NEG_BIG)
            m_prev = m_ref[hh]
            m_new = jnp.maximum(m_prev, jnp.max(s, axis=0, keepdims=True))
            alpha = jnp.exp(m_prev - m_new)
            pt = jnp.exp(s - m_new)
            l_ref[hh] = alpha * l_ref[hh] + jnp.sum(pt, axis=0, keepdims=True)
            acc_ref[hh] = alpha * acc_ref[hh] + jnp.dot(vt, pt.astype(BF16), preferred_element_type=F32)
            m_ref[hh] = m_new

    @pl.when(j < i)
    def _():
        step(False)

    @pl.when(j == i)
    def _():
        step(True)
        row = lax.broadcasted_iota(jnp.int32, (LANES, 1), 0)
        ot = jnp.where(row < HEAD_DIM, acc_ref[0] / l_ref[0], acc_ref[1] / l_ref[1])
        o_ref[...] = ot.T.astype(o_ref.dtype)


def _fox_attention(qk, vt, c, batch, seq, n_pairs, tile):
    tile = min(tile, seq)
    nq = seq // tile
    pairs = [(i, j) for i in range(nq) for j in range(i + 1)]
    itab = jnp.asarray([ij[0] for ij in pairs], jnp.int32)
    jtab = jnp.asarray([ij[1] for ij in pairs], jnp.int32)
    grid_spec = pltpu.PrefetchScalarGridSpec(
        num_scalar_prefetch=2,
        grid=(batch, n_pairs, len(pairs)),
        in_specs=[
            pl.BlockSpec((tile, LANES), lambda b, p, s, it, jt: (b * nq + it[s], p)),
            pl.BlockSpec((tile, LANES), lambda b, p, s, it, jt: (b * nq + jt[s], n_pairs + p)),
            pl.BlockSpec((LANES, tile), lambda b, p, s, it, jt: (p, b * nq + jt[s])),
            pl.BlockSpec((1, tile, LANES), lambda b, p, s, it, jt: (b, jt[s], 0)),
        ],
        out_specs=pl.BlockSpec((tile, LANES), lambda b, p, s, it, jt: (b * nq + it[s], p)),
        scratch_shapes=[
            pltpu.VMEM((2, 1, tile), F32),
            pltpu.VMEM((2, 1, tile), F32),
            pltpu.VMEM((2, LANES, tile), F32),
        ],
    )
    return pl.pallas_call(
        _fox_attn_body,
        grid_spec=grid_spec,
        out_shape=jax.ShapeDtypeStruct((batch * seq, n_pairs * LANES), BF16),
        compiler_params=_cparams("parallel", "parallel", "arbitrary"),
        name="fox_attention",
    )(itab, jtab, qk, qk, vt, c)


def _swa_body(slopes_ref, sinks_ref, q_ref, kp_ref, kc_ref, vp_ref, vc_ref, o_ref):
    pr = pl.program_id(1)
    n = pl.program_id(2)
    w = WINDOW
    n_win = q_ref.shape[0] // w
    lane = lax.broadcasted_iota(jnp.int32, (1, LANES), 1)
    row = lax.broadcasted_iota(jnp.int32, (2 * w, 1), 0)
    head1 = row >= w
    slope = jnp.where(head1, slopes_ref[2 * pr + 1], slopes_ref[2 * pr])
    sink = jnp.where(head1, sinks_ref[2 * pr + 1], sinks_ref[2 * pr])
    qi = lax.broadcasted_iota(jnp.int32, (2 * w, 2 * w), 0) % w
    kj = lax.broadcasted_iota(jnp.int32, (2 * w, 2 * w), 1)
    rel = qi - kj + w
    in_window = (rel >= 0) & (rel < w)
    bias = slope * rel.astype(F32)
    for wi in range(n_win):
        qw = q_ref[wi * w:(wi + 1) * w, :]
        if wi == 0:
            k = jnp.concatenate([kp_ref[...], kc_ref[0:w, :]], axis=0)
            v = jnp.concatenate([vp_ref[...], vc_ref[0:w, :]], axis=0)
            valid = in_window & (kj >= jnp.where(n > 0, 0, w))
        else:
            k = kc_ref[(wi - 1) * w:(wi + 1) * w, :]
            v = vc_ref[(wi - 1) * w:(wi + 1) * w, :]
            valid = in_window
        q2 = jnp.concatenate([jnp.where(lane < HEAD_DIM, qw, jnp.zeros_like(qw)),
                              jnp.where(lane >= HEAD_DIM, qw, jnp.zeros_like(qw))], axis=0)
        s = _dot_nt(q2, k) * (HEAD_DIM ** -0.5) - bias
        s = jnp.where(valid, s, NEG_BIG)
        m = jnp.maximum(jnp.max(s, axis=-1, keepdims=True), sink)
        p = jnp.exp(s - m)
        denom = jnp.sum(p, axis=-1, keepdims=True) + jnp.exp(sink - m)
        o = jnp.dot((p / denom).astype(BF16), v, preferred_element_type=F32)
        o_ref[wi * w:(wi + 1) * w, :] = jnp.where(lane < HEAD_DIM, o[:w], o[w:]).astype(o_ref.dtype)


def _swa_attention(proj, slopes, sinks, batch, seq, n_pairs, n_kv):
    n_win = min(SWA_WINDOWS_PER_STEP, seq // WINDOW)
    blk = n_win * WINDOW
    nb = seq // blk
    wins = seq // WINDOW

    def q_map(b, p, n):
        return (b * nb + n, p)

    def cur_map(base):
        return lambda b, p, n: (b * nb + n, base + p // 2)

    def prev_map(base):
        return lambda b, p, n: (b * wins + jnp.maximum(n * n_win - 1, 0), base + p // 2)

    smem = pl.BlockSpec(memory_space=pltpu.SMEM)
    return pl.pallas_call(
        _swa_body,
        grid=(batch, n_pairs, nb),
        in_specs=[
            smem, smem,
            pl.BlockSpec((blk, LANES), q_map),
            pl.BlockSpec((WINDOW, LANES), prev_map(n_pairs)),
            pl.BlockSpec((blk, LANES), cur_map(n_pairs)),
            pl.BlockSpec((WINDOW, LANES), prev_map(n_pairs + n_kv)),
            pl.BlockSpec((blk, LANES), cur_map(n_pairs + n_kv)),
        ],
        out_specs=pl.BlockSpec((blk, LANES), q_map),
        out_shape=jax.ShapeDtypeStruct((batch * seq, n_pairs * LANES), BF16),
        compiler_params=_cparams("parallel", "parallel", "parallel"),
        name="swa_attention",
    )(slopes, sinks, proj, proj, proj, proj, proj)


def _out_proj_body(a1_ref, a2_ref, w_ref, x_ref, o_ref):
    a = jnp.concatenate([a1_ref[...], a2_ref[...]], axis=1)
    o_ref[...] = x_ref[...] + jnp.dot(a, w_ref[...], preferred_element_type=F32)


def _out_proj(a1, a2, w, x2d, bm, bn):
    m, d = x2d.shape
    k1, k2 = a1.shape[1], a2.shape[1]
    bm, bn = min(bm, m), min(bn, d)
    return pl.pallas_call(
        _out_proj_body,
        grid=(m // bm, d // bn),
        in_specs=[
            pl.BlockSpec((bm, k1), lambda i, j: (i, 0)),
            pl.BlockSpec((bm, k2), lambda i, j: (i, 0)),
            pl.BlockSpec((k1 + k2, bn), lambda i, j: (0, j)),
            pl.BlockSpec((bm, bn), lambda i, j: (i, j)),
        ],
        out_specs=pl.BlockSpec((bm, bn), lambda i, j: (i, j)),
        out_shape=jax.ShapeDtypeStruct((m, d), F32),
        compiler_params=_cparams("parallel", "parallel"),
        name="out_proj",
    )(a1, a2, w, x2d)


def _cross_body(x_ref, g_ref, wq_ref, kv_ref, wo_ref, o_ref, o3_ref):
    x = x_ref[...]
    h = _rmsnorm(x, g_ref[...]).astype(BF16)
    q = jnp.dot(h, wq_ref[...], preferred_element_type=F32).astype(BF16)
    cw = wq_ref.shape[1]
    hd = cw // N_CROSS_HEADS
    outs = []
    for hh in range(N_CROSS_HEADS):
        qh = q[:, hh * hd:(hh + 1) * hd]
        kh = kv_ref[:, hh * hd:(hh + 1) * hd]
        vh = kv_ref[:, cw + hh * hd:cw + (hh + 1) * hd]
        s = _dot_nt(qh, kh) * (hd ** -0.5)
        e = jnp.exp(s - jnp.max(s, axis=-1, keepdims=True))
        p = (e / jnp.sum(e, axis=-1, keepdims=True)).astype(BF16)
        outs.append(jnp.dot(p, vh, preferred_element_type=F32))
    o = jnp.concatenate(outs, axis=1).astype(BF16)
    o_ref[...] = x + jnp.dot(o, wo_ref[...], preferred_element_type=F32)
    o3_ref[...] = o_ref[...].reshape(o3_ref.shape)


def _cross_block(x2d, g, wq, kv, wo, seq, n_mem, bm):
    m, d = x2d.shape
    bm = min(bm, seq)
    per_b = seq // bm
    cw = wq.shape[1]
    return pl.pallas_call(
        _cross_body,
        grid=(m // bm,),
        in_specs=[
            pl.BlockSpec((bm, d), lambda i: (i, 0)),
            pl.BlockSpec((1, d), lambda i: (0, 0)),
            pl.BlockSpec((d, cw), lambda i: (0, 0)),
            pl.BlockSpec((n_mem, 2 * cw), lambda i: (i // per_b, 0)),
            pl.BlockSpec((cw, d), lambda i: (0, 0)),
        ],
        out_specs=[pl.BlockSpec((bm, d), lambda i: (i, 0)), pl.BlockSpec((bm, 1, d), lambda i: (i, 0, 0))],
        out_shape=[jax.ShapeDtypeStruct((m, d), F32), jax.ShapeDtypeStruct((m, 1, d), F32)],
        compiler_params=_cparams("parallel"),
        name="cross_block",
    )(x2d, g.reshape(1, d), wq, kv, wo)


def _router_body(x_ref, g_ref, w_ref, b_ref, idx_ref, gate_ref, rank_ref, cnt_ref, carry_ref):
    @pl.when(pl.program_id(0) == 0)
    def _():
        carry_ref[...] = jnp.zeros_like(carry_ref)

    bm = x_ref.shape[0]
    h = _rmsnorm(x_ref[...], g_ref[...]).astype(BF16)
    logits = jnp.dot(h, w_ref[...], preferred_element_type=F32) + b_ref[...]
    lane = lax.broadcasted_iota(jnp.int32, (bm, LANES), 1)
    lane_f = lane.astype(F32)
    work = logits
    self32 = jnp.zeros((bm, LANES), F32)
    vals, picks, hots = [], [], []
    for _ in range(TOP_K):
        mk = jnp.max(work, axis=-1, keepdims=True)
        pick = jnp.min(jnp.where(work == mk, lane_f, float(LANES)), axis=-1, keepdims=True)
        hot = lane_f == pick
        vals.append(mk)
        picks.append(pick.astype(jnp.int32))
        hots.append(hot)
        self32 = jnp.where(hot, 1.0, self32)
        work = jnp.where(hot, -jnp.inf, work)
    es = [jnp.exp(v - vals[0]) for v in vals]
    denom = es[0] + es[1] + es[2] + es[3]

    r = lax.broadcasted_iota(jnp.int32, (bm, bm), 0)
    c = lax.broadcasted_iota(jnp.int32, (bm, bm), 1)
    tri = (c < r).astype(BF16)
    rank = jnp.dot(tri, self32.astype(BF16), preferred_element_type=F32) + carry_ref[0:1, :]
    carry_new = carry_ref[0:1, :] + jnp.sum(self32, axis=0, keepdims=True)
    carry_ref[...] = jnp.broadcast_to(carry_new, carry_ref.shape)

    idx_o = jnp.zeros((bm, LANES), jnp.int32)
    gate_o = jnp.zeros((bm, LANES), F32)
    rank_o = jnp.zeros((bm, LANES), jnp.int32)
    for kk in range(TOP_K):
        rk = jnp.sum(jnp.where(hots[kk], rank, 0.0), axis=-1, keepdims=True).astype(jnp.int32)
        idx_o = jnp.where(lane == kk, picks[kk], idx_o)
        gate_o = jnp.where(lane == kk, es[kk] / denom, gate_o)
        rank_o = jnp.where(lane == kk, rk, rank_o)
    idx_ref[...] = idx_o
    gate_ref[...] = gate_o
    rank_ref[...] = rank_o
    cnt_ref[...] = jnp.broadcast_to(carry_new, cnt_ref.shape).astype(jnp.int32)


def _router(x2d, g, w_pad, b_pad, bm):
    m, d = x2d.shape
    bm = min(bm, m)
    tok = pl.BlockSpec((bm, LANES), lambda i: (i, 0))
    return pl.pallas_call(
        _router_body,
        grid=(m // bm,),
        in_specs=[
            pl.BlockSpec((bm, d), lambda i: (i, 0)),
            pl.BlockSpec((1, d), lambda i: (0, 0)),
            pl.BlockSpec((d, LANES), lambda i: (0, 0)),
            pl.BlockSpec((1, LANES), lambda i: (0, 0)),
        ],
        out_specs=[tok, tok, tok, pl.BlockSpec((8, LANES), lambda i: (0, 0))],
        out_shape=[
            jax.ShapeDtypeStruct((m, LANES), jnp.int32),
            jax.ShapeDtypeStruct((m, LANES), F32),
            jax.ShapeDtypeStruct((m, LANES), jnp.int32),
            jax.ShapeDtypeStruct((8, LANES), jnp.int32),
        ],
        scratch_shapes=[pltpu.VMEM((8, LANES), F32)],
        compiler_params=_cparams("arbitrary"),
        name="router",
    )(x2d, g.reshape(1, d), w_pad, b_pad)


def _dispatch_body(dest_ref, pend_ref, padded_ref, x_ref, xs_ref, z_ref, sem, zsem):
    i = pl.program_id(0)
    bm = x_ref.shape[0]
    n_exp = pend_ref.shape[0]
    tm = z_ref.shape[0]

    def zero_copy(e):
        return pltpu.make_async_copy(z_ref, xs_ref.at[pl.ds(pend_ref[e] - tm, tm)], zsem)

    @pl.when(i == 0)
    def _():
        z_ref[...] = jnp.zeros_like(z_ref)
        for e in range(n_exp):
            @pl.when(padded_ref[e] > 0)
            def _():
                zero_copy(e).start()
        for e in range(n_exp):
            @pl.when(padded_ref[e] > 0)
            def _():
                zero_copy(e).wait()

    def row_copy(r, kk):
        d = dest_ref[(i * bm + r) * TOP_K + kk]
        return pltpu.make_async_copy(x_ref.at[r], xs_ref.at[d], sem)

    def issue(r, carry):
        for kk in range(TOP_K):
            row_copy(r, kk).start(priority=kk % 2)
        return carry

    def drain(r, carry):
        for kk in range(TOP_K):
            row_copy(r, kk).wait()
        return carry

    lax.fori_loop(0, bm, issue, 0)
    lax.fori_loop(0, bm, drain, 0)


def _dispatch(dest_flat, pend, padded, x3d, n_rows, bm):
    m, _, d = x3d.shape
    bm = min(bm, m)
    grid_spec = pltpu.PrefetchScalarGridSpec(
        num_scalar_prefetch=3,
        grid=(m // bm,),
        in_specs=[pl.BlockSpec((bm, 1, d), lambda i, *_: (i, 0, 0))],
        out_specs=pl.BlockSpec(memory_space=pl.ANY),
        scratch_shapes=[
            pltpu.VMEM((MOE_ROW_TILE, 1, d), F32),
            pltpu.SemaphoreType.DMA,
            pltpu.SemaphoreType.DMA,
        ],
    )
    return pl.pallas_call(
        _dispatch_body,
        grid_spec=grid_spec,
        out_shape=jax.ShapeDtypeStruct((n_rows, 1, d), F32),
        compiler_params=_cparams("arbitrary"),
        name="moe_dispatch",
    )(dest_flat, pend, padded, x3d)


def _moe_body(ie_ref, istart_ref, insub_ref, xs_ref, g_ref, w1_ref, b1_ref, w2_ref, b2_ref,
              ys_ref, xbf_ref, yacc_ref, st0_ref, st1_ref, fl0_ref, fl1_ref, w1b_ref, w2m_ref, w2b_ref, sems):
    it = pl.program_id(0)
    c = pl.program_id(1)
    n_chunks = pl.num_programs(1)
    nsub = insub_ref[it]
    start = istart_ref[it]
    tm = fl0_ref.shape[0]
    tc = w1_ref.shape[2]
    half = tc // 2
    quarter = tc // 4
    even_lane = lax.broadcasted_iota(jnp.int32, (1, half), 1) % 2 == 0
    stage = (st0_ref, st1_ref)
    flat = (fl0_ref, fl1_ref)
    n_pairs = (nsub + 1) // 2
    n_full = nsub // 2
    odd = nsub % 2 == 1

    def rows(s):
        return pl.ds(pl.multiple_of(s * tm, tm), tm)

    def load_copy(s, k):
        return pltpu.make_async_copy(xs_ref.at[pl.ds(start + s * tm, tm)], stage[k], sems.at[k])

    def store_copy(s, k):
        return pltpu.make_async_copy(stage[k], ys_ref.at[pl.ds(start + s * tm, tm)], sems.at[k])

    def consume(s, k):
        flat[k][...] = stage[k][...].reshape(flat[k].shape)
        xbf_ref[rows(s), :] = _rmsnorm(flat[k][...], g_ref[...]).astype(BF16)
        yacc_ref[rows(s), :] = jnp.zeros((tm, yacc_ref.shape[1]), F32)

    def hidden(s):
        return jnp.dot(xbf_ref[rows(s), :], w1b_ref[...], preferred_element_type=F32) + b1_ref[0]

    def accumulated(s, u):
        glu = jnp.minimum(u, SWIGLU_LIMIT)
        act = glu * (1.0 / (1.0 + jnp.exp(-SWIGLU_ALPHA * glu)))
        lin1 = jnp.clip(u, -SWIGLU_LIMIT, SWIGLU_LIMIT) + 1.0
        a_even = act * pltpu.roll(lin1, tc - 1, 1)
        merged = jnp.where(even_lane, a_even[:, :half], pltpu.roll(a_even[:, half:], 1, 1))
        return yacc_ref[rows(s), :] + jnp.dot(merged.astype(BF16), w2b_ref[...], preferred_element_type=F32)

    def emit(s, k, wait_for):
        if wait_for is not None:
            store_copy(wait_for, k).wait()
        stage[k][...] = flat[k][...].reshape(stage[k].shape)
        store_copy(s, k).start()

    @pl.when(nsub > 0)
    def _():
        @pl.when(c == 0)
        def _():
            load_copy(0, 0).start()

            def load_pair(sp, carry):
                s0 = 2 * sp
                s1 = s0 + 1

                @pl.when(s1 < nsub)
                def _():
                    load_copy(s1, 1).start()
                load_copy(s0, 0).wait()
                consume(s0, 0)

                @pl.when(s1 < nsub)
                def _():
                    @pl.when(s1 + 1 < nsub)
                    def _():
                        load_copy(s1 + 1, 0).start()
                    load_copy(s1, 1).wait()
                    consume(s1, 1)
                return carry
            lax.fori_loop(0, n_pairs, load_pair, 0)

        w1b_ref[...] = w1_ref[0].astype(BF16)
        for j in range(w2m_ref.shape[0]):
            cols = slice(j * LANES, (j + 1) * LANES)
            w2m_ref[j, pl.ds(0, quarter, stride=2), :] = w2_ref[0, 0:quarter, cols]
            w2m_ref[j, pl.ds(1, quarter, stride=2), :] = w2_ref[0, quarter:2 * quarter, cols]
            w2b_ref[:, cols] = w2m_ref[j].astype(BF16)

        @pl.when(c < n_chunks - 1)
        def _():
            def pair(sp, carry):
                us = [hidden(2 * sp + k) for k in range(2)]
                for k in range(2):
                    yacc_ref[rows(2 * sp + k), :] = accumulated(2 * sp + k, us[k])
                return carry
            lax.fori_loop(0, n_full, pair, 0)

            @pl.when(odd)
            def _():
                yacc_ref[rows(nsub - 1), :] = accumulated(nsub - 1, hidden(nsub - 1))

        @pl.when(c == n_chunks - 1)
        def _():
            @pl.when(n_full > 0)
            def _():
                us = [hidden(k) for k in range(2)]
                for k in range(2):
                    flat[k][...] = accumulated(k, us[k]) + b2_ref[0]
                for k in range(2):
                    emit(k, k, None)

                def pair(sp, carry):
                    us = [hidden(2 * sp + k) for k in range(2)]
                    for k in range(2):
                        flat[k][...] = accumulated(2 * sp + k, us[k]) + b2_ref[0]
                    for k in range(2):
                        emit(2 * sp + k, k, 2 * sp + k - 2)
                    return carry
                lax.fori_loop(1, n_full, pair, 0)

            @pl.when(odd)
            def _():
                flat[0][...] = accumulated(nsub - 1, hidden(nsub - 1)) + b2_ref[0]

                @pl.when(n_full > 0)
                def _():
                    store_copy(nsub - 3, 0).wait()
                emit(nsub - 1, 0, None)

            store_copy(jnp.where(odd, nsub - 1, nsub - 2), 0).wait()

            @pl.when(nsub > 1)
            def _():
                store_copy(jnp.where(odd, nsub - 2, nsub - 1), 1).wait()


def _moe_mlp(item_e, item_start, item_nsub, xs, g, w1, b1, w2, b2):
    n_rows, _, d = xs.shape
    n_exp, _, two_i = w1.shape
    tc = min(MOE_COL_CHUNK, two_i)
    n_chunks = two_i // tc
    n_items = item_e.shape[0]
    last = n_chunks - 1

    def chunk(c, it, insub):
        return jnp.where(insub[it] > 0, c, last)

    grid_spec = pltpu.PrefetchScalarGridSpec(
        num_scalar_prefetch=3,
        grid=(n_items, n_chunks),
        in_specs=[
            pl.BlockSpec(memory_space=pl.ANY),
            pl.BlockSpec((1, d), lambda it, c, ie, ist, ins: (0, 0)),
            pl.BlockSpec((1, d, tc), lambda it, c, ie, ist, ins: (ie[it], 0, chunk(c, it, ins))),
            pl.BlockSpec((1, 1, tc), lambda it, c, ie, ist, ins: (ie[it], 0, chunk(c, it, ins))),
            pl.BlockSpec((1, tc // 2, d), lambda it, c, ie, ist, ins: (ie[it], chunk(c, it, ins), 0)),
            pl.BlockSpec((1, 1, d), lambda it, c, ie, ist, ins: (ie[it], 0, 0)),
        ],
        out_specs=pl.BlockSpec(memory_space=pl.ANY),
        scratch_shapes=[
            pltpu.VMEM((MOE_ROWS_PER_ITEM, d), BF16),
            pltpu.VMEM((MOE_ROWS_PER_ITEM, d), F32),
            pltpu.VMEM((MOE_ROW_TILE, 1, d), F32),
            pltpu.VMEM((MOE_ROW_TILE, 1, d), F32),
            pltpu.VMEM((MOE_ROW_TILE, d), F32),
            pltpu.VMEM((MOE_ROW_TILE, d), F32),
            pltpu.VMEM((d, tc), BF16),
            pltpu.VMEM((d // LANES, tc // 2, LANES), F32),
            pltpu.VMEM((tc // 2, d), BF16),
            pltpu.SemaphoreType.DMA((2,)),
        ],
    )
    return pl.pallas_call(
        _moe_body,
        grid_spec=grid_spec,
        out_shape=jax.ShapeDtypeStruct((n_rows, 1, d), F32),
        compiler_params=_cparams("arbitrary", "arbitrary"),
        name="moe_mlp",
    )(item_e, item_start, item_nsub, xs, g.reshape(1, d), w1, b1, w2, b2)


def _combine_body(dest_ref, x_ref, gate_ref, g_ref, ys_ref, o_ref, b0, b1, b2, b3, flat_ref, sem, *, final_norm):
    i = pl.program_id(0)
    bm = x_ref.shape[0]
    bufs = (b0, b1, b2, b3)

    def row_copy(r, kk):
        d = dest_ref[(i * bm + r) * TOP_K + kk]
        return pltpu.make_async_copy(ys_ref.at[d], bufs[kk].at[r], sem)

    def issue(r, carry):
        for kk in range(TOP_K):
            row_copy(r, kk).start(priority=kk % 2)
        return carry

    def drain(r, carry):
        for kk in range(TOP_K):
            row_copy(r, kk).wait()
        return carry

    lax.fori_loop(0, bm, issue, 0)
    lax.fori_loop(0, bm, drain, 0)
    acc = x_ref[...]
    for kk in range(TOP_K):
        flat_ref[...] = bufs[kk][...].reshape(flat_ref.shape)
        acc = acc + gate_ref[:, kk:kk + 1] * flat_ref[...]
    o_ref[...] = _rmsnorm(acc, g_ref[...]) if final_norm else acc


def _combine(dest_flat, x2d, gate4, g, ys, bm, final_norm):
    m, d = x2d.shape
    bm = min(bm, m)
    grid_spec = pltpu.PrefetchScalarGridSpec(
        num_scalar_prefetch=1,
        grid=(m // bm,),
        in_specs=[
            pl.BlockSpec((bm, d), lambda i, *_: (i, 0)),
            pl.BlockSpec((bm, LANES), lambda i, *_: (i, 0)),
            pl.BlockSpec((1, d), lambda i, *_: (0, 0)),
            pl.BlockSpec(memory_space=pl.ANY),
        ],
        out_specs=pl.BlockSpec((bm, d), lambda i, *_: (i, 0)),
        scratch_shapes=[pltpu.VMEM((bm, 1, d), F32)] * TOP_K + [pltpu.VMEM((bm, d), F32), pltpu.SemaphoreType.DMA],
    )
    return pl.pallas_call(
        functools.partial(_combine_body, final_norm=final_norm),
        grid_spec=grid_spec,
        out_shape=jax.ShapeDtypeStruct((m, d), F32),
        compiler_params=_cparams("arbitrary"),
        name="moe_combine",
    )(dest_flat, x2d, gate4, g.reshape(1, d), ys)


def _routing_tables(idx4, rank4, counts, n_assign, n_exp):
    tm, rpi = MOE_ROW_TILE, MOE_ROWS_PER_ITEM
    padded = (counts + tm - 1) // tm * tm
    pend = jnp.cumsum(padded)
    pstart = pend - padded
    dest = (pstart[idx4] + rank4).astype(jnp.int32).reshape(-1)

    n_rows = n_assign + n_exp * tm
    n_items = n_rows // rpi + n_exp
    per_e = (padded + rpi - 1) // rpi
    icum = jnp.cumsum(per_e)
    ioff = icum - per_e
    total = icum[-1]
    ii = jnp.arange(n_items, dtype=jnp.int32)
    ic = jnp.minimum(ii, total - 1)
    e_of = jnp.sum((icum[None, :] <= ic[:, None]).astype(jnp.int32), axis=1)
    e_of = jnp.minimum(e_of, n_exp - 1)
    local = ii - ioff[e_of]
    valid = ii < total
    start = jnp.where(valid, pstart[e_of] + local * rpi, 0).astype(jnp.int32)
    nsub = jnp.where(valid, jnp.clip((padded[e_of] - local * rpi) // tm, 0, rpi // tm), 0).astype(jnp.int32)
    return dest, pend.astype(jnp.int32), padded.astype(jnp.int32), e_of, start, nsub, n_rows


def _alibi_slopes(n):
    start = 2.0 ** (-8.0 / n)
    return np.array([start ** (i + 1) for i in range(n)], dtype=np.float32)


def kernel(x, mem, g_mix, w_in, b_forget, swa_sinks, w_out, g_cross, g_mem, w_cq, w_ckv, w_co, g_moe,
           w_router, b_router, w_mlp1, b_mlp1, w_mlp2, b_mlp2, g_final):
    batch, seq, d = x.shape
    n_mem = mem.shape[1]
    depth = w_in.shape[0]
    n_fox = d // (2 * HEAD_DIM)
    n_swa = d // (2 * HEAD_DIM)
    n_kv = n_swa // SWA_GROUP
    fox_w = n_fox * HEAD_DIM
    swa_q_w = n_swa * HEAD_DIM
    swa_kv_w = n_kv * HEAD_DIM
    o2 = 2 * fox_w
    o3 = 3 * fox_w
    o4 = o3 + n_fox
    o5 = o4 + swa_q_w
    o6 = o5 + swa_kv_w
    n_exp = w_router.shape[2]
    t = batch * seq
    slopes = jnp.asarray(_alibi_slopes(n_swa))

    def dup_heads(wk):
        wk = wk.reshape(d, n_kv, 1, HEAD_DIM)
        return jnp.broadcast_to(wk, (d, n_kv, 2, HEAD_DIM)).reshape(d, n_kv * LANES)

    x2d = x.reshape(t, d)
    for l in range(depth):
        wl = w_in[l]
        w_qk = wl[:, :o2].astype(BF16)
        w_vt = wl[:, o2:o3].T.astype(BF16)
        w_ff = jnp.zeros((d, LANES), BF16).at[:, :n_fox].set(wl[:, o3:o4].astype(BF16))
        b_ff = jnp.zeros((1, LANES), F32).at[0, :n_fox].set(b_forget[l])
        w_swa = jnp.concatenate(
            [wl[:, o4:o5], dup_heads(wl[:, o5:o6]), dup_heads(wl[:, o6:])], axis=1).astype(BF16)

        qk = _norm_matmul(x2d, g_mix[l], w_qk, BF16, 1024, 1024, "fox_qk_proj")
        vt = _norm_matmul(x2d, g_mix[l], w_vt, BF16, 1024, 1024, "fox_vt_proj", transposed=True)
        swa_proj = _norm_matmul(x2d, g_mix[l], w_swa, BF16, 1024, 1024, "swa_proj")
        c = _fox_gate(x2d.reshape(batch, seq, d), g_mix[l], w_ff, b_ff, 512)
        fox = _fox_attention(qk, vt, c, batch, seq, fox_w // LANES, FOX_TILE)
        swa = _swa_attention(swa_proj, slopes, swa_sinks[l], batch, seq, swa_q_w // LANES, n_kv)
        x2d = _out_proj(fox, swa, w_out[l].astype(BF16), x2d, 1024, 1024)

        kv = _norm_matmul(mem.reshape(batch * n_mem, d), g_mem[l], w_ckv[l].astype(BF16), BF16, 1024, 1024,
                          "mem_kv_proj")
        x2d, x3d = _cross_block(x2d, g_cross[l], w_cq[l].astype(BF16), kv, w_co[l].astype(BF16), seq, n_mem, 512)

        w_r = jnp.zeros((d, LANES), BF16).at[:, :n_exp].set(w_router[l].astype(BF16))
        b_r = jnp.full((1, LANES), NEG_BIG, F32).at[0, :n_exp].set(b_router[l])
        idx4, gate4, rank4, cnt = _router(x2d, g_moe[l], w_r, b_r, 512)
        dest, pend, padded, item_e, item_start, item_nsub, n_rows = _routing_tables(
            idx4[:, :TOP_K], rank4[:, :TOP_K], cnt[0, :n_exp], t * TOP_K, n_exp)
        xs = _dispatch(dest, pend, padded, x3d, n_rows, 256)
        two_i = w_mlp1.shape[3]
        ys = _moe_mlp(item_e, item_start, item_nsub, xs, g_moe[l], w_mlp1[l], b_mlp1[l].reshape(n_exp, 1, two_i),
                      w_mlp2[l], b_mlp2[l].reshape(n_exp, 1, d))
        x2d = _combine(dest, x2d, gate4, g_final, ys, 256, final_norm=(l == depth - 1))
    return x2d.reshape(batch, seq, d)
```

```python
import functools

import numpy as np
import jax
import jax.numpy as jnp
from jax import lax
from jax.experimental import pallas as pl
from jax.experimental.pallas import tpu as pltpu

F32 = jnp.float32
BF16 = jnp.bfloat16

RMS_EPS = 1e-5
HEAD_DIM = 64
WINDOW = 128
SWA_GROUP = 4
N_CROSS_HEADS = 4
TOP_K = 4
SWIGLU_ALPHA = 1.702
SWIGLU_LIMIT = 7.0

LANES = 128
MXU_DIM = 256
VMEM_LIMIT_BYTES = 56 * 2**20
NEG_BIG = -1e30

FOX_TILE = 512
SWA_WINDOWS_PER_STEP = 8
ROW_COPY_UNROLL = 8
MOE_ROW_TILE = 256
MOE_ROWS_PER_ITEM = 2304
MOE_COL_CHUNK = 512


def _cparams(*sem):
    return pltpu.CompilerParams(dimension_semantics=sem, vmem_limit_bytes=VMEM_LIMIT_BYTES)


def _rmsnorm(x, g):
    ms = jnp.mean(x * x, axis=-1, keepdims=True)
    return x * lax.rsqrt(ms + RMS_EPS) * g


def _for_rows(n, fn):
    def trip(o, carry):
        for u in range(ROW_COPY_UNROLL):
            fn(o * ROW_COPY_UNROLL + u)
        return carry
    lax.fori_loop(0, n // ROW_COPY_UNROLL, trip, 0)


def _dot_nt(a, b):
    return lax.dot_general(a, b, (((1,), (1,)), ((), ())), preferred_element_type=F32)


def _norm_matmul_body(x_ref, g_ref, w_ref, o_ref, h_ref, *, transposed):
    @pl.when(pl.program_id(1) == 0)
    def _():
        h_ref[...] = _rmsnorm(x_ref[...], g_ref[...]).astype(h_ref.dtype)

    if transposed:
        o_ref[...] = _dot_nt(w_ref[...], h_ref[...]).astype(o_ref.dtype)
    else:
        o_ref[...] = jnp.dot(h_ref[...], w_ref[...], preferred_element_type=F32).astype(o_ref.dtype)


def _norm_matmul(x2d, g, w, out_dtype, bm, bn, name, transposed=False):
    m, d = x2d.shape
    n = w.shape[0] if transposed else w.shape[1]
    bm, bn = min(bm, m), min(bn, n)
    if transposed:
        w_spec = pl.BlockSpec((bn, d), lambda i, j: (j, 0))
        o_spec = pl.BlockSpec((bn, bm), lambda i, j: (j, i))
        o_shape = (n, m)
    else:
        w_spec = pl.BlockSpec((d, bn), lambda i, j: (0, j))
        o_spec = pl.BlockSpec((bm, bn), lambda i, j: (i, j))
        o_shape = (m, n)
    return pl.pallas_call(
        functools.partial(_norm_matmul_body, transposed=transposed),
        grid=(m // bm, n // bn),
        in_specs=[pl.BlockSpec((bm, d), lambda i, j: (i, 0)), pl.BlockSpec((1, d), lambda i, j: (0, 0)), w_spec],
        out_specs=o_spec,
        out_shape=jax.ShapeDtypeStruct(o_shape, out_dtype),
        scratch_shapes=[pltpu.VMEM((bm, d), BF16)],
        compiler_params=_cparams("parallel", "arbitrary"),
        name=name,
    )(x2d, g.reshape(1, d), w)


def _fox_gate_body(x_ref, g_ref, w_ref, b_ref, c_ref, carry_ref):
    @pl.when(pl.program_id(1) == 0)
    def _():
        carry_ref[...] = jnp.zeros_like(carry_ref)

    bs = x_ref.shape[1]
    h = _rmsnorm(x_ref[0], g_ref[...]).astype(BF16)
    z = jnp.dot(h, w_ref[...], preferred_element_type=F32) + b_ref[...]
    logf = jnp.minimum(z, 0.0) - jnp.log1p(jnp.exp(-jnp.abs(z)))
    r = lax.broadcasted_iota(jnp.int32, (bs, bs), 0)
    c = lax.broadcasted_iota(jnp.int32, (bs, bs), 1)
    tri = (c <= r).astype(F32)
    cs = jnp.dot(tri, logf, preferred_element_type=F32, precision=lax.Precision.HIGHEST)
    cs = cs + carry_ref[0:1, :]
    c_ref[0] = cs
    carry_ref[...] = jnp.broadcast_to(cs[bs - 1:bs, :], carry_ref.shape)


def _fox_gate(x, g, w_pad, b_pad, bs):
    b, s, d = x.shape
    bs = min(bs, s)
    return pl.pallas_call(
        _fox_gate_body,
        grid=(b, s // bs),
        in_specs=[
            pl.BlockSpec((1, bs, d), lambda i, j: (i, j, 0)),
            pl.BlockSpec((1, d), lambda i, j: (0, 0)),
            pl.BlockSpec((d, LANES), lambda i, j: (0, 0)),
            pl.BlockSpec((1, LANES), lambda i, j: (0, 0)),
        ],
        out_specs=pl.BlockSpec((1, bs, LANES), lambda i, j: (i, j, 0)),
        out_shape=jax.ShapeDtypeStruct((b, s, LANES), F32),
        scratch_shapes=[pltpu.VMEM((8, LANES), F32)],
        compiler_params=_cparams("parallel", "arbitrary"),
        name="fox_gate",
    )(x, g.reshape(1, d), w_pad, b_pad)


def _fox_attn_body(itab_ref, jtab_ref, q_ref, k_ref, vt_ref, c_ref, o_ref, m_ref, l_ref, acc_ref):
    p = pl.program_id(1)
    st = pl.program_id(2)
    i = itab_ref[st]
    j = jtab_ref[st]
    tq = q_ref.shape[0]
    tk = k_ref.shape[0]
    lane = lax.broadcasted_iota(jnp.int32, (1, LANES), 1)

    @pl.when(j == 0)
    def _():
        m_ref[...] = jnp.full_like(m_ref, NEG_BIG)
        l_ref[...] = jnp.zeros_like(l_ref)
        acc_ref[...] = jnp.zeros_like(acc_ref)

    def step(masked):
        q = q_ref[...]
        k = k_ref[...]
        vt = vt_ref[...]
        cblk = c_ref[0]
        if masked:
            key = lax.broadcasted_iota(jnp.int32, (tk, tq), 0)
            qry = lax.broadcasted_iota(jnp.int32, (tk, tq), 1)
            causal = key <= qry
        scores = []
        for hh in range(2):
            qh = (jnp.where(lane // HEAD_DIM == hh, q, jnp.zeros_like(q)) * (HEAD_DIM ** -0.5)).astype(BF16)
            scores.append(_dot_nt(k, qh))
        n_split = 2 if tq % (2 * MXU_DIM) == 0 else 1
        wq = tq // n_split
        for hh in range(2):
            c_col = jnp.sum(jnp.where(lane == 2 * p + hh, cblk, 0.0), axis=1, keepdims=True)
            for g in range(n_split):
                cols = slice(g * wq, (g + 1) * wq)
                s = scores[hh][:, cols] - c_col
                if masked:
                    s = jnp.where(causal[:, cols], s, NEG_BIG)
                m_prev = m_ref[hh, :, cols]
                m_new = jnp.maximum(m_prev, jnp.max(s, axis=0, keepdims=True))
                alpha = jnp.exp(m_prev - m_new)
                pt = jnp.exp(s - m_new)
                l_ref[hh, :, cols] = alpha * l_ref[hh, :, cols] + jnp.sum(pt, axis=0, keepdims=True)
                acc_ref[hh, :, cols] = alpha * acc_ref[hh, :, cols] + jnp.dot(
                    vt, pt.astype(BF16), preferred_element_type=F32)
                m_ref[hh, :, cols] = m_new

    @pl.when(j < i)
    def _():
        step(False)

    @pl.when(j == i)
    def _():
        step(True)
        row = lax.broadcasted_iota(jnp.int32, (LANES, 1), 0)
        ot = jnp.where(row < HEAD_DIM, acc_ref[0] / l_ref[0], acc_ref[1] / l_ref[1])
        o_ref[...] = ot.T.astype(o_ref.dtype)


def _fox_attention(qk, vt, c, batch, seq, n_pairs, tile):
    tile = min(tile, seq)
    nq = seq // tile
    pairs = [(i, j) for i in range(nq) for j in range(i + 1)]
    itab = jnp.asarray([ij[0] for ij in pairs], jnp.int32)
    jtab = jnp.asarray([ij[1] for ij in pairs], jnp.int32)
    grid_spec = pltpu.PrefetchScalarGridSpec(
        num_scalar_prefetch=2,
        grid=(batch, n_pairs, len(pairs)),
        in_specs=[
            pl.BlockSpec((tile, LANES), lambda b, p, s, it, jt: (b * nq + it[s], p)),
            pl.BlockSpec((tile, LANES), lambda b, p, s, it, jt: (b * nq + jt[s], n_pairs + p)),
            pl.BlockSpec((LANES, tile), lambda b, p, s, it, jt: (p, b * nq + jt[s])),
            pl.BlockSpec((1, tile, LANES), lambda b, p, s, it, jt: (b, jt[s], 0)),
        ],
        out_specs=pl.BlockSpec((tile, LANES), lambda b, p, s, it, jt: (b * nq + it[s], p)),
        scratch_shapes=[
            pltpu.VMEM((2, 1, tile), F32),
            pltpu.VMEM((2, 1, tile), F32),
            pltpu.VMEM((2, LANES, tile), F32),
        ],
    )
    return pl.pallas_call(
        _fox_attn_body,
        grid_spec=grid_spec,
        out_shape=jax.ShapeDtypeStruct((batch * seq, n_pairs * LANES), BF16),
        compiler_params=_cparams("parallel", "parallel", "arbitrary"),
        name="fox_attention",
    )(itab, jtab, qk, qk, vt, c)


def _swa_body(slopes_ref, sinks_ref, q_ref, kp_ref, kc_ref, vp_ref, vc_ref, o_ref):
    pr = pl.program_id(1)
    n = pl.program_id(2)
    w = WINDOW
    n_win = q_ref.shape[0] // w
    lane = lax.broadcasted_iota(jnp.int32, (1, LANES), 1)
    row = lax.broadcasted_iota(jnp.int32, (2 * w, 1), 0)
    head1 = row >= w
    slope = jnp.where(head1, slopes_ref[2 * pr + 1], slopes_ref[2 * pr])
    sink = jnp.where(head1, sinks_ref[2 * pr + 1], sinks_ref[2 * pr])
    qi = lax.broadcasted_iota(jnp.int32, (2 * w, 2 * w), 0) % w
    kj = lax.broadcasted_iota(jnp.int32, (2 * w, 2 * w), 1)
    rel = qi - kj + w
    in_window = (rel >= 0) & (rel < w)
    bias = slope * rel.astype(F32)
    for wi in range(n_win):
        qw = q_ref[wi * w:(wi + 1) * w, :]
        if wi == 0:
            k = jnp.concatenate([kp_ref[...], kc_ref[0:w, :]], axis=0)
            v = jnp.concatenate([vp_ref[...], vc_ref[0:w, :]], axis=0)
            valid = in_window & (kj >= jnp.where(n > 0, 0, w))
        else:
            k = kc_ref[(wi - 1) * w:(wi + 1) * w, :]
            v = vc_ref[(wi - 1) * w:(wi + 1) * w, :]
            valid = in_window
        q2 = jnp.concatenate([jnp.where(lane < HEAD_DIM, qw, jnp.zeros_like(qw)),
                              jnp.where(lane >= HEAD_DIM, qw, jnp.zeros_like(qw))], axis=0)
        s = _dot_nt(q2, k) * (HEAD_DIM ** -0.5) - bias
        s = jnp.where(valid, s, NEG_BIG)
        m = jnp.maximum(jnp.max(s, axis=-1, keepdims=True), sink)
        p = jnp.exp(s - m)
        denom = jnp.sum(p, axis=-1, keepdims=True) + jnp.exp(sink - m)
        o = jnp.dot((p / denom).astype(BF16), v, preferred_element_type=F32)
        o_ref[wi * w:(wi + 1) * w, :] = jnp.where(lane < HEAD_DIM, o[:w], o[w:]).astype(o_ref.dtype)


def _swa_attention(proj, slopes, sinks, batch, seq, n_pairs, n_kv):
    n_win = min(SWA_WINDOWS_PER_STEP, seq // WINDOW)
    blk = n_win * WINDOW
    nb = seq // blk
    wins = seq // WINDOW

    def q_map(b, p, n):
        return (b * nb + n, p)

    def cur_map(base):
        return lambda b, p, n: (b * nb + n, base + p // 2)

    def prev_map(base):
        return lambda b, p, n: (b * wins + jnp.maximum(n * n_win - 1, 0), base + p // 2)

    smem = pl.BlockSpec(memory_space=pltpu.SMEM)
    return pl.pallas_call(
        _swa_body,
        grid=(batch, n_pairs, nb),
        in_specs=[
            smem, smem,
            pl.BlockSpec((blk, LANES), q_map),
            pl.BlockSpec((WINDOW, LANES), prev_map(n_pairs)),
            pl.BlockSpec((blk, LANES), cur_map(n_pairs)),
            pl.BlockSpec((WINDOW, LANES), prev_map(n_pairs + n_kv)),
            pl.BlockSpec((blk, LANES), cur_map(n_pairs + n_kv)),
        ],
        out_specs=pl.BlockSpec((blk, LANES), q_map),
        out_shape=jax.ShapeDtypeStruct((batch * seq, n_pairs * LANES), BF16),
        compiler_params=_cparams("parallel", "parallel", "parallel"),
        name="swa_attention",
    )(slopes, sinks, proj, proj, proj, proj, proj)


def _out_proj_body(a1_ref, a2_ref, w_ref, x_ref, o_ref):
    a = jnp.concatenate([a1_ref[...], a2_ref[...]], axis=1)
    o_ref[...] = x_ref[...] + jnp.dot(a, w_ref[...], preferred_element_type=F32)


def _out_proj(a1, a2, w, x2d, bm, bn):
    m, d = x2d.shape
    k1, k2 = a1.shape[1], a2.shape[1]
    bm, bn = min(bm, m), min(bn, d)
    return pl.pallas_call(
        _out_proj_body,
        grid=(m // bm, d // bn),
        in_specs=[
            pl.BlockSpec((bm, k1), lambda i, j: (i, 0)),
            pl.BlockSpec((bm, k2), lambda i, j: (i, 0)),
            pl.BlockSpec((k1 + k2, bn), lambda i, j: (0, j)),
            pl.BlockSpec((bm, bn), lambda i, j: (i, j)),
        ],
        out_specs=pl.BlockSpec((bm, bn), lambda i, j: (i, j)),
        out_shape=jax.ShapeDtypeStruct((m, d), F32),
        compiler_params=_cparams("parallel", "parallel"),
        name="out_proj",
    )(a1, a2, w, x2d)


def _cross_body(x_ref, g_ref, wq_ref, kv_ref, wo_ref, o_ref):
    x = x_ref[...]
    h = _rmsnorm(x, g_ref[...]).astype(BF16)
    q = jnp.dot(h, wq_ref[...], preferred_element_type=F32).astype(BF16)
    cw = wq_ref.shape[1]
    hd = cw // N_CROSS_HEADS
    outs = []
    for hh in range(N_CROSS_HEADS):
        qh = q[:, hh * hd:(hh + 1) * hd]
        kh = kv_ref[:, hh * hd:(hh + 1) * hd]
        vh = kv_ref[:, cw + hh * hd:cw + (hh + 1) * hd]
        s = _dot_nt(qh, kh) * (hd ** -0.5)
        e = jnp.exp(s - jnp.max(s, axis=-1, keepdims=True))
        p = (e / jnp.sum(e, axis=-1, keepdims=True)).astype(BF16)
        outs.append(jnp.dot(p, vh, preferred_element_type=F32))
    o = jnp.concatenate(outs, axis=1).astype(BF16)
    o_ref[...] = x + jnp.dot(o, wo_ref[...], preferred_element_type=F32)


def _cross_block(x2d, g, wq, kv, wo, seq, n_mem, bm):
    m, d = x2d.shape
    bm = min(bm, seq)
    per_b = seq // bm
    cw = wq.shape[1]
    return pl.pallas_call(
        _cross_body,
        grid=(m // bm,),
        in_specs=[
            pl.BlockSpec((bm, d), lambda i: (i, 0)),
            pl.BlockSpec((1, d), lambda i: (0, 0)),
            pl.BlockSpec((d, cw), lambda i: (0, 0)),
            pl.BlockSpec((n_mem, 2 * cw), lambda i: (i // per_b, 0)),
            pl.BlockSpec((cw, d), lambda i: (0, 0)),
        ],
        out_specs=pl.BlockSpec((bm, d), lambda i: (i, 0)),
        out_shape=jax.ShapeDtypeStruct((m, d), F32),
        compiler_params=_cparams("parallel"),
        name="cross_block",
    )(x2d, g.reshape(1, d), wq, kv, wo)


def _router_body(x_ref, g_ref, w_ref, b_ref, idx_ref, gate_ref, rank_ref, cnt_ref, h3_ref, hn_ref, carry_ref):
    @pl.when(pl.program_id(0) == 0)
    def _():
        carry_ref[...] = jnp.zeros_like(carry_ref)

    bm = x_ref.shape[0]
    hn_ref[...] = _rmsnorm(x_ref[...], g_ref[...])
    h3_ref[...] = hn_ref[...].reshape(h3_ref.shape)
    h = hn_ref[...].astype(BF16)
    logits = jnp.dot(h, w_ref[...], preferred_element_type=F32) + b_ref[...]
    lane = lax.broadcasted_iota(jnp.int32, (bm, LANES), 1)
    lane_f = lane.astype(F32)
    work = logits
    self32 = jnp.zeros((bm, LANES), F32)
    vals, picks, hots = [], [], []
    for _ in range(TOP_K):
        mk = jnp.max(work, axis=-1, keepdims=True)
        pick = jnp.min(jnp.where(work == mk, lane_f, float(LANES)), axis=-1, keepdims=True)
        hot = lane_f == pick
        vals.append(mk)
        picks.append(pick.astype(jnp.int32))
        hots.append(hot)
        self32 = jnp.where(hot, 1.0, self32)
        work = jnp.where(hot, -jnp.inf, work)
    es = [jnp.exp(v - vals[0]) for v in vals]
    denom = es[0] + es[1] + es[2] + es[3]

    r = lax.broadcasted_iota(jnp.int32, (bm, bm), 0)
    c = lax.broadcasted_iota(jnp.int32, (bm, bm), 1)
    tri = (c < r).astype(BF16)
    rank = jnp.dot(tri, self32.astype(BF16), preferred_element_type=F32) + carry_ref[0:1, :]
    carry_new = carry_ref[0:1, :] + jnp.sum(self32, axis=0, keepdims=True)
    carry_ref[...] = jnp.broadcast_to(carry_new, carry_ref.shape)

    idx_o = jnp.zeros((bm, LANES), jnp.int32)
    gate_o = jnp.zeros((bm, LANES), F32)
    rank_o = jnp.zeros((bm, LANES), jnp.int32)
    for kk in range(TOP_K):
        rk = jnp.sum(jnp.where(hots[kk], rank, 0.0), axis=-1, keepdims=True).astype(jnp.int32)
        idx_o = jnp.where(lane == kk, picks[kk], idx_o)
        gate_o = jnp.where(lane == kk, es[kk] / denom, gate_o)
        rank_o = jnp.where(lane == kk, rk, rank_o)
    idx_ref[...] = idx_o
    gate_ref[...] = gate_o
    rank_ref[...] = rank_o
    cnt_ref[...] = jnp.broadcast_to(carry_new, cnt_ref.shape).astype(jnp.int32)


def _router(x2d, g, w_pad, b_pad, bm):
    m, d = x2d.shape
    bm = min(bm, m)
    tok = pl.BlockSpec((bm, LANES), lambda i: (i, 0))
    return pl.pallas_call(
        _router_body,
        grid=(m // bm,),
        in_specs=[
            pl.BlockSpec((bm, d), lambda i: (i, 0)),
            pl.BlockSpec((1, d), lambda i: (0, 0)),
            pl.BlockSpec((d, LANES), lambda i: (0, 0)),
            pl.BlockSpec((1, LANES), lambda i: (0, 0)),
        ],
        out_specs=[tok, tok, tok, pl.BlockSpec((8, LANES), lambda i: (0, 0)),
                   pl.BlockSpec((bm, 1, d), lambda i: (i, 0, 0))],
        out_shape=[
            jax.ShapeDtypeStruct((m, LANES), jnp.int32),
            jax.ShapeDtypeStruct((m, LANES), F32),
            jax.ShapeDtypeStruct((m, LANES), jnp.int32),
            jax.ShapeDtypeStruct((8, LANES), jnp.int32),
            jax.ShapeDtypeStruct((m, 1, d), F32),
        ],
        scratch_shapes=[pltpu.VMEM((bm, d), F32), pltpu.VMEM((8, LANES), F32)],
        compiler_params=_cparams("arbitrary"),
        name="router",
    )(x2d, g.reshape(1, d), w_pad, b_pad)


def _dispatch_body(dest_ref, pend_ref, padded_ref, x_ref, xs_ref, z_ref, sem, zsem):
    i = pl.program_id(0)
    bm = x_ref.shape[0]
    n_exp = pend_ref.shape[0]
    tm = z_ref.shape[0]

    def zero_copy(e):
        return pltpu.make_async_copy(z_ref, xs_ref.at[pl.ds(pend_ref[e] - tm, tm)], zsem)

    @pl.when(i == 0)
    def _():
        z_ref[...] = jnp.zeros_like(z_ref)
        for e in range(n_exp):
            @pl.when(padded_ref[e] > 0)
            def _():
                zero_copy(e).start()
        for e in range(n_exp):
            @pl.when(padded_ref[e] > 0)
            def _():
                zero_copy(e).wait()

    def row_copy(r, kk):
        d = dest_ref[(i * bm + r) * TOP_K + kk]
        return pltpu.make_async_copy(x_ref.at[r], xs_ref.at[d], sem)

    def issue(r):
        for kk in range(TOP_K):
            row_copy(r, kk).start(priority=kk % 2)

    def drain(r):
        for kk in range(TOP_K):
            row_copy(r, kk).wait()

    _for_rows(bm, issue)
    _for_rows(bm, drain)


def _dispatch(dest_flat, pend, padded, x3d, n_rows, bm):
    m, _, d = x3d.shape
    bm = min(bm, m)
    grid_spec = pltpu.PrefetchScalarGridSpec(
        num_scalar_prefetch=3,
        grid=(m // bm,),
        in_specs=[pl.BlockSpec((bm, 1, d), lambda i, *_: (i, 0, 0))],
        out_specs=pl.BlockSpec(memory_space=pl.ANY),
        scratch_shapes=[
            pltpu.VMEM((MOE_ROW_TILE, 1, d), F32),
            pltpu.SemaphoreType.DMA,
            pltpu.SemaphoreType.DMA,
        ],
    )
    return pl.pallas_call(
        _dispatch_body,
        grid_spec=grid_spec,
        out_shape=jax.ShapeDtypeStruct((n_rows, 1, d), F32),
        compiler_params=_cparams("arbitrary"),
        name="moe_dispatch",
    )(dest_flat, pend, padded, x3d)


def _moe_body(ie_ref, istart_ref, insub_ref, xs_ref, w1_ref, b1_ref, w2_ref, b2_ref,
              ys_ref, xbf_ref, yacc_ref, st0_ref, st1_ref, fl0_ref, fl1_ref, w1b_ref, w2m_ref, w2b_ref, sems):
    it = pl.program_id(0)
    c = pl.program_id(1)
    n_chunks = pl.num_programs(1)
    nsub = insub_ref[it]
    start = istart_ref[it]
    tm = fl0_ref.shape[0]
    tc = w1_ref.shape[2]
    half = tc // 2
    quarter = tc // 4
    even_lane = lax.broadcasted_iota(jnp.int32, (1, half), 1) % 2 == 0
    stage = (st0_ref, st1_ref)
    flat = (fl0_ref, fl1_ref)
    n_pairs = (nsub + 1) // 2
    n_full = nsub // 2
    odd = nsub % 2 == 1

    def rows(s):
        return pl.ds(pl.multiple_of(s * tm, tm), tm)

    def load_copy(s, k):
        return pltpu.make_async_copy(xs_ref.at[pl.ds(start + s * tm, tm)], stage[k], sems.at[k])

    def store_copy(s, k):
        return pltpu.make_async_copy(stage[k], ys_ref.at[pl.ds(start + s * tm, tm)], sems.at[k])

    def consume(s, k):
        flat[k][...] = stage[k][...].reshape(flat[k].shape)
        xbf_ref[rows(s), :] = flat[k][...].astype(BF16)
        yacc_ref[rows(s), :] = jnp.zeros((tm, yacc_ref.shape[1]), F32)

    def hidden(s):
        return jnp.dot(xbf_ref[rows(s), :], w1b_ref[...], preferred_element_type=F32) + b1_ref[0]

    def accumulated(s, u):
        glu = jnp.minimum(u, SWIGLU_LIMIT)
        act = glu * (1.0 / (1.0 + jnp.exp(-SWIGLU_ALPHA * glu)))
        lin1 = jnp.clip(u, -SWIGLU_LIMIT, SWIGLU_LIMIT) + 1.0
        a_even = act * pltpu.roll(lin1, tc - 1, 1)
        merged = jnp.where(even_lane, a_even[:, :half], pltpu.roll(a_even[:, half:], 1, 1))
        return yacc_ref[rows(s), :] + jnp.dot(merged.astype(BF16), w2b_ref[...], preferred_element_type=F32)

    def emit(s, k, wait_for):
        if wait_for is not None:
            store_copy(wait_for, k).wait()
        stage[k][...] = flat[k][...].reshape(stage[k].shape)
        store_copy(s, k).start()

    @pl.when(nsub > 0)
    def _():
        @pl.when(c == 0)
        def _():
            load_copy(0, 0).start()

            def load_pair(sp, carry):
                s0 = 2 * sp
                s1 = s0 + 1

                @pl.when(s1 < nsub)
                def _():
                    load_copy(s1, 1).start()
                load_copy(s0, 0).wait()
                consume(s0, 0)

                @pl.when(s1 < nsub)
                def _():
                    @pl.when(s1 + 1 < nsub)
                    def _():
                        load_copy(s1 + 1, 0).start()
                    load_copy(s1, 1).wait()
                    consume(s1, 1)
                return carry
            lax.fori_loop(0, n_pairs, load_pair, 0)

        w1b_ref[...] = w1_ref[0].astype(BF16)
        for j in range(w2m_ref.shape[0]):
            cols = slice(j * LANES, (j + 1) * LANES)
            w2m_ref[j, pl.ds(0, quarter, stride=2), :] = w2_ref[0, 0:quarter, cols]
            w2m_ref[j, pl.ds(1, quarter, stride=2), :] = w2_ref[0, quarter:2 * quarter, cols]
            w2b_ref[:, cols] = w2m_ref[j].astype(BF16)

        @pl.when(c < n_chunks - 1)
        def _():
            def pair(sp, carry):
                us = [hidden(2 * sp + k) for k in range(2)]
                for k in range(2):
                    yacc_ref[rows(2 * sp + k), :] = accumulated(2 * sp + k, us[k])
                return carry
            lax.fori_loop(0, n_full, pair, 0)

            @pl.when(odd)
            def _():
                yacc_ref[rows(nsub - 1), :] = accumulated(nsub - 1, hidden(nsub - 1))

        @pl.when(c == n_chunks - 1)
        def _():
            @pl.when(n_full > 0)
            def _():
                us = [hidden(k) for k in range(2)]
                for k in range(2):
                    flat[k][...] = accumulated(k, us[k]) + b2_ref[0]
                for k in range(2):
                    emit(k, k, None)

                def pair(sp, carry):
                    us = [hidden(2 * sp + k) for k in range(2)]
                    for k in range(2):
                        flat[k][...] = accumulated(2 * sp + k, us[k]) + b2_ref[0]
                    for k in range(2):
                        emit(2 * sp + k, k, 2 * sp + k - 2)
                    return carry
                lax.fori_loop(1, n_full, pair, 0)

            @pl.when(odd)
            def _():
                flat[0][...] = accumulated(nsub - 1, hidden(nsub - 1)) + b2_ref[0]

                @pl.when(n_full > 0)
                def _():
                    store_copy(nsub - 3, 0).wait()
                emit(nsub - 1, 0, None)

            store_copy(jnp.where(odd, nsub - 1, nsub - 2), 0).wait()

            @pl.when(nsub > 1)
            def _():
                store_copy(jnp.where(odd, nsub - 2, nsub - 1), 1).wait()


def _moe_mlp(item_e, item_start, item_nsub, xs, w1, b1, w2, b2):
    n_rows, _, d = xs.shape
    n_exp, _, two_i = w1.shape
    tc = min(MOE_COL_CHUNK, two_i)
    n_chunks = two_i // tc
    n_items = item_e.shape[0]
    last = n_chunks - 1

    def chunk(c, it, insub):
        return jnp.where(insub[it] > 0, c, last)

    grid_spec = pltpu.PrefetchScalarGridSpec(
        num_scalar_prefetch=3,
        grid=(n_items, n_chunks),
        in_specs=[
            pl.BlockSpec(memory_space=pl.ANY),
            pl.BlockSpec((1, d, tc), lambda it, c, ie, ist, ins: (ie[it], 0, chunk(c, it, ins))),
            pl.BlockSpec((1, 1, tc), lambda it, c, ie, ist, ins: (ie[it], 0, chunk(c, it, ins))),
            pl.BlockSpec((1, tc // 2, d), lambda it, c, ie, ist, ins: (ie[it], chunk(c, it, ins), 0)),
            pl.BlockSpec((1, 1, d), lambda it, c, ie, ist, ins: (ie[it], 0, 0)),
        ],
        out_specs=pl.BlockSpec(memory_space=pl.ANY),
        scratch_shapes=[
            pltpu.VMEM((MOE_ROWS_PER_ITEM, d), BF16),
            pltpu.VMEM((MOE_ROWS_PER_ITEM, d), F32),
            pltpu.VMEM((MOE_ROW_TILE, 1, d), F32),
            pltpu.VMEM((MOE_ROW_TILE, 1, d), F32),
            pltpu.VMEM((MOE_ROW_TILE, d), F32),
            pltpu.VMEM((MOE_ROW_TILE, d), F32),
            pltpu.VMEM((d, tc), BF16),
            pltpu.VMEM((d // LANES, tc // 2, LANES), F32),
            pltpu.VMEM((tc // 2, d), BF16),
            pltpu.SemaphoreType.DMA((2,)),
        ],
    )
    return pl.pallas_call(
        _moe_body,
        grid_spec=grid_spec,
        out_shape=jax.ShapeDtypeStruct((n_rows, 1, d), F32),
        compiler_params=_cparams("arbitrary", "arbitrary"),
        name="moe_mlp",
    )(item_e, item_start, item_nsub, xs, w1, b1, w2, b2)


def _combine_body(dest_ref, x_ref, gate_ref, g_ref, ys_ref, o_ref, b0, b1, b2, b3, flat_ref, sem, *, final_norm):
    i = pl.program_id(0)
    bm = x_ref.shape[0]
    bufs = (b0, b1, b2, b3)

    def row_copy(r, kk):
        d = dest_ref[(i * bm + r) * TOP_K + kk]
        return pltpu.make_async_copy(ys_ref.at[d], bufs[kk].at[r], sem)

    def issue(r):
        for kk in range(TOP_K):
            row_copy(r, kk).start(priority=kk % 2)

    def drain(r):
        for kk in range(TOP_K):
            row_copy(r, kk).wait()

    _for_rows(bm, issue)
    _for_rows(bm, drain)
    acc = x_ref[...]
    for kk in range(TOP_K):
        flat_ref[...] = bufs[kk][...].reshape(flat_ref.shape)
        acc = acc + gate_ref[:, kk:kk + 1] * flat_ref[...]
    o_ref[...] = _rmsnorm(acc, g_ref[...]) if final_norm else acc


def _combine(dest_flat, x2d, gate4, g, ys, bm, final_norm):
    m, d = x2d.shape
    bm = min(bm, m)
    grid_spec = pltpu.PrefetchScalarGridSpec(
        num_scalar_prefetch=1,
        grid=(m // bm,),
        in_specs=[
            pl.BlockSpec((bm, d), lambda i, *_: (i, 0)),
            pl.BlockSpec((bm, LANES), lambda i, *_: (i, 0)),
            pl.BlockSpec((1, d), lambda i, *_: (0, 0)),
            pl.BlockSpec(memory_space=pl.ANY),
        ],
        out_specs=pl.BlockSpec((bm, d), lambda i, *_: (i, 0)),
        scratch_shapes=[pltpu.VMEM((bm, 1, d), F32)] * TOP_K + [pltpu.VMEM((bm, d), F32), pltpu.SemaphoreType.DMA],
    )
    return pl.pallas_call(
        functools.partial(_combine_body, final_norm=final_norm),
        grid_spec=grid_spec,
        out_shape=jax.ShapeDtypeStruct((m, d), F32),
        compiler_params=_cparams("arbitrary"),
        name="moe_combine",
    )(dest_flat, x2d, gate4, g.reshape(1, d), ys)


def _routing_tables(idx4, rank4, counts, n_assign, n_exp):
    tm, rpi = MOE_ROW_TILE, MOE_ROWS_PER_ITEM
    padded = (counts + tm - 1) // tm * tm
    pend = jnp.cumsum(padded)
    pstart = pend - padded
    dest = (pstart[idx4] + rank4).astype(jnp.int32).reshape(-1)

    n_rows = n_assign + n_exp * tm
    n_items = n_rows // rpi + n_exp
    per_e = (padded + rpi - 1) // rpi
    icum = jnp.cumsum(per_e)
    ioff = icum - per_e
    total = icum[-1]
    ii = jnp.arange(n_items, dtype=jnp.int32)
    ic = jnp.minimum(ii, total - 1)
    e_of = jnp.sum((icum[None, :] <= ic[:, None]).astype(jnp.int32), axis=1)
    e_of = jnp.minimum(e_of, n_exp - 1)
    local = ii - ioff[e_of]
    valid = ii < total
    start = jnp.where(valid, pstart[e_of] + local * rpi, 0).astype(jnp.int32)
    nsub = jnp.where(valid, jnp.clip((padded[e_of] - local * rpi) // tm, 0, rpi // tm), 0).astype(jnp.int32)
    return dest, pend.astype(jnp.int32), padded.astype(jnp.int32), e_of, start, nsub, n_rows


def _alibi_slopes(n):
    start = 2.0 ** (-8.0 / n)
    return np.array([start ** (i + 1) for i in range(n)], dtype=np.float32)


def kernel(x, mem, g_mix, w_in, b_forget, swa_sinks, w_out, g_cross, g_mem, w_cq, w_ckv, w_co, g_moe,
           w_router, b_router, w_mlp1, b_mlp1, w_mlp2, b_mlp2, g_final):
    batch, seq, d = x.shape
    n_mem = mem.shape[1]
    depth = w_in.shape[0]
    n_fox = d // (2 * HEAD_DIM)
    n_swa = d // (2 * HEAD_DIM)
    n_kv = n_swa // SWA_GROUP
    fox_w = n_fox * HEAD_DIM
    swa_q_w = n_swa * HEAD_DIM
    swa_kv_w = n_kv * HEAD_DIM
    o2 = 2 * fox_w
    o3 = 3 * fox_w
    o4 = o3 + n_fox
    o5 = o4 + swa_q_w
    o6 = o5 + swa_kv_w
    n_exp = w_router.shape[2]
    t = batch * seq
    slopes = jnp.asarray(_alibi_slopes(n_swa))

    def dup_heads(wk):
        wk = wk.reshape(d, n_kv, 1, HEAD_DIM)
        return jnp.broadcast_to(wk, (d, n_kv, 2, HEAD_DIM)).reshape(d, n_kv * LANES)

    x2d = x.reshape(t, d)
    for l in range(depth):
        wl = w_in[l]
        w_qk = wl[:, :o2].astype(BF16)
        w_vt = wl[:, o2:o3].T.astype(BF16)
        w_ff = jnp.zeros((d, LANES), BF16).at[:, :n_fox].set(wl[:, o3:o4].astype(BF16))
        b_ff = jnp.zeros((1, LANES), F32).at[0, :n_fox].set(b_forget[l])
        w_swa = jnp.concatenate(
            [wl[:, o4:o5], dup_heads(wl[:, o5:o6]), dup_heads(wl[:, o6:])], axis=1).astype(BF16)

        qk = _norm_matmul(x2d, g_mix[l], w_qk, BF16, 1024, 1024, "fox_qk_proj")
        vt = _norm_matmul(x2d, g_mix[l], w_vt, BF16, 1024, 1024, "fox_vt_proj", transposed=True)
        swa_proj = _norm_matmul(x2d, g_mix[l], w_swa, BF16, 1024, 1024, "swa_proj")
        c = _fox_gate(x2d.reshape(batch, seq, d), g_mix[l], w_ff, b_ff, 512)
        fox = _fox_attention(qk, vt, c, batch, seq, fox_w // LANES, FOX_TILE)
        swa = _swa_attention(swa_proj, slopes, swa_sinks[l], batch, seq, swa_q_w // LANES, n_kv)
        x2d = _out_proj(fox, swa, w_out[l].astype(BF16), x2d, 1024, 1024)

        kv = _norm_matmul(mem.reshape(batch * n_mem, d), g_mem[l], w_ckv[l].astype(BF16), BF16, 1024, 1024,
                          "mem_kv_proj")
        x2d = _cross_block(x2d, g_cross[l], w_cq[l].astype(BF16), kv, w_co[l].astype(BF16), seq, n_mem, 512)

        w_r = jnp.zeros((d, LANES), BF16).at[:, :n_exp].set(w_router[l].astype(BF16))
        b_r = jnp.full((1, LANES), NEG_BIG, F32).at[0, :n_exp].set(b_router[l])
        idx4, gate4, rank4, cnt, h3d = _router(x2d, g_moe[l], w_r, b_r, 512)
        dest, pend, padded, item_e, item_start, item_nsub, n_rows = _routing_tables(
            idx4[:, :TOP_K], rank4[:, :TOP_K], cnt[0, :n_exp], t * TOP_K, n_exp)
        xs = _dispatch(dest, pend, padded, h3d, n_rows, 256)
        two_i = w_mlp1.shape[3]
        ys = _moe_mlp(item_e, item_start, item_nsub, xs, w_mlp1[l], b_mlp1[l].reshape(n_exp, 1, two_i),
                      w_mlp2[l], b_mlp2[l].reshape(n_exp, 1, d))
        x2d = _combine(dest, x2d, gate4, g_final, ys, 256, final_norm=(l == depth - 1))
    return x2d.reshape(batch, seq, d)
```

```python
import functools

import numpy as np
import jax
import jax.numpy as jnp
from jax import lax
from jax.experimental import pallas as pl
from jax.experimental.pallas import tpu as pltpu

F32 = jnp.float32
BF16 = jnp.bfloat16

RMS_EPS = 1e-5
HEAD_DIM = 64
WINDOW = 128
SWA_GROUP = 4
N_CROSS_HEADS = 4
TOP_K = 4
SWIGLU_ALPHA = 1.702
SWIGLU_LIMIT = 7.0

LANES = 128
MXU_DIM = 256
VMEM_LIMIT_BYTES = 56 * 2**20
NEG_BIG = -1e30

FOX_TILE = 1024
SWA_WINDOWS_PER_STEP = 8
ROW_COPY_UNROLL = 8
MOE_ROW_TILE = 256
MOE_ROWS_PER_ITEM = 2304
MOE_COL_CHUNK = 512
MOE_GROUP = 4


def _cparams(*sem):
    return pltpu.CompilerParams(dimension_semantics=sem, vmem_limit_bytes=VMEM_LIMIT_BYTES)


def _rmsnorm(x, g):
    ms = jnp.mean(x * x, axis=-1, keepdims=True)
    return x * lax.rsqrt(ms + RMS_EPS) * g


def _for_rows(n, fn):
    def trip(o, carry):
        for u in range(ROW_COPY_UNROLL):
            fn(o * ROW_COPY_UNROLL + u)
        return carry
    lax.fori_loop(0, n // ROW_COPY_UNROLL, trip, 0)


def _dot_nt(a, b):
    return lax.dot_general(a, b, (((1,), (1,)), ((), ())), preferred_element_type=F32)


def _norm_matmul_body(x_ref, g_ref, w_ref, o_ref, h_ref, *, transposed):
    @pl.when(pl.program_id(1) == 0)
    def _():
        h_ref[...] = _rmsnorm(x_ref[...], g_ref[...]).astype(h_ref.dtype)

    if transposed:
        o_ref[...] = _dot_nt(w_ref[...], h_ref[...]).astype(o_ref.dtype)
    else:
        o_ref[...] = jnp.dot(h_ref[...], w_ref[...], preferred_element_type=F32).astype(o_ref.dtype)


def _norm_matmul(x2d, g, w, out_dtype, bm, bn, name, transposed=False):
    m, d = x2d.shape
    n = w.shape[0] if transposed else w.shape[1]
    bm, bn = min(bm, m), min(bn, n)
    if transposed:
        w_spec = pl.BlockSpec((bn, d), lambda i, j: (j, 0))
        o_spec = pl.BlockSpec((bn, bm), lambda i, j: (j, i))
        o_shape = (n, m)
    else:
        w_spec = pl.BlockSpec((d, bn), lambda i, j: (0, j))
        o_spec = pl.BlockSpec((bm, bn), lambda i, j: (i, j))
        o_shape = (m, n)
    return pl.pallas_call(
        functools.partial(_norm_matmul_body, transposed=transposed),
        grid=(m // bm, n // bn),
        in_specs=[pl.BlockSpec((bm, d), lambda i, j: (i, 0)), pl.BlockSpec((1, d), lambda i, j: (0, 0)), w_spec],
        out_specs=o_spec,
        out_shape=jax.ShapeDtypeStruct(o_shape, out_dtype),
        scratch_shapes=[pltpu.VMEM((bm, d), BF16)],
        compiler_params=_cparams("parallel", "arbitrary"),
        name=name,
    )(x2d, g.reshape(1, d), w)


def _fox_gate_body(x_ref, g_ref, w_ref, b_ref, c_ref, carry_ref):
    @pl.when(pl.program_id(1) == 0)
    def _():
        carry_ref[...] = jnp.zeros_like(carry_ref)

    bs = x_ref.shape[1]
    h = _rmsnorm(x_ref[0], g_ref[...]).astype(BF16)
    z = jnp.dot(h, w_ref[...], preferred_element_type=F32) + b_ref[...]
    logf = jnp.minimum(z, 0.0) - jnp.log1p(jnp.exp(-jnp.abs(z)))
    r = lax.broadcasted_iota(jnp.int32, (bs, bs), 0)
    c = lax.broadcasted_iota(jnp.int32, (bs, bs), 1)
    tri = (c <= r).astype(F32)
    cs = jnp.dot(tri, logf, preferred_element_type=F32, precision=lax.Precision.HIGHEST)
    cs = cs + carry_ref[0:1, :]
    c_ref[0] = cs
    carry_ref[...] = jnp.broadcast_to(cs[bs - 1:bs, :], carry_ref.shape)


def _fox_gate(x, g, w_pad, b_pad, bs):
    b, s, d = x.shape
    bs = min(bs, s)
    return pl.pallas_call(
        _fox_gate_body,
        grid=(b, s // bs),
        in_specs=[
            pl.BlockSpec((1, bs, d), lambda i, j: (i, j, 0)),
            pl.BlockSpec((1, d), lambda i, j: (0, 0)),
            pl.BlockSpec((d, LANES), lambda i, j: (0, 0)),
            pl.BlockSpec((1, LANES), lambda i, j: (0, 0)),
        ],
        out_specs=pl.BlockSpec((1, bs, LANES), lambda i, j: (i, j, 0)),
        out_shape=jax.ShapeDtypeStruct((b, s, LANES), F32),
        scratch_shapes=[pltpu.VMEM((8, LANES), F32)],
        compiler_params=_cparams("parallel", "arbitrary"),
        name="fox_gate",
    )(x, g.reshape(1, d), w_pad, b_pad)


def _fox_attn_body(itab_ref, jtab_ref, q_ref, k_ref, vt_ref, c_ref, o_ref, m_ref, l_ref, acc_ref):
    p = pl.program_id(1)
    st = pl.program_id(2)
    i = itab_ref[st]
    j = jtab_ref[st]
    tq = q_ref.shape[0]
    tk = k_ref.shape[0]
    lane = lax.broadcasted_iota(jnp.int32, (1, LANES), 1)

    @pl.when(j == 0)
    def _():
        m_ref[...] = jnp.full_like(m_ref, NEG_BIG)
        l_ref[...] = jnp.zeros_like(l_ref)
        acc_ref[...] = jnp.zeros_like(acc_ref)

    def step(masked):
        q = q_ref[...]
        k = k_ref[...]
        vt = vt_ref[...]
        cblk = c_ref[0]
        if masked:
            key = lax.broadcasted_iota(jnp.int32, (tk, tq), 0)
            qry = lax.broadcasted_iota(jnp.int32, (tk, tq), 1)
            causal = key <= qry
        scores = []
        for hh in range(2):
            qh = (jnp.where(lane // HEAD_DIM == hh, q, jnp.zeros_like(q)) * (HEAD_DIM ** -0.5)).astype(BF16)
            scores.append(_dot_nt(k, qh))
        n_split = 2 if tq % (2 * MXU_DIM) == 0 else 1
        wq = tq // n_split
        for hh in range(2):
            c_col = jnp.sum(jnp.where(lane == 2 * p + hh, cblk, 0.0), axis=1, keepdims=True)
            for g in range(n_split):
                cols = slice(g * wq, (g + 1) * wq)
                s = scores[hh][:, cols] - c_col
                if masked:
                    s = jnp.where(causal[:, cols], s, NEG_BIG)
                m_prev = m_ref[hh, :, cols]
                m_new = jnp.maximum(m_prev, jnp.max(s, axis=0, keepdims=True))
                alpha = jnp.exp(m_prev - m_new)
                pt = jnp.exp(s - m_new)
                l_ref[hh, :, cols] = alpha * l_ref[hh, :, cols] + jnp.sum(pt, axis=0, keepdims=True)
                acc_ref[hh, :, cols] = alpha * acc_ref[hh, :, cols] + jnp.dot(
                    vt, pt.astype(BF16), preferred_element_type=F32)
                m_ref[hh, :, cols] = m_new

    @pl.when(j < i)
    def _():
        step(False)

    @pl.when(j == i)
    def _():
        step(True)
        row = lax.broadcasted_iota(jnp.int32, (LANES, 1), 0)
        ot = jnp.where(row < HEAD_DIM, acc_ref[0] / l_ref[0], acc_ref[1] / l_ref[1])
        o_ref[...] = ot.T.astype(o_ref.dtype)


def _fox_attention(qk, vt, c, batch, seq, n_pairs, tile):
    tile = min(tile, seq)
    nq = seq // tile
    pairs = [(i, j) for i in range(nq) for j in range(i + 1)]
    itab = jnp.asarray([ij[0] for ij in pairs], jnp.int32)
    jtab = jnp.asarray([ij[1] for ij in pairs], jnp.int32)
    grid_spec = pltpu.PrefetchScalarGridSpec(
        num_scalar_prefetch=2,
        grid=(batch, n_pairs, len(pairs)),
        in_specs=[
            pl.BlockSpec((tile, LANES), lambda b, p, s, it, jt: (b * nq + it[s], p)),
            pl.BlockSpec((tile, LANES), lambda b, p, s, it, jt: (b * nq + jt[s], n_pairs + p)),
            pl.BlockSpec((LANES, tile), lambda b, p, s, it, jt: (p, b * nq + jt[s])),
            pl.BlockSpec((1, tile, LANES), lambda b, p, s, it, jt: (b, jt[s], 0)),
        ],
        out_specs=pl.BlockSpec((tile, LANES), lambda b, p, s, it, jt: (b * nq + it[s], p)),
        scratch_shapes=[
            pltpu.VMEM((2, 1, tile), F32),
            pltpu.VMEM((2, 1, tile), F32),
            pltpu.VMEM((2, LANES, tile), F32),
        ],
    )
    return pl.pallas_call(
        _fox_attn_body,
        grid_spec=grid_spec,
        out_shape=jax.ShapeDtypeStruct((batch * seq, n_pairs * LANES), BF16),
        compiler_params=_cparams("parallel", "parallel", "arbitrary"),
        name="fox_attention",
    )(itab, jtab, qk, qk, vt, c)


def _swa_body(slopes_ref, sinks_ref, q_ref, kp_ref, kc_ref, vp_ref, vc_ref, o_ref):
    pr = pl.program_id(1)
    n = pl.program_id(2)
    w = WINDOW
    n_win = q_ref.shape[0] // w
    lane = lax.broadcasted_iota(jnp.int32, (1, LANES), 1)
    row = lax.broadcasted_iota(jnp.int32, (2 * w, 1), 0)
    head1 = row >= w
    slope = jnp.where(head1, slopes_ref[2 * pr + 1], slopes_ref[2 * pr])
    sink = jnp.where(head1, sinks_ref[2 * pr + 1], sinks_ref[2 * pr])
    qi = lax.broadcasted_iota(jnp.int32, (2 * w, 2 * w), 0) % w
    kj = lax.broadcasted_iota(jnp.int32, (2 * w, 2 * w), 1)
    rel = qi - kj + w
    in_window = (rel >= 0) & (rel < w)
    bias = slope * rel.astype(F32)
    for wi in range(n_win):
        qw = q_ref[wi * w:(wi + 1) * w, :]
        if wi == 0:
            k = jnp.concatenate([kp_ref[...], kc_ref[0:w, :]], axis=0)
            v = jnp.concatenate([vp_ref[...], vc_ref[0:w, :]], axis=0)
            valid = in_window & (kj >= jnp.where(n > 0, 0, w))
        else:
            k = kc_ref[(wi - 1) * w:(wi + 1) * w, :]
            v = vc_ref[(wi - 1) * w:(wi + 1) * w, :]
            valid = in_window
        q2 = jnp.concatenate([jnp.where(lane < HEAD_DIM, qw, jnp.zeros_like(qw)),
                              jnp.where(lane >= HEAD_DIM, qw, jnp.zeros_like(qw))], axis=0)
        s = _dot_nt(q2, k) * (HEAD_DIM ** -0.5) - bias
        s = jnp.where(valid, s, NEG_BIG)
        m = jnp.maximum(jnp.max(s, axis=-1, keepdims=True), sink)
        p = jnp.exp(s - m)
        denom = jnp.sum(p, axis=-1, keepdims=True) + jnp.exp(sink - m)
        o = jnp.dot((p / denom).astype(BF16), v, preferred_element_type=F32)
        o_ref[wi * w:(wi + 1) * w, :] = jnp.where(lane < HEAD_DIM, o[:w], o[w:]).astype(o_ref.dtype)


def _swa_attention(proj, slopes, sinks, batch, seq, n_pairs, n_kv):
    n_win = min(SWA_WINDOWS_PER_STEP, seq // WINDOW)
    blk = n_win * WINDOW
    nb = seq // blk
    wins = seq // WINDOW

    def q_map(b, p, n):
        return (b * nb + n, p)

    def cur_map(base):
        return lambda b, p, n: (b * nb + n, base + p // 2)

    def prev_map(base):
        return lambda b, p, n: (b * wins + jnp.maximum(n * n_win - 1, 0), base + p // 2)

    smem = pl.BlockSpec(memory_space=pltpu.SMEM)
    return pl.pallas_call(
        _swa_body,
        grid=(batch, n_pairs, nb),
        in_specs=[
            smem, smem,
            pl.BlockSpec((blk, LANES), q_map),
            pl.BlockSpec((WINDOW, LANES), prev_map(n_pairs)),
            pl.BlockSpec((blk, LANES), cur_map(n_pairs)),
            pl.BlockSpec((WINDOW, LANES), prev_map(n_pairs + n_kv)),
            pl.BlockSpec((blk, LANES), cur_map(n_pairs + n_kv)),
        ],
        out_specs=pl.BlockSpec((blk, LANES), q_map),
        out_shape=jax.ShapeDtypeStruct((batch * seq, n_pairs * LANES), BF16),
        compiler_params=_cparams("parallel", "parallel", "parallel"),
        name="swa_attention",
    )(slopes, sinks, proj, proj, proj, proj, proj)


def _out_proj_body(a1_ref, a2_ref, w_ref, x_ref, o_ref):
    a = jnp.concatenate([a1_ref[...], a2_ref[...]], axis=1)
    o_ref[...] = x_ref[...] + jnp.dot(a, w_ref[...], preferred_element_type=F32)


def _out_proj(a1, a2, w, x2d, bm, bn):
    m, d = x2d.shape
    k1, k2 = a1.shape[1], a2.shape[1]
    bm, bn = min(bm, m), min(bn, d)
    return pl.pallas_call(
        _out_proj_body,
        grid=(m // bm, d // bn),
        in_specs=[
            pl.BlockSpec((bm, k1), lambda i, j: (i, 0)),
            pl.BlockSpec((bm, k2), lambda i, j: (i, 0)),
            pl.BlockSpec((k1 + k2, bn), lambda i, j: (0, j)),
            pl.BlockSpec((bm, bn), lambda i, j: (i, j)),
        ],
        out_specs=pl.BlockSpec((bm, bn), lambda i, j: (i, j)),
        out_shape=jax.ShapeDtypeStruct((m, d), F32),
        compiler_params=_cparams("parallel", "parallel"),
        name="out_proj",
    )(a1, a2, w, x2d)


def _cross_body(x_ref, g_ref, wq_ref, kv_ref, wo_ref, o_ref):
    x = x_ref[...]
    h = _rmsnorm(x, g_ref[...]).astype(BF16)
    q = jnp.dot(h, wq_ref[...], preferred_element_type=F32).astype(BF16)
    cw = wq_ref.shape[1]
    hd = cw // N_CROSS_HEADS
    outs = []
    for hh in range(N_CROSS_HEADS):
        qh = q[:, hh * hd:(hh + 1) * hd]
        kh = kv_ref[:, hh * hd:(hh + 1) * hd]
        vh = kv_ref[:, cw + hh * hd:cw + (hh + 1) * hd]
        s = _dot_nt(qh, kh) * (hd ** -0.5)
        e = jnp.exp(s - jnp.max(s, axis=-1, keepdims=True))
        p = (e / jnp.sum(e, axis=-1, keepdims=True)).astype(BF16)
        outs.append(jnp.dot(p, vh, preferred_element_type=F32))
    o = jnp.concatenate(outs, axis=1).astype(BF16)
    o_ref[...] = x + jnp.dot(o, wo_ref[...], preferred_element_type=F32)


def _cross_block(x2d, g, wq, kv, wo, seq, n_mem, bm):
    m, d = x2d.shape
    bm = min(bm, seq)
    per_b = seq // bm
    cw = wq.shape[1]
    return pl.pallas_call(
        _cross_body,
        grid=(m // bm,),
        in_specs=[
            pl.BlockSpec((bm, d), lambda i: (i, 0)),
            pl.BlockSpec((1, d), lambda i: (0, 0)),
            pl.BlockSpec((d, cw), lambda i: (0, 0)),
            pl.BlockSpec((n_mem, 2 * cw), lambda i: (i // per_b, 0)),
            pl.BlockSpec((cw, d), lambda i: (0, 0)),
        ],
        out_specs=pl.BlockSpec((bm, d), lambda i: (i, 0)),
        out_shape=jax.ShapeDtypeStruct((m, d), F32),
        compiler_params=_cparams("parallel"),
        name="cross_block",
    )(x2d, g.reshape(1, d), wq, kv, wo)


def _router_body(x_ref, g_ref, w_ref, b_ref, idx_ref, gate_ref, rank_ref, cnt_ref, h3_ref, hn_ref, carry_ref):
    @pl.when(pl.program_id(0) == 0)
    def _():
        carry_ref[...] = jnp.zeros_like(carry_ref)

    bm = x_ref.shape[0]
    hn_ref[...] = _rmsnorm(x_ref[...], g_ref[...])
    h3_ref[...] = hn_ref[...].reshape(h3_ref.shape)
    h = hn_ref[...].astype(BF16)
    logits = jnp.dot(h, w_ref[...], preferred_element_type=F32) + b_ref[...]
    lane = lax.broadcasted_iota(jnp.int32, (bm, LANES), 1)
    lane_f = lane.astype(F32)
    work = logits
    self32 = jnp.zeros((bm, LANES), F32)
    vals, picks, hots = [], [], []
    for _ in range(TOP_K):
        mk = jnp.max(work, axis=-1, keepdims=True)
        pick = jnp.min(jnp.where(work == mk, lane_f, float(LANES)), axis=-1, keepdims=True)
        hot = lane_f == pick
        vals.append(mk)
        picks.append(pick.astype(jnp.int32))
        hots.append(hot)
        self32 = jnp.where(hot, 1.0, self32)
        work = jnp.where(hot, -jnp.inf, work)
    es = [jnp.exp(v - vals[0]) for v in vals]
    denom = es[0] + es[1] + es[2] + es[3]

    r = lax.broadcasted_iota(jnp.int32, (bm, bm), 0)
    c = lax.broadcasted_iota(jnp.int32, (bm, bm), 1)
    tri = (c < r).astype(BF16)
    rank = jnp.dot(tri, self32.astype(BF16), preferred_element_type=F32) + carry_ref[0:1, :]
    carry_new = carry_ref[0:1, :] + jnp.sum(self32, axis=0, keepdims=True)
    carry_ref[...] = jnp.broadcast_to(carry_new, carry_ref.shape)

    idx_o = jnp.zeros((bm, LANES), jnp.int32)
    gate_o = jnp.zeros((bm, LANES), F32)
    rank_o = jnp.zeros((bm, LANES), jnp.int32)
    for kk in range(TOP_K):
        rk = jnp.sum(jnp.where(hots[kk], rank, 0.0), axis=-1, keepdims=True).astype(jnp.int32)
        idx_o = jnp.where(lane == kk, picks[kk], idx_o)
        gate_o = jnp.where(lane == kk, es[kk] / denom, gate_o)
        rank_o = jnp.where(lane == kk, rk, rank_o)
    idx_ref[...] = idx_o
    gate_ref[...] = gate_o
    rank_ref[...] = rank_o
    cnt_ref[...] = jnp.broadcast_to(carry_new, cnt_ref.shape).astype(jnp.int32)


def _router(x2d, g, w_pad, b_pad, bm):
    m, d = x2d.shape
    bm = min(bm, m)
    tok = pl.BlockSpec((bm, LANES), lambda i: (i, 0))
    return pl.pallas_call(
        _router_body,
        grid=(m // bm,),
        in_specs=[
            pl.BlockSpec((bm, d), lambda i: (i, 0)),
            pl.BlockSpec((1, d), lambda i: (0, 0)),
            pl.BlockSpec((d, LANES), lambda i: (0, 0)),
            pl.BlockSpec((1, LANES), lambda i: (0, 0)),
        ],
        out_specs=[tok, tok, tok, pl.BlockSpec((8, LANES), lambda i: (0, 0)),
                   pl.BlockSpec((bm, 1, d), lambda i: (i, 0, 0))],
        out_shape=[
            jax.ShapeDtypeStruct((m, LANES), jnp.int32),
            jax.ShapeDtypeStruct((m, LANES), F32),
            jax.ShapeDtypeStruct((m, LANES), jnp.int32),
            jax.ShapeDtypeStruct((8, LANES), jnp.int32),
            jax.ShapeDtypeStruct((m, 1, d), F32),
        ],
        scratch_shapes=[pltpu.VMEM((bm, d), F32), pltpu.VMEM((8, LANES), F32)],
        compiler_params=_cparams("arbitrary"),
        name="router",
    )(x2d, g.reshape(1, d), w_pad, b_pad)


def _sorted_row(idx_ref, rank_ref, pend_ref, padded_ref, a):
    e = idx_ref[a]
    return pend_ref[e] - padded_ref[e] + rank_ref[a]


def _dispatch_body(idx_ref, rank_ref, pend_ref, padded_ref, x_ref, xs_ref, z_ref, sem, zsem):
    i = pl.program_id(0)
    bm = x_ref.shape[0]
    n_exp = pend_ref.shape[0]
    tm = z_ref.shape[0]

    def zero_copy(e):
        return pltpu.make_async_copy(z_ref, xs_ref.at[pl.ds(pend_ref[e] - tm, tm)], zsem)

    @pl.when(i == 0)
    def _():
        z_ref[...] = jnp.zeros_like(z_ref)
        for e in range(n_exp):
            @pl.when(padded_ref[e] > 0)
            def _():
                zero_copy(e).start()
        for e in range(n_exp):
            @pl.when(padded_ref[e] > 0)
            def _():
                zero_copy(e).wait()

    def row_copy(r, kk):
        d = _sorted_row(idx_ref, rank_ref, pend_ref, padded_ref, (i * bm + r) * TOP_K + kk)
        return pltpu.make_async_copy(x_ref.at[r], xs_ref.at[d], sem)

    def issue(r):
        for kk in range(TOP_K):
            row_copy(r, kk).start(priority=kk % 2)

    def drain(r):
        for kk in range(TOP_K):
            row_copy(r, kk).wait()

    _for_rows(bm, issue)
    _for_rows(bm, drain)


def _dispatch(idx_flat, rank_flat, pend, padded, x3d, n_rows, bm):
    m, _, d = x3d.shape
    bm = min(bm, m)
    grid_spec = pltpu.PrefetchScalarGridSpec(
        num_scalar_prefetch=4,
        grid=(m // bm,),
        in_specs=[pl.BlockSpec((bm, 1, d), lambda i, *_: (i, 0, 0))],
        out_specs=pl.BlockSpec(memory_space=pl.ANY),
        scratch_shapes=[
            pltpu.VMEM((MOE_ROW_TILE, 1, d), F32),
            pltpu.SemaphoreType.DMA,
            pltpu.SemaphoreType.DMA,
        ],
    )
    return pl.pallas_call(
        _dispatch_body,
        grid_spec=grid_spec,
        out_shape=jax.ShapeDtypeStruct((n_rows, 1, d), F32),
        compiler_params=_cparams("arbitrary"),
        name="moe_dispatch",
    )(idx_flat, rank_flat, pend, padded, x3d)


def _moe_body(ie_ref, istart_ref, insub_ref, xs_ref, w1_ref, b1_ref, w2_ref, b2_ref,
              ys_ref, xbf_ref, yacc_ref, st0_ref, st1_ref, w1b_ref, w2m_ref, w2b_ref, sems):
    it = pl.program_id(0)
    c = pl.program_id(1)
    n_chunks = pl.num_programs(1)
    nsub = insub_ref[it]
    start = istart_ref[it]
    tm = st0_ref.shape[0]
    tc = w1_ref.shape[2]
    half = tc // 2
    quarter = tc // 4
    even_lane = lax.broadcasted_iota(jnp.int32, (1, half), 1) % 2 == 0
    stage = (st0_ref, st1_ref)
    n_pairs = (nsub + 1) // 2
    n_full = nsub // 2
    odd = nsub % 2 == 1

    def rows(s):
        return pl.ds(pl.multiple_of(s * tm, tm), tm)

    def load_copy(s, k):
        return pltpu.make_async_copy(xs_ref.at[pl.ds(start + s * tm, tm)], stage[k], sems.at[k])

    def store_copy(s, k):
        return pltpu.make_async_copy(stage[k], ys_ref.at[pl.ds(start + s * tm, tm)], sems.at[k])

    def consume(s, k):
        yacc_ref[rows(s), :] = stage[k][...].reshape(tm, yacc_ref.shape[1])
        xbf_ref[rows(s), :] = yacc_ref[rows(s), :].astype(BF16)
        yacc_ref[rows(s), :] = jnp.zeros((tm, yacc_ref.shape[1]), F32)

    def hidden(rs):
        return jnp.dot(xbf_ref[rs, :], w1b_ref[...], preferred_element_type=F32) + b1_ref[0]

    def accumulated(rs, u):
        glu = jnp.minimum(u, SWIGLU_LIMIT)
        act = glu * (1.0 / (1.0 + jnp.exp(-SWIGLU_ALPHA * glu)))
        lin1 = jnp.clip(u, -SWIGLU_LIMIT, SWIGLU_LIMIT) + 1.0
        a_even = act * pltpu.roll(lin1, tc - 1, 1)
        merged = jnp.where(even_lane, a_even[:, :half], pltpu.roll(a_even[:, half:], 1, 1))
        return yacc_ref[rs, :] + jnp.dot(merged.astype(BF16), w2b_ref[...], preferred_element_type=F32)

    def emit(s, k, wait_for):
        if wait_for is not None:
            store_copy(wait_for, k).wait()
        stage[k][...] = yacc_ref[rows(s), :].reshape(stage[k].shape)
        store_copy(s, k).start()

    @pl.when(nsub > 0)
    def _():
        @pl.when(c == 0)
        def _():
            load_copy(0, 0).start()

            def load_pair(sp, carry):
                s0 = 2 * sp
                s1 = s0 + 1

                @pl.when(s1 < nsub)
                def _():
                    load_copy(s1, 1).start()
                load_copy(s0, 0).wait()
                consume(s0, 0)

                @pl.when(s1 < nsub)
                def _():
                    @pl.when(s1 + 1 < nsub)
                    def _():
                        load_copy(s1 + 1, 0).start()
                    load_copy(s1, 1).wait()
                    consume(s1, 1)
                return carry
            lax.fori_loop(0, n_pairs, load_pair, 0)

        w1b_ref[...] = w1_ref[0].astype(BF16)
        for j in range(w2m_ref.shape[0]):
            cols = slice(j * LANES, (j + 1) * LANES)
            w2m_ref[j, pl.ds(0, quarter, stride=2), :] = w2_ref[0, 0:quarter, cols]
            w2m_ref[j, pl.ds(1, quarter, stride=2), :] = w2_ref[0, quarter:2 * quarter, cols]
            w2b_ref[:, cols] = w2m_ref[j].astype(BF16)

        def group_update(first, n, bias):
            rs = [rows(first + k) for k in range(n)]
            us = [hidden(r) for r in rs]
            for k in range(n):
                yacc_ref[rs[k], :] = accumulated(rs[k], us[k]) + bias

        def pair_update(sp, bias):
            group_update(2 * sp, 2, bias)

        def single_update(s, bias):
            group_update(s, 1, bias)

        @pl.when(c < n_chunks - 1)
        def _():
            n_groups = nsub // MOE_GROUP

            def group(gi, carry):
                group_update(gi * MOE_GROUP, MOE_GROUP, 0.0)
                return carry
            lax.fori_loop(0, n_groups, group, 0)

            def single(s, carry):
                single_update(s, 0.0)
                return carry
            lax.fori_loop(n_groups * MOE_GROUP, nsub, single, 0)

        @pl.when(c == n_chunks - 1)
        def _():
            @pl.when(n_full > 0)
            def _():
                def finish(sp):
                    pair_update(sp, b2_ref[0])

                finish(0)
                for k in range(2):
                    emit(k, k, None)

                def pair(sp, carry):
                    finish(sp)
                    for k in range(2):
                        emit(2 * sp + k, k, 2 * sp + k - 2)
                    return carry
                lax.fori_loop(1, n_full, pair, 0)

            @pl.when(odd)
            def _():
                single_update(nsub - 1, b2_ref[0])

                @pl.when(n_full > 0)
                def _():
                    store_copy(nsub - 3, 0).wait()
                emit(nsub - 1, 0, None)

            store_copy(jnp.where(odd, nsub - 1, nsub - 2), 0).wait()

            @pl.when(nsub > 1)
            def _():
                store_copy(jnp.where(odd, nsub - 2, nsub - 1), 1).wait()


def _moe_mlp(item_e, item_start, item_nsub, xs, w1, b1, w2, b2):
    n_rows, _, d = xs.shape
    n_exp, _, two_i = w1.shape
    tc = min(MOE_COL_CHUNK, two_i)
    n_chunks = two_i // tc
    n_items = item_e.shape[0]
    last = n_chunks - 1

    def chunk(c, it, insub):
        return jnp.where(insub[it] > 0, c, last)

    grid_spec = pltpu.PrefetchScalarGridSpec(
        num_scalar_prefetch=3,
        grid=(n_items, n_chunks),
        in_specs=[
            pl.BlockSpec(memory_space=pl.ANY),
            pl.BlockSpec((1, d, tc), lambda it, c, ie, ist, ins: (ie[it], 0, chunk(c, it, ins))),
            pl.BlockSpec((1, 1, tc), lambda it, c, ie, ist, ins: (ie[it], 0, chunk(c, it, ins))),
            pl.BlockSpec((1, tc // 2, d), lambda it, c, ie, ist, ins: (ie[it], chunk(c, it, ins), 0)),
            pl.BlockSpec((1, 1, d), lambda it, c, ie, ist, ins: (ie[it], 0, 0)),
        ],
        out_specs=pl.BlockSpec(memory_space=pl.ANY),
        scratch_shapes=[
            pltpu.VMEM((MOE_ROWS_PER_ITEM, d), BF16),
            pltpu.VMEM((MOE_ROWS_PER_ITEM, d), F32),
            pltpu.VMEM((MOE_ROW_TILE, 1, d), F32),
            pltpu.VMEM((MOE_ROW_TILE, 1, d), F32),
            pltpu.VMEM((d, tc), BF16),
            pltpu.VMEM((d // LANES, tc // 2, LANES), F32),
            pltpu.VMEM((tc // 2, d), BF16),
            pltpu.SemaphoreType.DMA((2,)),
        ],
    )
    return pl.pallas_call(
        _moe_body,
        grid_spec=grid_spec,
        out_shape=jax.ShapeDtypeStruct((n_rows, 1, d), F32),
        compiler_params=_cparams("arbitrary", "arbitrary"),
        name="moe_mlp",
    )(item_e, item_start, item_nsub, xs, w1, b1, w2, b2)


def _combine_body(idx_ref, rank_ref, pend_ref, padded_ref, x_ref, gate_ref, g_ref, ys_ref, o_ref, b0, b1, b2, b3, flat_ref, sem, *, final_norm):
    i = pl.program_id(0)
    bm = x_ref.shape[0]
    bufs = (b0, b1, b2, b3)

    def row_copy(r, kk):
        d = _sorted_row(idx_ref, rank_ref, pend_ref, padded_ref, (i * bm + r) * TOP_K + kk)
        return pltpu.make_async_copy(ys_ref.at[d], bufs[kk].at[r], sem)

    def issue(r):
        for kk in range(TOP_K):
            row_copy(r, kk).start(priority=kk % 2)

    def drain(r):
        for kk in range(TOP_K):
            row_copy(r, kk).wait()

    _for_rows(bm, issue)
    _for_rows(bm, drain)
    acc = x_ref[...]
    for kk in range(TOP_K):
        flat_ref[...] = bufs[kk][...].reshape(flat_ref.shape)
        acc = acc + gate_ref[:, kk:kk + 1] * flat_ref[...]
    o_ref[...] = _rmsnorm(acc, g_ref[...]) if final_norm else acc


def _combine(idx_flat, rank_flat, pend, padded, x2d, gate4, g, ys, bm, final_norm):
    m, d = x2d.shape
    bm = min(bm, m)
    grid_spec = pltpu.PrefetchScalarGridSpec(
        num_scalar_prefetch=4,
        grid=(m // bm,),
        in_specs=[
            pl.BlockSpec((bm, d), lambda i, *_: (i, 0)),
            pl.BlockSpec((bm, LANES), lambda i, *_: (i, 0)),
            pl.BlockSpec((1, d), lambda i, *_: (0, 0)),
            pl.BlockSpec(memory_space=pl.ANY),
        ],
        out_specs=pl.BlockSpec((bm, d), lambda i, *_: (i, 0)),
        scratch_shapes=[pltpu.VMEM((bm, 1, d), F32)] * TOP_K + [pltpu.VMEM((bm, d), F32), pltpu.SemaphoreType.DMA],
    )
    return pl.pallas_call(
        functools.partial(_combine_body, final_norm=final_norm),
        grid_spec=grid_spec,
        out_shape=jax.ShapeDtypeStruct((m, d), F32),
        compiler_params=_cparams("arbitrary"),
        name="moe_combine",
    )(idx_flat, rank_flat, pend, padded, x2d, gate4, g.reshape(1, d), ys)


def _routing_tables(counts, n_assign, n_exp):
    tm, rpi = MOE_ROW_TILE, MOE_ROWS_PER_ITEM
    padded = (counts + tm - 1) // tm * tm
    pend = jnp.cumsum(padded)
    pstart = pend - padded

    n_rows = n_assign + n_exp * tm
    n_items = n_rows // rpi + n_exp
    per_e = (padded + rpi - 1) // rpi
    icum = jnp.cumsum(per_e)
    ioff = icum - per_e
    total = icum[-1]
    ii = jnp.arange(n_items, dtype=jnp.int32)
    ic = jnp.minimum(ii, total - 1)
    e_of = jnp.sum((icum[None, :] <= ic[:, None]).astype(jnp.int32), axis=1)
    e_of = jnp.minimum(e_of, n_exp - 1)
    local = ii - ioff[e_of]
    valid = ii < total
    start = jnp.where(valid, pstart[e_of] + local * rpi, 0).astype(jnp.int32)
    nsub = jnp.where(valid, jnp.clip((padded[e_of] - local * rpi) // tm, 0, rpi // tm), 0).astype(jnp.int32)
    return pend.astype(jnp.int32), padded.astype(jnp.int32), e_of, start, nsub, n_rows


def _alibi_slopes(n):
    start = 2.0 ** (-8.0 / n)
    return np.array([start ** (i + 1) for i in range(n)], dtype=np.float32)


def kernel(x, mem, g_mix, w_in, b_forget, swa_sinks, w_out, g_cross, g_mem, w_cq, w_ckv, w_co, g_moe,
           w_router, b_router, w_mlp1, b_mlp1, w_mlp2, b_mlp2, g_final):
    batch, seq, d = x.shape
    n_mem = mem.shape[1]
    depth = w_in.shape[0]
    n_fox = d // (2 * HEAD_DIM)
    n_swa = d // (2 * HEAD_DIM)
    n_kv = n_swa // SWA_GROUP
    fox_w = n_fox * HEAD_DIM
    swa_q_w = n_swa * HEAD_DIM
    swa_kv_w = n_kv * HEAD_DIM
    o2 = 2 * fox_w
    o3 = 3 * fox_w
    o4 = o3 + n_fox
    o5 = o4 + swa_q_w
    o6 = o5 + swa_kv_w
    n_exp = w_router.shape[2]
    t = batch * seq
    slopes = jnp.asarray(_alibi_slopes(n_swa))

    def dup_heads(wk):
        wk = wk.reshape(d, n_kv, 1, HEAD_DIM)
        return jnp.broadcast_to(wk, (d, n_kv, 2, HEAD_DIM)).reshape(d, n_kv * LANES)

    x2d = x.reshape(t, d)
    for l in range(depth):
        wl = w_in[l]
        w_qk = wl[:, :o2].astype(BF16)
        w_vt = wl[:, o2:o3].T.astype(BF16)
        w_ff = jnp.zeros((d, LANES), BF16).at[:, :n_fox].set(wl[:, o3:o4].astype(BF16))
        b_ff = jnp.zeros((1, LANES), F32).at[0, :n_fox].set(b_forget[l])
        w_swa = jnp.concatenate(
            [wl[:, o4:o5], dup_heads(wl[:, o5:o6]), dup_heads(wl[:, o6:])], axis=1).astype(BF16)

        qk = _norm_matmul(x2d, g_mix[l], w_qk, BF16, 1024, 1024, "fox_qk_proj")
        vt = _norm_matmul(x2d, g_mix[l], w_vt, BF16, 1024, 1024, "fox_vt_proj", transposed=True)
        swa_proj = _norm_matmul(x2d, g_mix[l], w_swa, BF16, 1024, 1024, "swa_proj")
        c = _fox_gate(x2d.reshape(batch, seq, d), g_mix[l], w_ff, b_ff, 512)
        fox = _fox_attention(qk, vt, c, batch, seq, fox_w // LANES, FOX_TILE)
        swa = _swa_attention(swa_proj, slopes, swa_sinks[l], batch, seq, swa_q_w // LANES, n_kv)
        x2d = _out_proj(fox, swa, w_out[l].astype(BF16), x2d, 1024, 1024)

        kv = _norm_matmul(mem.reshape(batch * n_mem, d), g_mem[l], w_ckv[l].astype(BF16), BF16, 1024, 1024,
                          "mem_kv_proj")
        x2d = _cross_block(x2d, g_cross[l], w_cq[l].astype(BF16), kv, w_co[l].astype(BF16), seq, n_mem, 512)

        w_r = jnp.zeros((d, LANES), BF16).at[:, :n_exp].set(w_router[l].astype(BF16))
        b_r = jnp.full((1, LANES), NEG_BIG, F32).at[0, :n_exp].set(b_router[l])
        idx4, gate4, rank4, cnt, h3d = _router(x2d, g_moe[l], w_r, b_r, 512)
        pend, padded, item_e, item_start, item_nsub, n_rows = _routing_tables(cnt[0, :n_exp], t * TOP_K, n_exp)
        idx_flat = idx4[:, :TOP_K].reshape(-1)
        rank_flat = rank4[:, :TOP_K].reshape(-1)
        xs = _dispatch(idx_flat, rank_flat, pend, padded, h3d, n_rows, 256)
        two_i = w_mlp1.shape[3]
        ys = _moe_mlp(item_e, item_start, item_nsub, xs, w_mlp1[l], b_mlp1[l].reshape(n_exp, 1, two_i),
                      w_mlp2[l], b_mlp2[l].reshape(n_exp, 1, d))
        x2d = _combine(idx_flat, rank_flat, pend, padded, x2d, gate4, g_final, ys, 256,
                       final_norm=(l == depth - 1))
    return x2d.reshape(batch, seq, d)
```

```python
import functools

import numpy as np
import jax
import jax.numpy as jnp
from jax import lax
from jax.experimental import pallas as pl
from jax.experimental.pallas import tpu as pltpu

F32 = jnp.float32
BF16 = jnp.bfloat16

RMS_EPS = 1e-5
HEAD_DIM = 64
WINDOW = 128
SWA_GROUP = 4
N_CROSS_HEADS = 4
TOP_K = 4
SWIGLU_ALPHA = 1.702
SWIGLU_LIMIT = 7.0

LANES = 128
MXU_DIM = 256
VMEM_LIMIT_BYTES = 56 * 2**20
NEG_BIG = -1e30
LOG2E = 1.4426950408889634

FOX_TILE = 1024
SWA_WINDOWS_PER_STEP = 8
ROW_COPY_UNROLL = 8
MOE_ROW_TILE = 256
MOE_ROWS_PER_ITEM = 2304
MOE_COL_CHUNK = 512
MOE_GROUP = 4


def _cparams(*sem):
    return pltpu.CompilerParams(dimension_semantics=sem, vmem_limit_bytes=VMEM_LIMIT_BYTES)


def _rmsnorm(x, g):
    ms = jnp.mean(x * x, axis=-1, keepdims=True)
    return x * lax.rsqrt(ms + RMS_EPS) * g


def _for_rows(n, fn):
    def trip(o, carry):
        for u in range(ROW_COPY_UNROLL):
            fn(o * ROW_COPY_UNROLL + u)
        return carry
    lax.fori_loop(0, n // ROW_COPY_UNROLL, trip, 0)


def _dot_nt(a, b):
    return lax.dot_general(a, b, (((1,), (1,)), ((), ())), preferred_element_type=F32)


def _norm_matmul_body(x_ref, g_ref, w_ref, o_ref, h_ref, *, transposed, lead_tiles, lead_scale):
    @pl.when(pl.program_id(1) == 0)
    def _():
        h_ref[...] = _rmsnorm(x_ref[...], g_ref[...]).astype(h_ref.dtype)

    if transposed:
        o_ref[...] = _dot_nt(w_ref[...], h_ref[...]).astype(o_ref.dtype)
    else:
        acc = jnp.dot(h_ref[...], w_ref[...], preferred_element_type=F32)
        if lead_tiles:
            acc = acc * jnp.where(pl.program_id(1) < lead_tiles, lead_scale, 1.0)
        o_ref[...] = acc.astype(o_ref.dtype)


def _norm_matmul(x2d, g, w, out_dtype, bm, bn, name, transposed=False, lead_tiles=0, lead_scale=1.0):
    m, d = x2d.shape
    n = w.shape[0] if transposed else w.shape[1]
    bm, bn = min(bm, m), min(bn, n)
    if transposed:
        w_spec = pl.BlockSpec((bn, d), lambda i, j: (j, 0))
        o_spec = pl.BlockSpec((bn, bm), lambda i, j: (j, i))
        o_shape = (n, m)
    else:
        w_spec = pl.BlockSpec((d, bn), lambda i, j: (0, j))
        o_spec = pl.BlockSpec((bm, bn), lambda i, j: (i, j))
        o_shape = (m, n)
    return pl.pallas_call(
        functools.partial(_norm_matmul_body, transposed=transposed, lead_tiles=lead_tiles, lead_scale=lead_scale),
        grid=(m // bm, n // bn),
        in_specs=[pl.BlockSpec((bm, d), lambda i, j: (i, 0)), pl.BlockSpec((1, d), lambda i, j: (0, 0)), w_spec],
        out_specs=o_spec,
        out_shape=jax.ShapeDtypeStruct(o_shape, out_dtype),
        scratch_shapes=[pltpu.VMEM((bm, d), BF16)],
        compiler_params=_cparams("parallel", "arbitrary"),
        name=name,
    )(x2d, g.reshape(1, d), w)


def _fox_gate_body(x_ref, g_ref, w_ref, b_ref, c_ref, carry_ref):
    @pl.when(pl.program_id(1) == 0)
    def _():
        carry_ref[...] = jnp.zeros_like(carry_ref)

    bs = x_ref.shape[1]
    h = _rmsnorm(x_ref[0], g_ref[...]).astype(BF16)
    z = jnp.dot(h, w_ref[...], preferred_element_type=F32) + b_ref[...]
    logf = jnp.minimum(z, 0.0) - jnp.log1p(jnp.exp(-jnp.abs(z)))
    r = lax.broadcasted_iota(jnp.int32, (bs, bs), 0)
    c = lax.broadcasted_iota(jnp.int32, (bs, bs), 1)
    tri = (c <= r).astype(F32)
    cs = jnp.dot(tri, logf, preferred_element_type=F32, precision=lax.Precision.HIGHEST)
    cs = cs + carry_ref[0:1, :]
    c_ref[0] = cs * LOG2E
    carry_ref[...] = jnp.broadcast_to(cs[bs - 1:bs, :], carry_ref.shape)


def _fox_gate(x, g, w_pad, b_pad, bs):
    b, s, d = x.shape
    bs = min(bs, s)
    return pl.pallas_call(
        _fox_gate_body,
        grid=(b, s // bs),
        in_specs=[
            pl.BlockSpec((1, bs, d), lambda i, j: (i, j, 0)),
            pl.BlockSpec((1, d), lambda i, j: (0, 0)),
            pl.BlockSpec((d, LANES), lambda i, j: (0, 0)),
            pl.BlockSpec((1, LANES), lambda i, j: (0, 0)),
        ],
        out_specs=pl.BlockSpec((1, bs, LANES), lambda i, j: (i, j, 0)),
        out_shape=jax.ShapeDtypeStruct((b, s, LANES), F32),
        scratch_shapes=[pltpu.VMEM((8, LANES), F32)],
        compiler_params=_cparams("parallel", "arbitrary"),
        name="fox_gate",
    )(x, g.reshape(1, d), w_pad, b_pad)


def _fox_attn_body(itab_ref, jtab_ref, q_ref, k_ref, vt_ref, c_ref, o_ref, m_ref, l_ref, acc_ref):
    p = pl.program_id(1)
    st = pl.program_id(2)
    i = itab_ref[st]
    j = jtab_ref[st]
    tq = q_ref.shape[0]
    tk = k_ref.shape[0]
    lane = lax.broadcasted_iota(jnp.int32, (1, LANES), 1)

    @pl.when(j == 0)
    def _():
        m_ref[...] = jnp.full_like(m_ref, NEG_BIG)
        l_ref[...] = jnp.zeros_like(l_ref)
        acc_ref[...] = jnp.zeros_like(acc_ref)

    def step(masked):
        q = q_ref[...]
        k = k_ref[...]
        vt = vt_ref[...]
        cblk = c_ref[0]
        if masked:
            key = lax.broadcasted_iota(jnp.int32, (tk, tq), 0)
            qry = lax.broadcasted_iota(jnp.int32, (tk, tq), 1)
            causal = key <= qry
        scores = []
        for hh in range(2):
            qh = jnp.where(lane // HEAD_DIM == hh, q, jnp.zeros_like(q))
            scores.append(_dot_nt(k, qh))
        n_split = 2 if tq % (2 * MXU_DIM) == 0 else 1
        wq = tq // n_split
        for hh in range(2):
            c_col = jnp.sum(jnp.where(lane == 2 * p + hh, cblk, 0.0), axis=1, keepdims=True)
            for g in range(n_split):
                cols = slice(g * wq, (g + 1) * wq)
                s = scores[hh][:, cols] - c_col
                if masked:
                    s = jnp.where(causal[:, cols], s, NEG_BIG)
                m_prev = m_ref[hh, :, cols]
                m_new = jnp.maximum(m_prev, jnp.max(s, axis=0, keepdims=True))
                alpha = jnp.exp2(m_prev - m_new)
                pt = jnp.exp2(s - m_new)
                l_ref[hh, :, cols] = alpha * l_ref[hh, :, cols] + jnp.sum(pt, axis=0, keepdims=True)
                acc_ref[hh, :, cols] = alpha * acc_ref[hh, :, cols] + jnp.dot(
                    vt, pt.astype(BF16), preferred_element_type=F32)
                m_ref[hh, :, cols] = m_new

    @pl.when(j < i)
    def _():
        step(False)

    @pl.when(j == i)
    def _():
        step(True)
        row = lax.broadcasted_iota(jnp.int32, (LANES, 1), 0)
        ot = jnp.where(row < HEAD_DIM, acc_ref[0] / l_ref[0], acc_ref[1] / l_ref[1])
        o_ref[...] = ot.T.astype(o_ref.dtype)


def _fox_attention(qk, vt, c, batch, seq, n_pairs, tile):
    tile = min(tile, seq)
    nq = seq // tile
    pairs = [(i, j) for i in range(nq) for j in range(i + 1)]
    itab = jnp.asarray([ij[0] for ij in pairs], jnp.int32)
    jtab = jnp.asarray([ij[1] for ij in pairs], jnp.int32)
    grid_spec = pltpu.PrefetchScalarGridSpec(
        num_scalar_prefetch=2,
        grid=(batch, n_pairs, len(pairs)),
        in_specs=[
            pl.BlockSpec((tile, LANES), lambda b, p, s, it, jt: (b * nq + it[s], p)),
            pl.BlockSpec((tile, LANES), lambda b, p, s, it, jt: (b * nq + jt[s], n_pairs + p)),
            pl.BlockSpec((LANES, tile), lambda b, p, s, it, jt: (p, b * nq + jt[s])),
            pl.BlockSpec((1, tile, LANES), lambda b, p, s, it, jt: (b, jt[s], 0)),
        ],
        out_specs=pl.BlockSpec((tile, LANES), lambda b, p, s, it, jt: (b * nq + it[s], p)),
        scratch_shapes=[
            pltpu.VMEM((2, 1, tile), F32),
            pltpu.VMEM((2, 1, tile), F32),
            pltpu.VMEM((2, LANES, tile), F32),
        ],
    )
    return pl.pallas_call(
        _fox_attn_body,
        grid_spec=grid_spec,
        out_shape=jax.ShapeDtypeStruct((batch * seq, n_pairs * LANES), BF16),
        compiler_params=_cparams("parallel", "parallel", "arbitrary"),
        name="fox_attention",
    )(itab, jtab, qk, qk, vt, c)


def _swa_body(slopes_ref, sinks_ref, q_ref, kp_ref, kc_ref, vp_ref, vc_ref, o_ref):
    pr = pl.program_id(1)
    n = pl.program_id(2)
    w = WINDOW
    n_win = q_ref.shape[0] // w
    lane = lax.broadcasted_iota(jnp.int32, (1, LANES), 1)
    row = lax.broadcasted_iota(jnp.int32, (2 * w, 1), 0)
    head1 = row >= w
    slope = jnp.where(head1, slopes_ref[2 * pr + 1], slopes_ref[2 * pr])
    sink = jnp.where(head1, sinks_ref[2 * pr + 1], sinks_ref[2 * pr])
    qi = lax.broadcasted_iota(jnp.int32, (2 * w, 2 * w), 0) % w
    kj = lax.broadcasted_iota(jnp.int32, (2 * w, 2 * w), 1)
    rel = qi - kj + w
    in_window = (rel >= 0) & (rel < w)
    bias = slope * rel.astype(F32)
    for wi in range(n_win):
        qw = q_ref[wi * w:(wi + 1) * w, :]
        if wi == 0:
            k = jnp.concatenate([kp_ref[...], kc_ref[0:w, :]], axis=0)
            v = jnp.concatenate([vp_ref[...], vc_ref[0:w, :]], axis=0)
            valid = in_window & (kj >= jnp.where(n > 0, 0, w))
        else:
            k = kc_ref[(wi - 1) * w:(wi + 1) * w, :]
            v = vc_ref[(wi - 1) * w:(wi + 1) * w, :]
            valid = in_window
        q2 = jnp.concatenate([jnp.where(lane < HEAD_DIM, qw, jnp.zeros_like(qw)),
                              jnp.where(lane >= HEAD_DIM, qw, jnp.zeros_like(qw))], axis=0)
        s = _dot_nt(q2, k) * (HEAD_DIM ** -0.5) - bias
        s = jnp.where(valid, s, NEG_BIG)
        m = jnp.maximum(jnp.max(s, axis=-1, keepdims=True), sink)
        p = jnp.exp(s - m)
        denom = jnp.sum(p, axis=-1, keepdims=True) + jnp.exp(sink - m)
        o = jnp.dot((p / denom).astype(BF16), v, preferred_element_type=F32)
        o_ref[wi * w:(wi + 1) * w, :] = jnp.where(lane < HEAD_DIM, o[:w], o[w:]).astype(o_ref.dtype)


def _swa_attention(proj, slopes, sinks, batch, seq, n_pairs, n_kv):
    n_win = min(SWA_WINDOWS_PER_STEP, seq // WINDOW)
    blk = n_win * WINDOW
    nb = seq // blk
    wins = seq // WINDOW

    def q_map(b, p, n):
        return (b * nb + n, p)

    def cur_map(base):
        return lambda b, p, n: (b * nb + n, base + p // 2)

    def prev_map(base):
        return lambda b, p, n: (b * wins + jnp.maximum(n * n_win - 1, 0), base + p // 2)

    smem = pl.BlockSpec(memory_space=pltpu.SMEM)
    return pl.pallas_call(
        _swa_body,
        grid=(batch, n_pairs, nb),
        in_specs=[
            smem, smem,
            pl.BlockSpec((blk, LANES), q_map),
            pl.BlockSpec((WINDOW, LANES), prev_map(n_pairs)),
            pl.BlockSpec((blk, LANES), cur_map(n_pairs)),
            pl.BlockSpec((WINDOW, LANES), prev_map(n_pairs + n_kv)),
            pl.BlockSpec((blk, LANES), cur_map(n_pairs + n_kv)),
        ],
        out_specs=pl.BlockSpec((blk, LANES), q_map),
        out_shape=jax.ShapeDtypeStruct((batch * seq, n_pairs * LANES), BF16),
        compiler_params=_cparams("parallel", "parallel", "parallel"),
        name="swa_attention",
    )(slopes, sinks, proj, proj, proj, proj, proj)


def _out_proj_body(a1_ref, a2_ref, w_ref, x_ref, o_ref):
    a = jnp.concatenate([a1_ref[...], a2_ref[...]], axis=1)
    o_ref[...] = x_ref[...] + jnp.dot(a, w_ref[...], preferred_element_type=F32)


def _out_proj(a1, a2, w, x2d, bm, bn):
    m, d = x2d.shape
    k1, k2 = a1.shape[1], a2.shape[1]
    bm, bn = min(bm, m), min(bn, d)
    return pl.pallas_call(
        _out_proj_body,
        grid=(m // bm, d // bn),
        in_specs=[
            pl.BlockSpec((bm, k1), lambda i, j: (i, 0)),
            pl.BlockSpec((bm, k2), lambda i, j: (i, 0)),
            pl.BlockSpec((k1 + k2, bn), lambda i, j: (0, j)),
            pl.BlockSpec((bm, bn), lambda i, j: (i, j)),
        ],
        out_specs=pl.BlockSpec((bm, bn), lambda i, j: (i, j)),
        out_shape=jax.ShapeDtypeStruct((m, d), F32),
        compiler_params=_cparams("parallel", "parallel"),
        name="out_proj",
    )(a1, a2, w, x2d)


def _cross_body(x_ref, g_ref, wq_ref, kv_ref, wo_ref, o_ref):
    x = x_ref[...]
    h = _rmsnorm(x, g_ref[...]).astype(BF16)
    q = jnp.dot(h, wq_ref[...], preferred_element_type=F32).astype(BF16)
    cw = wq_ref.shape[1]
    hd = cw // N_CROSS_HEADS
    outs = []
    for hh in range(N_CROSS_HEADS):
        qh = q[:, hh * hd:(hh + 1) * hd]
        kh = kv_ref[:, hh * hd:(hh + 1) * hd]
        vh = kv_ref[:, cw + hh * hd:cw + (hh + 1) * hd]
        s = _dot_nt(qh, kh) * (hd ** -0.5)
        e = jnp.exp(s - jnp.max(s, axis=-1, keepdims=True))
        p = (e / jnp.sum(e, axis=-1, keepdims=True)).astype(BF16)
        outs.append(jnp.dot(p, vh, preferred_element_type=F32))
    o = jnp.concatenate(outs, axis=1).astype(BF16)
    o_ref[...] = x + jnp.dot(o, wo_ref[...], preferred_element_type=F32)


def _cross_block(x2d, g, wq, kv, wo, seq, n_mem, bm):
    m, d = x2d.shape
    bm = min(bm, seq)
    per_b = seq // bm
    cw = wq.shape[1]
    return pl.pallas_call(
        _cross_body,
        grid=(m // bm,),
        in_specs=[
            pl.BlockSpec((bm, d), lambda i: (i, 0)),
            pl.BlockSpec((1, d), lambda i: (0, 0)),
            pl.BlockSpec((d, cw), lambda i: (0, 0)),
            pl.BlockSpec((n_mem, 2 * cw), lambda i: (i // per_b, 0)),
            pl.BlockSpec((cw, d), lambda i: (0, 0)),
        ],
        out_specs=pl.BlockSpec((bm, d), lambda i: (i, 0)),
        out_shape=jax.ShapeDtypeStruct((m, d), F32),
        compiler_params=_cparams("parallel"),
        name="cross_block",
    )(x2d, g.reshape(1, d), wq, kv, wo)


def _router_body(x_ref, g_ref, w_ref, b_ref, idx_ref, gate_ref, rank_ref, cnt_ref, h3_ref, hn_ref, carry_ref):
    @pl.when(pl.program_id(0) == 0)
    def _():
        carry_ref[...] = jnp.zeros_like(carry_ref)

    bm = x_ref.shape[0]
    h = _rmsnorm(x_ref[...], g_ref[...]).astype(BF16)
    bits = lax.bitcast_convert_type(h.astype(F32), jnp.int32)
    hw = bits.shape[1] // 2
    hn_ref[...] = lax.shift_right_logical(bits[:, :hw], 16) | (bits[:, hw:] & jnp.int32(-65536))
    h3_ref[...] = hn_ref[...].reshape(h3_ref.shape)
    logits = jnp.dot(h, w_ref[...], preferred_element_type=F32) + b_ref[...]
    lane = lax.broadcasted_iota(jnp.int32, (bm, LANES), 1)
    lane_f = lane.astype(F32)
    work = logits
    self32 = jnp.zeros((bm, LANES), F32)
    vals, picks, hots = [], [], []
    for _ in range(TOP_K):
        mk = jnp.max(work, axis=-1, keepdims=True)
        pick = jnp.min(jnp.where(work == mk, lane_f, float(LANES)), axis=-1, keepdims=True)
        hot = lane_f == pick
        vals.append(mk)
        picks.append(pick.astype(jnp.int32))
        hots.append(hot)
        self32 = jnp.where(hot, 1.0, self32)
        work = jnp.where(hot, -jnp.inf, work)
    es = [jnp.exp(v - vals[0]) for v in vals]
    denom = es[0] + es[1] + es[2] + es[3]

    r = lax.broadcasted_iota(jnp.int32, (bm, bm), 0)
    c = lax.broadcasted_iota(jnp.int32, (bm, bm), 1)
    tri = (c < r).astype(BF16)
    rank = jnp.dot(tri, self32.astype(BF16), preferred_element_type=F32) + carry_ref[0:1, :]
    carry_new = carry_ref[0:1, :] + jnp.sum(self32, axis=0, keepdims=True)
    carry_ref[...] = jnp.broadcast_to(carry_new, carry_ref.shape)

    idx_o = jnp.zeros((bm, LANES), jnp.int32)
    gate_o = jnp.zeros((bm, LANES), F32)
    rank_o = jnp.zeros((bm, LANES), jnp.int32)
    for kk in range(TOP_K):
        rk = jnp.sum(jnp.where(hots[kk], rank, 0.0), axis=-1, keepdims=True).astype(jnp.int32)
        idx_o = jnp.where(lane == kk, picks[kk], idx_o)
        gate_o = jnp.where(lane == kk, es[kk] / denom, gate_o)
        rank_o = jnp.where(lane == kk, rk, rank_o)
    idx_ref[...] = idx_o
    gate_ref[...] = gate_o
    rank_ref[...] = rank_o
    cnt_ref[...] = jnp.broadcast_to(carry_new, cnt_ref.shape).astype(jnp.int32)


def _router(x2d, g, w_pad, b_pad, bm):
    m, d = x2d.shape
    bm = min(bm, m)
    tok = pl.BlockSpec((bm, LANES), lambda i: (i, 0))
    return pl.pallas_call(
        _router_body,
        grid=(m // bm,),
        in_specs=[
            pl.BlockSpec((bm, d), lambda i: (i, 0)),
            pl.BlockSpec((1, d), lambda i: (0, 0)),
            pl.BlockSpec((d, LANES), lambda i: (0, 0)),
            pl.BlockSpec((1, LANES), lambda i: (0, 0)),
        ],
        out_specs=[tok, tok, tok, pl.BlockSpec((8, LANES), lambda i: (0, 0)),
                   pl.BlockSpec((bm, 1, d // 2), lambda i: (i, 0, 0))],
        out_shape=[
            jax.ShapeDtypeStruct((m, LANES), jnp.int32),
            jax.ShapeDtypeStruct((m, LANES), F32),
            jax.ShapeDtypeStruct((m, LANES), jnp.int32),
            jax.ShapeDtypeStruct((8, LANES), jnp.int32),
            jax.ShapeDtypeStruct((m, 1, d // 2), jnp.int32),
        ],
        scratch_shapes=[pltpu.VMEM((bm, d // 2), jnp.int32), pltpu.VMEM((8, LANES), F32)],
        compiler_params=_cparams("arbitrary"),
        name="router",
    )(x2d, g.reshape(1, d), w_pad, b_pad)


def _sorted_row(idx_ref, rank_ref, pend_ref, padded_ref, a):
    e = idx_ref[a]
    return pend_ref[e] - padded_ref[e] + rank_ref[a]


def _dispatch_body(idx_ref, rank_ref, pend_ref, padded_ref, x_ref, xs_ref, z_ref, sem, zsem):
    i = pl.program_id(0)
    bm = x_ref.shape[0]
    n_exp = pend_ref.shape[0]
    tm = z_ref.shape[0]

    def zero_copy(e):
        return pltpu.make_async_copy(z_ref, xs_ref.at[pl.ds(pend_ref[e] - tm, tm)], zsem)

    @pl.when(i == 0)
    def _():
        z_ref[...] = jnp.zeros_like(z_ref)
        for e in range(n_exp):
            @pl.when(padded_ref[e] > 0)
            def _():
                zero_copy(e).start()
        for e in range(n_exp):
            @pl.when(padded_ref[e] > 0)
            def _():
                zero_copy(e).wait()

    def row_copy(r, kk):
        d = _sorted_row(idx_ref, rank_ref, pend_ref, padded_ref, (i * bm + r) * TOP_K + kk)
        return pltpu.make_async_copy(x_ref.at[r], xs_ref.at[d], sem)

    def issue(r):
        for kk in range(TOP_K):
            row_copy(r, kk).start(priority=kk % 2)

    def drain(r):
        for kk in range(TOP_K):
            row_copy(r, kk).wait()

    _for_rows(bm, issue)
    _for_rows(bm, drain)


def _dispatch(idx_flat, rank_flat, pend, padded, x3d, n_rows, bm):
    m, _, d = x3d.shape
    bm = min(bm, m)
    grid_spec = pltpu.PrefetchScalarGridSpec(
        num_scalar_prefetch=4,
        grid=(m // bm,),
        in_specs=[pl.BlockSpec((bm, 1, d), lambda i, *_: (i, 0, 0))],
        out_specs=pl.BlockSpec(memory_space=pl.ANY),
        scratch_shapes=[
            pltpu.VMEM((MOE_ROW_TILE, 1, d), x3d.dtype),
            pltpu.SemaphoreType.DMA,
            pltpu.SemaphoreType.DMA,
        ],
    )
    return pl.pallas_call(
        _dispatch_body,
        grid_spec=grid_spec,
        out_shape=jax.ShapeDtypeStruct((n_rows, 1, d), x3d.dtype),
        compiler_params=_cparams("arbitrary"),
        name="moe_dispatch",
    )(idx_flat, rank_flat, pend, padded, x3d)


def _moe_body(ie_ref, istart_ref, insub_ref, xs_ref, w1_ref, b1_ref, w2_ref, b2_ref,
              ys_ref, xbf_ref, yacc_ref, st0_ref, st1_ref, ld0_ref, ld1_ref, land_ref, w1b_ref, w2m_ref, w2b_ref,
              sems):
    it = pl.program_id(0)
    c = pl.program_id(1)
    n_chunks = pl.num_programs(1)
    nsub = insub_ref[it]
    start = istart_ref[it]
    tm = st0_ref.shape[0]
    tc = w1_ref.shape[2]
    half = tc // 2
    quarter = tc // 4
    even_lane = lax.broadcasted_iota(jnp.int32, (1, half), 1) % 2 == 0
    stage = (st0_ref, st1_ref)
    load_stage = (ld0_ref, ld1_ref)
    n_pairs = (nsub + 1) // 2
    n_full = nsub // 2
    odd = nsub % 2 == 1

    def rows(s):
        return pl.ds(pl.multiple_of(s * tm, tm), tm)

    def load_copy(s, k):
        return pltpu.make_async_copy(xs_ref.at[pl.ds(start + s * tm, tm)], load_stage[k], sems.at[k])

    def store_copy(s, k):
        return pltpu.make_async_copy(stage[k], ys_ref.at[pl.ds(start + s * tm, tm)], sems.at[k])

    def consume(s, k):
        land_ref[...] = load_stage[k][...].reshape(land_ref.shape)
        word = land_ref[...]
        hw = word.shape[1]
        lo = lax.bitcast_convert_type(lax.shift_left(word, 16), F32)
        hi = lax.bitcast_convert_type(word & jnp.int32(-65536), F32)
        xbf_ref[rows(s), :hw] = lo.astype(BF16)
        xbf_ref[rows(s), hw:] = hi.astype(BF16)
        yacc_ref[rows(s), :] = jnp.zeros((tm, yacc_ref.shape[1]), F32)

    def hidden(rs):
        return jnp.dot(xbf_ref[rs, :], w1b_ref[...], preferred_element_type=F32) + b1_ref[0]

    def accumulated(rs, u):
        glu = jnp.minimum(u, SWIGLU_LIMIT)
        act = glu * (1.0 / (1.0 + jnp.exp(-SWIGLU_ALPHA * glu)))
        lin1 = jnp.clip(u, -SWIGLU_LIMIT, SWIGLU_LIMIT) + 1.0
        a_even = act * pltpu.roll(lin1, tc - 1, 1)
        merged = jnp.where(even_lane, a_even[:, :half], pltpu.roll(a_even[:, half:], 1, 1))
        return yacc_ref[rs, :] + jnp.dot(merged.astype(BF16), w2b_ref[...], preferred_element_type=F32)

    def emit(s, k, wait_for):
        if wait_for is not None:
            store_copy(wait_for, k).wait()
        stage[k][...] = yacc_ref[rows(s), :].reshape(stage[k].shape)
        store_copy(s, k).start()

    @pl.when(nsub > 0)
    def _():
        @pl.when(c == 0)
        def _():
            load_copy(0, 0).start()

            def load_pair(sp, carry):
                s0 = 2 * sp
                s1 = s0 + 1

                @pl.when(s1 < nsub)
                def _():
                    load_copy(s1, 1).start()
                load_copy(s0, 0).wait()
                consume(s0, 0)

                @pl.when(s1 < nsub)
                def _():
                    @pl.when(s1 + 1 < nsub)
                    def _():
                        load_copy(s1 + 1, 0).start()
                    load_copy(s1, 1).wait()
                    consume(s1, 1)
                return carry
            lax.fori_loop(0, n_pairs, load_pair, 0)

        w1b_ref[...] = w1_ref[0].astype(BF16)
        for j in range(w2m_ref.shape[0]):
            cols = slice(j * LANES, (j + 1) * LANES)
            w2m_ref[j, pl.ds(0, quarter, stride=2), :] = w2_ref[0, 0:quarter, cols]
            w2m_ref[j, pl.ds(1, quarter, stride=2), :] = w2_ref[0, quarter:2 * quarter, cols]
            w2b_ref[:, cols] = w2m_ref[j].astype(BF16)

        def group_update(first, n, bias):
            rs = [rows(first + k) for k in range(n)]
            us = [hidden(r) for r in rs]
            for k in range(n):
                yacc_ref[rs[k], :] = accumulated(rs[k], us[k]) + bias

        def pair_update(sp, bias):
            group_update(2 * sp, 2, bias)

        def single_update(s, bias):
            group_update(s, 1, bias)

        @pl.when(c < n_chunks - 1)
        def _():
            n_groups = nsub // MOE_GROUP

            def group(gi, carry):
                group_update(gi * MOE_GROUP, MOE_GROUP, 0.0)
                return carry
            lax.fori_loop(0, n_groups, group, 0)

            def single(s, carry):
                single_update(s, 0.0)
                return carry
            lax.fori_loop(n_groups * MOE_GROUP, nsub, single, 0)

        @pl.when(c == n_chunks - 1)
        def _():
            @pl.when(n_full > 0)
            def _():
                def finish(sp):
                    pair_update(sp, b2_ref[0])

                finish(0)
                for k in range(2):
                    emit(k, k, None)

                def pair(sp, carry):
                    finish(sp)
                    for k in range(2):
                        emit(2 * sp + k, k, 2 * sp + k - 2)
                    return carry
                lax.fori_loop(1, n_full, pair, 0)

            @pl.when(odd)
            def _():
                single_update(nsub - 1, b2_ref[0])

                @pl.when(n_full > 0)
                def _():
                    store_copy(nsub - 3, 0).wait()
                emit(nsub - 1, 0, None)

            store_copy(jnp.where(odd, nsub - 1, nsub - 2), 0).wait()

            @pl.when(nsub > 1)
            def _():
                store_copy(jnp.where(odd, nsub - 2, nsub - 1), 1).wait()


def _moe_mlp(item_e, item_start, item_nsub, xs, w1, b1, w2, b2):
    n_rows, _, dw = xs.shape
    n_exp, d, two_i = w1.shape
    tc = min(MOE_COL_CHUNK, two_i)
    n_chunks = two_i // tc
    n_items = item_e.shape[0]
    last = n_chunks - 1

    def chunk(c, it, insub):
        return jnp.where(insub[it] > 0, c, last)

    grid_spec = pltpu.PrefetchScalarGridSpec(
        num_scalar_prefetch=3,
        grid=(n_items, n_chunks),
        in_specs=[
            pl.BlockSpec(memory_space=pl.ANY),
            pl.BlockSpec((1, d, tc), lambda it, c, ie, ist, ins: (ie[it], 0, chunk(c, it, ins))),
            pl.BlockSpec((1, 1, tc), lambda it, c, ie, ist, ins: (ie[it], 0, chunk(c, it, ins))),
            pl.BlockSpec((1, tc // 2, d), lambda it, c, ie, ist, ins: (ie[it], chunk(c, it, ins), 0)),
            pl.BlockSpec((1, 1, d), lambda it, c, ie, ist, ins: (ie[it], 0, 0)),
        ],
        out_specs=pl.BlockSpec(memory_space=pl.ANY),
        scratch_shapes=[
            pltpu.VMEM((MOE_ROWS_PER_ITEM, d), BF16),
            pltpu.VMEM((MOE_ROWS_PER_ITEM, d), F32),
            pltpu.VMEM((MOE_ROW_TILE, 1, d), F32),
            pltpu.VMEM((MOE_ROW_TILE, 1, d), F32),
            pltpu.VMEM((MOE_ROW_TILE, 1, dw), jnp.int32),
            pltpu.VMEM((MOE_ROW_TILE, 1, dw), jnp.int32),
            pltpu.VMEM((MOE_ROW_TILE, dw), jnp.int32),
            pltpu.VMEM((d, tc), BF16),
            pltpu.VMEM((d // LANES, tc // 2, LANES), F32),
            pltpu.VMEM((tc // 2, d), BF16),
            pltpu.SemaphoreType.DMA((2,)),
        ],
    )
    return pl.pallas_call(
        _moe_body,
        grid_spec=grid_spec,
        out_shape=jax.ShapeDtypeStruct((n_rows, 1, d), F32),
        compiler_params=_cparams("arbitrary", "arbitrary"),
        name="moe_mlp",
    )(item_e, item_start, item_nsub, xs, w1, b1, w2, b2)


def _combine_body(idx_ref, rank_ref, pend_ref, padded_ref, x_ref, gate_ref, g_ref, ys_ref, o_ref, b0, b1, b2, b3, flat_ref, sem, *, final_norm):
    i = pl.program_id(0)
    bm = x_ref.shape[0]
    bufs = (b0, b1, b2, b3)

    def row_copy(r, kk):
        d = _sorted_row(idx_ref, rank_ref, pend_ref, padded_ref, (i * bm + r) * TOP_K + kk)
        return pltpu.make_async_copy(ys_ref.at[d], bufs[kk].at[r], sem)

    def issue(r):
        for kk in range(TOP_K):
            row_copy(r, kk).start(priority=kk % 2)

    def drain(r):
        for kk in range(TOP_K):
            row_copy(r, kk).wait()

    _for_rows(bm, issue)
    _for_rows(bm, drain)
    acc = x_ref[...]
    for kk in range(TOP_K):
        flat_ref[...] = bufs[kk][...].reshape(flat_ref.shape)
        acc = acc + gate_ref[:, kk:kk + 1] * flat_ref[...]
    o_ref[...] = _rmsnorm(acc, g_ref[...]) if final_norm else acc


def _combine(idx_flat, rank_flat, pend, padded, x2d, gate4, g, ys, bm, final_norm):
    m, d = x2d.shape
    bm = min(bm, m)
    grid_spec = pltpu.PrefetchScalarGridSpec(
        num_scalar_prefetch=4,
        grid=(m // bm,),
        in_specs=[
            pl.BlockSpec((bm, d), lambda i, *_: (i, 0)),
            pl.BlockSpec((bm, LANES), lambda i, *_: (i, 0)),
            pl.BlockSpec((1, d), lambda i, *_: (0, 0)),
            pl.BlockSpec(memory_space=pl.ANY),
        ],
        out_specs=pl.BlockSpec((bm, d), lambda i, *_: (i, 0)),
        scratch_shapes=[pltpu.VMEM((bm, 1, d), F32)] * TOP_K + [pltpu.VMEM((bm, d), F32), pltpu.SemaphoreType.DMA],
    )
    return pl.pallas_call(
        functools.partial(_combine_body, final_norm=final_norm),
        grid_spec=grid_spec,
        out_shape=jax.ShapeDtypeStruct((m, d), F32),
        compiler_params=_cparams("arbitrary"),
        name="moe_combine",
    )(idx_flat, rank_flat, pend, padded, x2d, gate4, g.reshape(1, d), ys)


def _routing_tables(counts, n_assign, n_exp):
    tm, rpi = MOE_ROW_TILE, MOE_ROWS_PER_ITEM
    padded = (counts + tm - 1) // tm * tm
    pend = jnp.cumsum(padded)
    pstart = pend - padded

    n_rows = n_assign + n_exp * tm
    n_items = n_rows // rpi + n_exp
    per_e = (padded + rpi - 1) // rpi
    icum = jnp.cumsum(per_e)
    ioff = icum - per_e
    total = icum[-1]
    ii = jnp.arange(n_items, dtype=jnp.int32)
    ic = jnp.minimum(ii, total - 1)
    e_of = jnp.sum((icum[None, :] <= ic[:, None]).astype(jnp.int32), axis=1)
    e_of = jnp.minimum(e_of, n_exp - 1)
    local = ii - ioff[e_of]
    valid = ii < total
    start = jnp.where(valid, pstart[e_of] + local * rpi, 0).astype(jnp.int32)
    nsub = jnp.where(valid, jnp.clip((padded[e_of] - local * rpi) // tm, 0, rpi // tm), 0).astype(jnp.int32)
    return pend.astype(jnp.int32), padded.astype(jnp.int32), e_of, start, nsub, n_rows


def _alibi_slopes(n):
    start = 2.0 ** (-8.0 / n)
    return np.array([start ** (i + 1) for i in range(n)], dtype=np.float32)


def kernel(x, mem, g_mix, w_in, b_forget, swa_sinks, w_out, g_cross, g_mem, w_cq, w_ckv, w_co, g_moe,
           w_router, b_router, w_mlp1, b_mlp1, w_mlp2, b_mlp2, g_final):
    batch, seq, d = x.shape
    n_mem = mem.shape[1]
    depth = w_in.shape[0]
    n_fox = d // (2 * HEAD_DIM)
    n_swa = d // (2 * HEAD_DIM)
    n_kv = n_swa // SWA_GROUP
    fox_w = n_fox * HEAD_DIM
    swa_q_w = n_swa * HEAD_DIM
    swa_kv_w = n_kv * HEAD_DIM
    o2 = 2 * fox_w
    o3 = 3 * fox_w
    o4 = o3 + n_fox
    o5 = o4 + swa_q_w
    o6 = o5 + swa_kv_w
    n_exp = w_router.shape[2]
    t = batch * seq
    slopes = jnp.asarray(_alibi_slopes(n_swa))

    def dup_heads(wk):
        wk = wk.reshape(d, n_kv, 1, HEAD_DIM)
        return jnp.broadcast_to(wk, (d, n_kv, 2, HEAD_DIM)).reshape(d, n_kv * LANES)

    x2d = x.reshape(t, d)
    for l in range(depth):
        wl = w_in[l]
        w_qk = wl[:, :o2].astype(BF16)
        w_vt = wl[:, o2:o3].T.astype(BF16)
        w_ff = jnp.zeros((d, LANES), BF16).at[:, :n_fox].set(wl[:, o3:o4].astype(BF16))
        b_ff = jnp.zeros((1, LANES), F32).at[0, :n_fox].set(b_forget[l])
        w_swa = jnp.concatenate(
            [wl[:, o4:o5], dup_heads(wl[:, o5:o6]), dup_heads(wl[:, o6:])], axis=1).astype(BF16)

        qk = _norm_matmul(x2d, g_mix[l], w_qk, BF16, 1024, fox_w, "fox_qk_proj",
                          lead_tiles=1, lead_scale=HEAD_DIM ** -0.5 * LOG2E)
        vt = _norm_matmul(x2d, g_mix[l], w_vt, BF16, 1024, 1024, "fox_vt_proj", transposed=True)
        swa_proj = _norm_matmul(x2d, g_mix[l], w_swa, BF16, 1024, 1024, "swa_proj")
        c = _fox_gate(x2d.reshape(batch, seq, d), g_mix[l], w_ff, b_ff, 512)
        fox = _fox_attention(qk, vt, c, batch, seq, fox_w // LANES, FOX_TILE)
        swa = _swa_attention(swa_proj, slopes, swa_sinks[l], batch, seq, swa_q_w // LANES, n_kv)
        x2d = _out_proj(fox, swa, w_out[l].astype(BF16), x2d, 1024, 1024)

        kv = _norm_matmul(mem.reshape(batch * n_mem, d), g_mem[l], w_ckv[l].astype(BF16), BF16, 1024, 1024,
                          "mem_kv_proj")
        x2d = _cross_block(x2d, g_cross[l], w_cq[l].astype(BF16), kv, w_co[l].astype(BF16), seq, n_mem, 512)

        w_r = jnp.zeros((d, LANES), BF16).at[:, :n_exp].set(w_router[l].astype(BF16))
        b_r = jnp.full((1, LANES), NEG_BIG, F32).at[0, :n_exp].set(b_router[l])
        idx4, gate4, rank4, cnt, h3d = _router(x2d, g_moe[l], w_r, b_r, 512)
        pend, padded, item_e, item_start, item_nsub, n_rows = _routing_tables(cnt[0, :n_exp], t * TOP_K, n_exp)
        idx_flat = idx4[:, :TOP_K].reshape(-1)
        rank_flat = rank4[:, :TOP_K].reshape(-1)
        xs = _dispatch(idx_flat, rank_flat, pend, padded, h3d, n_rows, 256)
        two_i = w_mlp1.shape[3]
        ys = _moe_mlp(item_e, item_start, item_nsub, xs, w_mlp1[l], b_mlp1[l].reshape(n_exp, 1, two_i),
                      w_mlp2[l], b_mlp2[l].reshape(n_exp, 1, d))
        x2d = _combine(idx_flat, rank_flat, pend, padded, x2d, gate4, g_final, ys, 256,
                       final_norm=(l == depth - 1))
    return x2d.reshape(batch, seq, d)
```

```python
import functools

import numpy as np
import jax
import jax.numpy as jnp
from jax import lax
from jax.experimental import pallas as pl
from jax.experimental.pallas import tpu as pltpu

F32 = jnp.float32
BF16 = jnp.bfloat16

RMS_EPS = 1e-5
HEAD_DIM = 64
WINDOW = 128
SWA_GROUP = 4
N_CROSS_HEADS = 4
TOP_K = 4
SWIGLU_ALPHA = 1.702
SWIGLU_LIMIT = 7.0

LANES = 128
MXU_DIM = 256
VMEM_LIMIT_BYTES = 56 * 2**20
NEG_BIG = -1e30
LOG2E = 1.4426950408889634

FOX_TILE = 1024
SWA_WINDOWS_PER_STEP = 8
ROW_COPY_UNROLL = 8
MOE_ROW_TILE = 256
MOE_ROWS_PER_ITEM = 2304
MOE_COL_CHUNK = 512
MOE_GROUP = 4


def _cparams(*sem):
    return pltpu.CompilerParams(dimension_semantics=sem, vmem_limit_bytes=VMEM_LIMIT_BYTES)


def _rmsnorm(x, g):
    ms = jnp.mean(x * x, axis=-1, keepdims=True)
    return x * lax.rsqrt(ms + RMS_EPS) * g


def _for_rows(n, fn):
    def trip(o, carry):
        for u in range(ROW_COPY_UNROLL):
            fn(o * ROW_COPY_UNROLL + u)
        return carry
    lax.fori_loop(0, n // ROW_COPY_UNROLL, trip, 0)


def _dot_nt(a, b):
    return lax.dot_general(a, b, (((1,), (1,)), ((), ())), preferred_element_type=F32)


def _norm_matmul_body(x_ref, g_ref, w_ref, o_ref, h_ref, *, transposed, lead_tiles, lead_scale):
    @pl.when(pl.program_id(1) == 0)
    def _():
        h_ref[...] = _rmsnorm(x_ref[...], g_ref[...]).astype(h_ref.dtype)

    if transposed:
        o_ref[...] = _dot_nt(w_ref[...], h_ref[...]).astype(o_ref.dtype)
    else:
        acc = jnp.dot(h_ref[...], w_ref[...], preferred_element_type=F32)
        if lead_tiles:
            acc = acc * jnp.where(pl.program_id(1) < lead_tiles, lead_scale, 1.0)
        o_ref[...] = acc.astype(o_ref.dtype)


def _norm_matmul(x2d, g, w, out_dtype, bm, bn, name, transposed=False, lead_tiles=0, lead_scale=1.0):
    m, d = x2d.shape
    n = w.shape[0] if transposed else w.shape[1]
    bm, bn = min(bm, m), min(bn, n)
    if transposed:
        w_spec = pl.BlockSpec((bn, d), lambda i, j: (j, 0))
        o_spec = pl.BlockSpec((bn, bm), lambda i, j: (j, i))
        o_shape = (n, m)
    else:
        w_spec = pl.BlockSpec((d, bn), lambda i, j: (0, j))
        o_spec = pl.BlockSpec((bm, bn), lambda i, j: (i, j))
        o_shape = (m, n)
    return pl.pallas_call(
        functools.partial(_norm_matmul_body, transposed=transposed, lead_tiles=lead_tiles, lead_scale=lead_scale),
        grid=(m // bm, n // bn),
        in_specs=[pl.BlockSpec((bm, d), lambda i, j: (i, 0)), pl.BlockSpec((1, d), lambda i, j: (0, 0)), w_spec],
        out_specs=o_spec,
        out_shape=jax.ShapeDtypeStruct(o_shape, out_dtype),
        scratch_shapes=[pltpu.VMEM((bm, d), BF16)],
        compiler_params=_cparams("parallel", "arbitrary"),
        name=name,
    )(x2d, g.reshape(1, d), w)


def _fox_gate_body(x_ref, g_ref, w_ref, b_ref, c_ref, carry_ref):
    @pl.when(pl.program_id(1) == 0)
    def _():
        carry_ref[...] = jnp.zeros_like(carry_ref)

    bs = x_ref.shape[1]
    h = _rmsnorm(x_ref[0], g_ref[...]).astype(BF16)
    z = jnp.dot(h, w_ref[...], preferred_element_type=F32) + b_ref[...]
    logf = jnp.minimum(z, 0.0) - jnp.log1p(jnp.exp(-jnp.abs(z)))
    r = lax.broadcasted_iota(jnp.int32, (bs, bs), 0)
    c = lax.broadcasted_iota(jnp.int32, (bs, bs), 1)
    tri = (c <= r).astype(F32)
    cs = jnp.dot(tri, logf, preferred_element_type=F32, precision=lax.Precision.HIGHEST)
    cs = cs + carry_ref[0:1, :]
    c_ref[0] = cs * LOG2E
    carry_ref[...] = jnp.broadcast_to(cs[bs - 1:bs, :], carry_ref.shape)


def _fox_gate(x, g, w_pad, b_pad, bs):
    b, s, d = x.shape
    bs = min(bs, s)
    return pl.pallas_call(
        _fox_gate_body,
        grid=(b, s // bs),
        in_specs=[
            pl.BlockSpec((1, bs, d), lambda i, j: (i, j, 0)),
            pl.BlockSpec((1, d), lambda i, j: (0, 0)),
            pl.BlockSpec((d, LANES), lambda i, j: (0, 0)),
            pl.BlockSpec((1, LANES), lambda i, j: (0, 0)),
        ],
        out_specs=pl.BlockSpec((1, bs, LANES), lambda i, j: (i, j, 0)),
        out_shape=jax.ShapeDtypeStruct((b, s, LANES), F32),
        scratch_shapes=[pltpu.VMEM((8, LANES), F32)],
        compiler_params=_cparams("parallel", "arbitrary"),
        name="fox_gate",
    )(x, g.reshape(1, d), w_pad, b_pad)


def _fox_attn_body(itab_ref, jtab_ref, q_ref, k_ref, vt_ref, c_ref, o_ref, m_ref, l_ref, acc_ref):
    p = pl.program_id(1)
    st = pl.program_id(2)
    i = itab_ref[st]
    j = jtab_ref[st]
    tq = q_ref.shape[0]
    tk = k_ref.shape[0]
    lane = lax.broadcasted_iota(jnp.int32, (1, LANES), 1)

    @pl.when(j == 0)
    def _():
        m_ref[...] = jnp.full_like(m_ref, NEG_BIG)
        l_ref[...] = jnp.zeros_like(l_ref)
        acc_ref[...] = jnp.zeros_like(acc_ref)

    def step(masked):
        q = q_ref[...]
        k = k_ref[...]
        vt = vt_ref[...]
        cblk = c_ref[0]
        if masked:
            key = lax.broadcasted_iota(jnp.int32, (tk, tq), 0)
            qry = lax.broadcasted_iota(jnp.int32, (tk, tq), 1)
            causal = key <= qry
        scores = []
        for hh in range(2):
            qh = jnp.where(lane // HEAD_DIM == hh, q, jnp.zeros_like(q))
            scores.append(_dot_nt(k, qh))
        n_split = 2 if tq % (2 * MXU_DIM) == 0 else 1
        wq = tq // n_split
        for hh in range(2):
            c_col = jnp.sum(jnp.where(lane == 2 * p + hh, cblk, 0.0), axis=1, keepdims=True)
            for g in range(n_split):
                cols = slice(g * wq, (g + 1) * wq)
                s = scores[hh][:, cols] - c_col
                if masked:
                    s = jnp.where(causal[:, cols], s, NEG_BIG)
                m_prev = m_ref[hh, :, cols]
                m_new = jnp.maximum(m_prev, jnp.max(s, axis=0, keepdims=True))
                alpha = jnp.exp2(m_prev - m_new)
                pt = jnp.exp2(s - m_new)
                l_ref[hh, :, cols] = alpha * l_ref[hh, :, cols] + jnp.sum(pt, axis=0, keepdims=True)
                acc_ref[hh, :, cols] = alpha * acc_ref[hh, :, cols] + jnp.dot(
                    vt, pt.astype(BF16), preferred_element_type=F32)
                m_ref[hh, :, cols] = m_new

    @pl.when(j < i)
    def _():
        step(False)

    @pl.when(j == i)
    def _():
        step(True)
        row = lax.broadcasted_iota(jnp.int32, (LANES, 1), 0)
        ot = jnp.where(row < HEAD_DIM, acc_ref[0] / l_ref[0], acc_ref[1] / l_ref[1])
        o_ref[...] = ot.T.astype(o_ref.dtype)


def _fox_attention(qk, vt, c, batch, seq, n_pairs, tile):
    tile = min(tile, seq)
    nq = seq // tile
    pairs = [(i, j) for i in range(nq) for j in range(i + 1)]
    itab = jnp.asarray([ij[0] for ij in pairs], jnp.int32)
    jtab = jnp.asarray([ij[1] for ij in pairs], jnp.int32)
    grid_spec = pltpu.PrefetchScalarGridSpec(
        num_scalar_prefetch=2,
        grid=(batch, n_pairs, len(pairs)),
        in_specs=[
            pl.BlockSpec((tile, LANES), lambda b, p, s, it, jt: (b * nq + it[s], p)),
            pl.BlockSpec((tile, LANES), lambda b, p, s, it, jt: (b * nq + jt[s], n_pairs + p)),
            pl.BlockSpec((LANES, tile), lambda b, p, s, it, jt: (p, b * nq + jt[s])),
            pl.BlockSpec((1, tile, LANES), lambda b, p, s, it, jt: (b, jt[s], 0)),
        ],
        out_specs=pl.BlockSpec((tile, LANES), lambda b, p, s, it, jt: (b * nq + it[s], p)),
        scratch_shapes=[
            pltpu.VMEM((2, 1, tile), F32),
            pltpu.VMEM((2, 1, tile), F32),
            pltpu.VMEM((2, LANES, tile), F32),
        ],
    )
    return pl.pallas_call(
        _fox_attn_body,
        grid_spec=grid_spec,
        out_shape=jax.ShapeDtypeStruct((batch * seq, n_pairs * LANES), BF16),
        compiler_params=_cparams("parallel", "parallel", "arbitrary"),
        name="fox_attention",
    )(itab, jtab, qk, qk, vt, c)


def _swa_body(slopes_ref, sinks_ref, q_ref, kp_ref, kc_ref, vp_ref, vc_ref, o_ref):
    pr = pl.program_id(1)
    n = pl.program_id(2)
    w = WINDOW
    n_win = q_ref.shape[0] // w
    lane = lax.broadcasted_iota(jnp.int32, (1, LANES), 1)
    row = lax.broadcasted_iota(jnp.int32, (2 * w, 1), 0)
    head1 = row >= w
    slope = jnp.where(head1, slopes_ref[2 * pr + 1], slopes_ref[2 * pr])
    sink = jnp.where(head1, sinks_ref[2 * pr + 1], sinks_ref[2 * pr])
    qi = lax.broadcasted_iota(jnp.int32, (2 * w, 2 * w), 0) % w
    kj = lax.broadcasted_iota(jnp.int32, (2 * w, 2 * w), 1)
    rel = qi - kj + w
    in_window = (rel >= 0) & (rel < w)
    bias = slope * rel.astype(F32)
    for wi in range(n_win):
        qw = q_ref[wi * w:(wi + 1) * w, :]
        if wi == 0:
            k = jnp.concatenate([kp_ref[...], kc_ref[0:w, :]], axis=0)
            v = jnp.concatenate([vp_ref[...], vc_ref[0:w, :]], axis=0)
            valid = in_window & (kj >= jnp.where(n > 0, 0, w))
        else:
            k = kc_ref[(wi - 1) * w:(wi + 1) * w, :]
            v = vc_ref[(wi - 1) * w:(wi + 1) * w, :]
            valid = in_window
        q2 = jnp.concatenate([jnp.where(lane < HEAD_DIM, qw, jnp.zeros_like(qw)),
                              jnp.where(lane >= HEAD_DIM, qw, jnp.zeros_like(qw))], axis=0)
        s = _dot_nt(q2, k) * (HEAD_DIM ** -0.5) - bias
        s = jnp.where(valid, s, NEG_BIG)
        m = jnp.maximum(jnp.max(s, axis=-1, keepdims=True), sink)
        p = jnp.exp(s - m)
        denom = jnp.sum(p, axis=-1, keepdims=True) + jnp.exp(sink - m)
        o = jnp.dot((p / denom).astype(BF16), v, preferred_element_type=F32)
        o_ref[wi * w:(wi + 1) * w, :] = jnp.where(lane < HEAD_DIM, o[:w], o[w:]).astype(o_ref.dtype)


def _swa_attention(proj, col0, slopes, sinks, batch, seq, n_pairs, n_kv):
    n_win = min(SWA_WINDOWS_PER_STEP, seq // WINDOW)
    blk = n_win * WINDOW
    nb = seq // blk
    wins = seq // WINDOW

    def q_map(b, p, n):
        return (b * nb + n, p)

    def q_in_map(b, p, n):
        return (b * nb + n, col0 + p)

    def cur_map(base):
        return lambda b, p, n: (b * nb + n, col0 + base + p // 2)

    def prev_map(base):
        return lambda b, p, n: (b * wins + jnp.maximum(n * n_win - 1, 0), col0 + base + p // 2)

    smem = pl.BlockSpec(memory_space=pltpu.SMEM)
    return pl.pallas_call(
        _swa_body,
        grid=(batch, n_pairs, nb),
        in_specs=[
            smem, smem,
            pl.BlockSpec((blk, LANES), q_in_map),
            pl.BlockSpec((WINDOW, LANES), prev_map(n_pairs)),
            pl.BlockSpec((blk, LANES), cur_map(n_pairs)),
            pl.BlockSpec((WINDOW, LANES), prev_map(n_pairs + n_kv)),
            pl.BlockSpec((blk, LANES), cur_map(n_pairs + n_kv)),
        ],
        out_specs=pl.BlockSpec((blk, LANES), q_map),
        out_shape=jax.ShapeDtypeStruct((batch * seq, n_pairs * LANES), BF16),
        compiler_params=_cparams("parallel", "parallel", "parallel"),
        name="swa_attention",
    )(slopes, sinks, proj, proj, proj, proj, proj)


def _out_proj_body(a1_ref, a2_ref, w_ref, x_ref, o_ref):
    a = jnp.concatenate([a1_ref[...], a2_ref[...]], axis=1)
    o_ref[...] = x_ref[...] + jnp.dot(a, w_ref[...], preferred_element_type=F32)


def _out_proj(a1, a2, w, x2d, bm, bn):
    m, d = x2d.shape
    k1, k2 = a1.shape[1], a2.shape[1]
    bm, bn = min(bm, m), min(bn, d)
    return pl.pallas_call(
        _out_proj_body,
        grid=(m // bm, d // bn),
        in_specs=[
            pl.BlockSpec((bm, k1), lambda i, j: (i, 0)),
            pl.BlockSpec((bm, k2), lambda i, j: (i, 0)),
            pl.BlockSpec((k1 + k2, bn), lambda i, j: (0, j)),
            pl.BlockSpec((bm, bn), lambda i, j: (i, j)),
        ],
        out_specs=pl.BlockSpec((bm, bn), lambda i, j: (i, j)),
        out_shape=jax.ShapeDtypeStruct((m, d), F32),
        compiler_params=_cparams("parallel", "parallel"),
        name="out_proj",
    )(a1, a2, w, x2d)


def _cross_body(x_ref, g_ref, wq_ref, kv_ref, wo_ref, o_ref):
    x = x_ref[...]
    h = _rmsnorm(x, g_ref[...]).astype(BF16)
    q = jnp.dot(h, wq_ref[...], preferred_element_type=F32).astype(BF16)
    cw = wq_ref.shape[1]
    hd = cw // N_CROSS_HEADS
    outs = []
    for hh in range(N_CROSS_HEADS):
        qh = q[:, hh * hd:(hh + 1) * hd]
        kh = kv_ref[:, hh * hd:(hh + 1) * hd]
        vh = kv_ref[:, cw + hh * hd:cw + (hh + 1) * hd]
        s = _dot_nt(qh, kh) * (hd ** -0.5)
        e = jnp.exp(s - jnp.max(s, axis=-1, keepdims=True))
        p = (e / jnp.sum(e, axis=-1, keepdims=True)).astype(BF16)
        outs.append(jnp.dot(p, vh, preferred_element_type=F32))
    o = jnp.concatenate(outs, axis=1).astype(BF16)
    o_ref[...] = x + jnp.dot(o, wo_ref[...], preferred_element_type=F32)


def _cross_block(x2d, g, wq, kv, wo, seq, n_mem, bm):
    m, d = x2d.shape
    bm = min(bm, seq)
    per_b = seq // bm
    cw = wq.shape[1]
    return pl.pallas_call(
        _cross_body,
        grid=(m // bm,),
        in_specs=[
            pl.BlockSpec((bm, d), lambda i: (i, 0)),
            pl.BlockSpec((1, d), lambda i: (0, 0)),
            pl.BlockSpec((d, cw), lambda i: (0, 0)),
            pl.BlockSpec((n_mem, 2 * cw), lambda i: (i // per_b, 0)),
            pl.BlockSpec((cw, d), lambda i: (0, 0)),
        ],
        out_specs=pl.BlockSpec((bm, d), lambda i: (i, 0)),
        out_shape=jax.ShapeDtypeStruct((m, d), F32),
        compiler_params=_cparams("parallel"),
        name="cross_block",
    )(x2d, g.reshape(1, d), wq, kv, wo)


def _router_body(x_ref, g_ref, w_ref, b_ref, idx_ref, gate_ref, rank_ref, cnt_ref, h3_ref, hn_ref, carry_ref):
    @pl.when(pl.program_id(0) == 0)
    def _():
        carry_ref[...] = jnp.zeros_like(carry_ref)

    bm = x_ref.shape[0]
    h = _rmsnorm(x_ref[...], g_ref[...]).astype(BF16)
    bits = lax.bitcast_convert_type(h.astype(F32), jnp.int32)
    hw = bits.shape[1] // 2
    hn_ref[...] = lax.shift_right_logical(bits[:, :hw], 16) | (bits[:, hw:] & jnp.int32(-65536))
    h3_ref[...] = hn_ref[...].reshape(h3_ref.shape)
    logits = jnp.dot(h, w_ref[...], preferred_element_type=F32) + b_ref[...]
    lane = lax.broadcasted_iota(jnp.int32, (bm, LANES), 1)
    lane_f = lane.astype(F32)
    work = logits
    self32 = jnp.zeros((bm, LANES), F32)
    vals, picks, hots = [], [], []
    for _ in range(TOP_K):
        mk = jnp.max(work, axis=-1, keepdims=True)
        pick = jnp.min(jnp.where(work == mk, lane_f, float(LANES)), axis=-1, keepdims=True)
        hot = lane_f == pick
        vals.append(mk)
        picks.append(pick.astype(jnp.int32))
        hots.append(hot)
        self32 = jnp.where(hot, 1.0, self32)
        work = jnp.where(hot, -jnp.inf, work)
    es = [jnp.exp(v - vals[0]) for v in vals]
    denom = es[0] + es[1] + es[2] + es[3]

    r = lax.broadcasted_iota(jnp.int32, (bm, bm), 0)
    c = lax.broadcasted_iota(jnp.int32, (bm, bm), 1)
    tri = (c < r).astype(BF16)
    rank = jnp.dot(tri, self32.astype(BF16), preferred_element_type=F32) + carry_ref[0:1, :]
    carry_new = carry_ref[0:1, :] + jnp.sum(self32, axis=0, keepdims=True)
    carry_ref[...] = jnp.broadcast_to(carry_new, carry_ref.shape)

    idx_o = jnp.zeros((bm, LANES), jnp.int32)
    gate_o = jnp.zeros((bm, LANES), F32)
    rank_o = jnp.zeros((bm, LANES), jnp.int32)
    for kk in range(TOP_K):
        rk = jnp.sum(jnp.where(hots[kk], rank, 0.0), axis=-1, keepdims=True).astype(jnp.int32)
        idx_o = jnp.where(lane == kk, picks[kk], idx_o)
        gate_o = jnp.where(lane == kk, es[kk] / denom, gate_o)
        rank_o = jnp.where(lane == kk, rk, rank_o)
    idx_ref[...] = idx_o
    gate_ref[...] = gate_o
    rank_ref[...] = rank_o
    cnt_ref[...] = jnp.broadcast_to(carry_new, cnt_ref.shape).astype(jnp.int32)


def _router(x2d, g, w_pad, b_pad, bm):
    m, d = x2d.shape
    bm = min(bm, m)
    tok = pl.BlockSpec((bm, LANES), lambda i: (i, 0))
    return pl.pallas_call(
        _router_body,
        grid=(m // bm,),
        in_specs=[
            pl.BlockSpec((bm, d), lambda i: (i, 0)),
            pl.BlockSpec((1, d), lambda i: (0, 0)),
            pl.BlockSpec((d, LANES), lambda i: (0, 0)),
            pl.BlockSpec((1, LANES), lambda i: (0, 0)),
        ],
        out_specs=[tok, tok, tok, pl.BlockSpec((8, LANES), lambda i: (0, 0)),
                   pl.BlockSpec((bm, 1, d // 2), lambda i: (i, 0, 0))],
        out_shape=[
            jax.ShapeDtypeStruct((m, LANES), jnp.int32),
            jax.ShapeDtypeStruct((m, LANES), F32),
            jax.ShapeDtypeStruct((m, LANES), jnp.int32),
            jax.ShapeDtypeStruct((8, LANES), jnp.int32),
            jax.ShapeDtypeStruct((m, 1, d // 2), jnp.int32),
        ],
        scratch_shapes=[pltpu.VMEM((bm, d // 2), jnp.int32), pltpu.VMEM((8, LANES), F32)],
        compiler_params=_cparams("arbitrary"),
        name="router",
    )(x2d, g.reshape(1, d), w_pad, b_pad)


def _sorted_row(idx_ref, rank_ref, pend_ref, padded_ref, a):
    e = idx_ref[a]
    return pend_ref[e] - padded_ref[e] + rank_ref[a]


def _dispatch_body(idx_ref, rank_ref, pend_ref, padded_ref, x_ref, xs_ref, z_ref, sem, zsem):
    i = pl.program_id(0)
    bm = x_ref.shape[0]
    n_exp = pend_ref.shape[0]
    tm = z_ref.shape[0]

    def zero_copy(e):
        return pltpu.make_async_copy(z_ref, xs_ref.at[pl.ds(pend_ref[e] - tm, tm)], zsem)

    @pl.when(i == 0)
    def _():
        z_ref[...] = jnp.zeros_like(z_ref)
        for e in range(n_exp):
            @pl.when(padded_ref[e] > 0)
            def _():
                zero_copy(e).start()
        for e in range(n_exp):
            @pl.when(padded_ref[e] > 0)
            def _():
                zero_copy(e).wait()

    def row_copy(r, kk):
        d = _sorted_row(idx_ref, rank_ref, pend_ref, padded_ref, (i * bm + r) * TOP_K + kk)
        return pltpu.make_async_copy(x_ref.at[r], xs_ref.at[d], sem)

    def issue(r):
        for kk in range(TOP_K):
            row_copy(r, kk).start(priority=kk % 2)

    def drain(r):
        for kk in range(TOP_K):
            row_copy(r, kk).wait()

    _for_rows(bm, issue)
    _for_rows(bm, drain)


def _dispatch(idx_flat, rank_flat, pend, padded, x3d, n_rows, bm):
    m, _, d = x3d.shape
    bm = min(bm, m)
    grid_spec = pltpu.PrefetchScalarGridSpec(
        num_scalar_prefetch=4,
        grid=(m // bm,),
        in_specs=[pl.BlockSpec((bm, 1, d), lambda i, *_: (i, 0, 0))],
        out_specs=pl.BlockSpec(memory_space=pl.ANY),
        scratch_shapes=[
            pltpu.VMEM((MOE_ROW_TILE, 1, d), x3d.dtype),
            pltpu.SemaphoreType.DMA,
            pltpu.SemaphoreType.DMA,
        ],
    )
    return pl.pallas_call(
        _dispatch_body,
        grid_spec=grid_spec,
        out_shape=jax.ShapeDtypeStruct((n_rows, 1, d), x3d.dtype),
        compiler_params=_cparams("arbitrary"),
        name="moe_dispatch",
    )(idx_flat, rank_flat, pend, padded, x3d)


def _moe_body(ie_ref, istart_ref, insub_ref, xs_ref, w1_ref, b1_ref, w2_ref, b2_ref,
              ys_ref, xbf_ref, yacc_ref, st0_ref, st1_ref, ld0_ref, ld1_ref, land_ref, w1b_ref, w2m_ref, w2b_ref,
              sems):
    it = pl.program_id(0)
    c = pl.program_id(1)
    n_chunks = pl.num_programs(1)
    nsub = insub_ref[it]
    start = istart_ref[it]
    tm = st0_ref.shape[0]
    tc = w1_ref.shape[2]
    half = tc // 2
    quarter = tc // 4
    even_lane = lax.broadcasted_iota(jnp.int32, (1, half), 1) % 2 == 0
    stage = (st0_ref, st1_ref)
    load_stage = (ld0_ref, ld1_ref)
    n_pairs = (nsub + 1) // 2

    def rows(s):
        return pl.ds(pl.multiple_of(s * tm, tm), tm)

    def load_copy(s, k):
        return pltpu.make_async_copy(xs_ref.at[pl.ds(start + s * tm, tm)], load_stage[k], sems.at[k])

    def store_copy(s, k):
        return pltpu.make_async_copy(stage[k], ys_ref.at[pl.ds(start + s * tm, tm)], sems.at[k])

    def consume(s, k):
        land_ref[...] = load_stage[k][...].reshape(land_ref.shape)
        word = land_ref[...]
        hw = word.shape[1]
        lo = lax.bitcast_convert_type(lax.shift_left(word, 16), F32)
        hi = lax.bitcast_convert_type(word & jnp.int32(-65536), F32)
        xbf_ref[rows(s), :hw] = lo.astype(BF16)
        xbf_ref[rows(s), hw:] = hi.astype(BF16)
        yacc_ref[rows(s), :] = jnp.zeros((tm, yacc_ref.shape[1]), F32)

    def hidden(rs):
        return jnp.dot(xbf_ref[rs, :], w1b_ref[...], preferred_element_type=F32) + b1_ref[0]

    def accumulated(rs, u):
        glu = jnp.minimum(u, SWIGLU_LIMIT)
        act = glu * (1.0 / (1.0 + jnp.exp(-SWIGLU_ALPHA * glu)))
        lin1 = jnp.clip(u, -SWIGLU_LIMIT, SWIGLU_LIMIT) + 1.0
        a_even = act * pltpu.roll(lin1, tc - 1, 1)
        merged = jnp.where(even_lane, a_even[:, :half], pltpu.roll(a_even[:, half:], 1, 1))
        return yacc_ref[rs, :] + jnp.dot(merged.astype(BF16), w2b_ref[...], preferred_element_type=F32)

    def emit(s, k, wait_for):
        if wait_for is not None:
            store_copy(wait_for, k).wait()
        stage[k][...] = yacc_ref[rows(s), :].reshape(stage[k].shape)
        store_copy(s, k).start()

    @pl.when(nsub > 0)
    def _():
        @pl.when(c == 0)
        def _():
            load_copy(0, 0).start()

            def load_pair(sp, carry):
                s0 = 2 * sp
                s1 = s0 + 1

                @pl.when(s1 < nsub)
                def _():
                    load_copy(s1, 1).start()
                load_copy(s0, 0).wait()
                consume(s0, 0)

                @pl.when(s1 < nsub)
                def _():
                    @pl.when(s1 + 1 < nsub)
                    def _():
                        load_copy(s1 + 1, 0).start()
                    load_copy(s1, 1).wait()
                    consume(s1, 1)
                return carry
            lax.fori_loop(0, n_pairs, load_pair, 0)

        w1b_ref[...] = w1_ref[0].astype(BF16)
        for j in range(w2m_ref.shape[0]):
            cols = slice(j * LANES, (j + 1) * LANES)
            w2m_ref[j, pl.ds(0, quarter, stride=2), :] = w2_ref[0, 0:quarter, cols]
            w2m_ref[j, pl.ds(1, quarter, stride=2), :] = w2_ref[0, quarter:2 * quarter, cols]
            w2b_ref[:, cols] = w2m_ref[j].astype(BF16)

        def group_update(first, n, bias):
            rs = [rows(first + k) for k in range(n)]
            us = [hidden(r) for r in rs]
            for k in range(n):
                yacc_ref[rs[k], :] = accumulated(rs[k], us[k]) + bias

        def single_update(s, bias):
            group_update(s, 1, bias)

        @pl.when(c < n_chunks - 1)
        def _():
            n_groups = nsub // MOE_GROUP

            def group(gi, carry):
                group_update(gi * MOE_GROUP, MOE_GROUP, 0.0)
                return carry
            lax.fori_loop(0, n_groups, group, 0)

            def single(s, carry):
                single_update(s, 0.0)
                return carry
            lax.fori_loop(n_groups * MOE_GROUP, nsub, single, 0)

        @pl.when(c == n_chunks - 1)
        def _():
            n_groups = nsub // MOE_GROUP

            @pl.when(n_groups > 0)
            def _():
                group_update(0, MOE_GROUP, b2_ref[0])
                for k in range(MOE_GROUP):
                    emit(k, k % 2, None if k < 2 else k - 2)

                def group(gi, carry):
                    first = gi * MOE_GROUP
                    group_update(first, MOE_GROUP, b2_ref[0])
                    for k in range(MOE_GROUP):
                        emit(first + k, k % 2, first + k - 2)
                    return carry
                lax.fori_loop(1, n_groups, group, 0)

            for r in range(MOE_GROUP - 1):
                s = n_groups * MOE_GROUP + r

                @pl.when(s < nsub)
                def _():
                    single_update(s, b2_ref[0])

                    @pl.when(s >= 2)
                    def _():
                        store_copy(s - 2, r % 2).wait()
                    emit(s, r % 2, None)

            for k in range(2):
                @pl.when(nsub > k)
                def _():
                    store_copy(nsub - 1 - (nsub - 1 - k) % 2, k).wait()


def _moe_mlp(item_e, item_start, item_nsub, xs, w1, b1, w2, b2):
    n_rows, _, dw = xs.shape
    n_exp, d, two_i = w1.shape
    tc = min(MOE_COL_CHUNK, two_i)
    n_chunks = two_i // tc
    n_items = item_e.shape[0]
    last = n_chunks - 1

    def chunk(c, it, insub):
        return jnp.where(insub[it] > 0, c, last)

    grid_spec = pltpu.PrefetchScalarGridSpec(
        num_scalar_prefetch=3,
        grid=(n_items, n_chunks),
        in_specs=[
            pl.BlockSpec(memory_space=pl.ANY),
            pl.BlockSpec((1, d, tc), lambda it, c, ie, ist, ins: (ie[it], 0, chunk(c, it, ins))),
            pl.BlockSpec((1, 1, tc), lambda it, c, ie, ist, ins: (ie[it], 0, chunk(c, it, ins))),
            pl.BlockSpec((1, tc // 2, d), lambda it, c, ie, ist, ins: (ie[it], chunk(c, it, ins), 0)),
            pl.BlockSpec((1, 1, d), lambda it, c, ie, ist, ins: (ie[it], 0, 0)),
        ],
        out_specs=pl.BlockSpec(memory_space=pl.ANY),
        scratch_shapes=[
            pltpu.VMEM((MOE_ROWS_PER_ITEM, d), BF16),
            pltpu.VMEM((MOE_ROWS_PER_ITEM, d), F32),
            pltpu.VMEM((MOE_ROW_TILE, 1, d), F32),
            pltpu.VMEM((MOE_ROW_TILE, 1, d), F32),
            pltpu.VMEM((MOE_ROW_TILE, 1, dw), jnp.int32),
            pltpu.VMEM((MOE_ROW_TILE, 1, dw), jnp.int32),
            pltpu.VMEM((MOE_ROW_TILE, dw), jnp.int32),
            pltpu.VMEM((d, tc), BF16),
            pltpu.VMEM((d // LANES, tc // 2, LANES), F32),
            pltpu.VMEM((tc // 2, d), BF16),
            pltpu.SemaphoreType.DMA((2,)),
        ],
    )
    return pl.pallas_call(
        _moe_body,
        grid_spec=grid_spec,
        out_shape=jax.ShapeDtypeStruct((n_rows, 1, d), F32),
        compiler_params=_cparams("arbitrary", "arbitrary"),
        name="moe_mlp",
    )(item_e, item_start, item_nsub, xs, w1, b1, w2, b2)


def _combine_body(idx_ref, rank_ref, pend_ref, padded_ref, x_ref, gate_ref, g_ref, ys_ref, o_ref, b0, b1, b2, b3, flat_ref, sem, *, final_norm):
    i = pl.program_id(0)
    bm = x_ref.shape[0]
    bufs = (b0, b1, b2, b3)

    def row_copy(r, kk):
        d = _sorted_row(idx_ref, rank_ref, pend_ref, padded_ref, (i * bm + r) * TOP_K + kk)
        return pltpu.make_async_copy(ys_ref.at[d], bufs[kk].at[r], sem)

    def issue(r):
        for kk in range(TOP_K):
            row_copy(r, kk).start(priority=kk % 2)

    def drain(r):
        for kk in range(TOP_K):
            row_copy(r, kk).wait()

    _for_rows(bm, issue)
    _for_rows(bm, drain)
    acc = x_ref[...]
    for kk in range(TOP_K):
        flat_ref[...] = bufs[kk][...].reshape(flat_ref.shape)
        acc = acc + gate_ref[:, kk:kk + 1] * flat_ref[...]
    o_ref[...] = _rmsnorm(acc, g_ref[...]) if final_norm else acc


def _combine(idx_flat, rank_flat, pend, padded, x2d, gate4, g, ys, bm, final_norm):
    m, d = x2d.shape
    bm = min(bm, m)
    grid_spec = pltpu.PrefetchScalarGridSpec(
        num_scalar_prefetch=4,
        grid=(m // bm,),
        in_specs=[
            pl.BlockSpec((bm, d), lambda i, *_: (i, 0)),
            pl.BlockSpec((bm, LANES), lambda i, *_: (i, 0)),
            pl.BlockSpec((1, d), lambda i, *_: (0, 0)),
            pl.BlockSpec(memory_space=pl.ANY),
        ],
        out_specs=pl.BlockSpec((bm, d), lambda i, *_: (i, 0)),
        scratch_shapes=[pltpu.VMEM((bm, 1, d), F32)] * TOP_K + [pltpu.VMEM((bm, d), F32), pltpu.SemaphoreType.DMA],
    )
    return pl.pallas_call(
        functools.partial(_combine_body, final_norm=final_norm),
        grid_spec=grid_spec,
        out_shape=jax.ShapeDtypeStruct((m, d), F32),
        compiler_params=_cparams("arbitrary"),
        name="moe_combine",
    )(idx_flat, rank_flat, pend, padded, x2d, gate4, g.reshape(1, d), ys)


def _routing_tables(counts, n_assign, n_exp):
    tm, rpi = MOE_ROW_TILE, MOE_ROWS_PER_ITEM
    padded = (counts + tm - 1) // tm * tm
    pend = jnp.cumsum(padded)
    pstart = pend - padded

    n_rows = n_assign + n_exp * tm
    n_items = n_rows // rpi + n_exp
    per_e = (padded + rpi - 1) // rpi
    icum = jnp.cumsum(per_e)
    ioff = icum - per_e
    total = icum[-1]
    ii = jnp.arange(n_items, dtype=jnp.int32)
    ic = jnp.minimum(ii, total - 1)
    e_of = jnp.sum((icum[None, :] <= ic[:, None]).astype(jnp.int32), axis=1)
    e_of = jnp.minimum(e_of, n_exp - 1)
    local = ii - ioff[e_of]
    valid = ii < total
    start = jnp.where(valid, pstart[e_of] + local * rpi, 0).astype(jnp.int32)
    nsub = jnp.where(valid, jnp.clip((padded[e_of] - local * rpi) // tm, 0, rpi // tm), 0).astype(jnp.int32)
    return pend.astype(jnp.int32), padded.astype(jnp.int32), e_of, start, nsub, n_rows


def _alibi_slopes(n):
    start = 2.0 ** (-8.0 / n)
    return np.array([start ** (i + 1) for i in range(n)], dtype=np.float32)


def kernel(x, mem, g_mix, w_in, b_forget, swa_sinks, w_out, g_cross, g_mem, w_cq, w_ckv, w_co, g_moe,
           w_router, b_router, w_mlp1, b_mlp1, w_mlp2, b_mlp2, g_final):
    batch, seq, d = x.shape
    n_mem = mem.shape[1]
    depth = w_in.shape[0]
    n_fox = d // (2 * HEAD_DIM)
    n_swa = d // (2 * HEAD_DIM)
    n_kv = n_swa // SWA_GROUP
    fox_w = n_fox * HEAD_DIM
    swa_q_w = n_swa * HEAD_DIM
    swa_kv_w = n_kv * HEAD_DIM
    o2 = 2 * fox_w
    o3 = 3 * fox_w
    o4 = o3 + n_fox
    o5 = o4 + swa_q_w
    o6 = o5 + swa_kv_w
    n_exp = w_router.shape[2]
    t = batch * seq
    slopes = jnp.asarray(_alibi_slopes(n_swa))

    def dup_heads(wk):
        wk = wk.reshape(d, n_kv, 1, HEAD_DIM)
        return jnp.broadcast_to(wk, (d, n_kv, 2, HEAD_DIM)).reshape(d, n_kv * LANES)

    x2d = x.reshape(t, d)
    for l in range(depth):
        wl = w_in[l]
        w_qk = wl[:, :o2].astype(BF16)
        w_vt = wl[:, o2:o3].T.astype(BF16)
        w_ff = jnp.zeros((d, LANES), BF16).at[:, :n_fox].set(wl[:, o3:o4].astype(BF16))
        b_ff = jnp.zeros((1, LANES), F32).at[0, :n_fox].set(b_forget[l])
        w_swa = jnp.concatenate(
            [wl[:, o4:o5], dup_heads(wl[:, o5:o6]), dup_heads(wl[:, o6:])], axis=1).astype(BF16)

        proj = _norm_matmul(x2d, g_mix[l], jnp.concatenate([w_qk, w_swa], axis=1), BF16, 1024, fox_w,
                            "qk_swa_proj", lead_tiles=1, lead_scale=HEAD_DIM ** -0.5 * LOG2E)
        vt = _norm_matmul(x2d, g_mix[l], w_vt, BF16, 1024, 1024, "fox_vt_proj", transposed=True)
        c = _fox_gate(x2d.reshape(batch, seq, d), g_mix[l], w_ff, b_ff, 512)
        fox = _fox_attention(proj, vt, c, batch, seq, fox_w // LANES, FOX_TILE)
        swa = _swa_attention(proj, o2 // LANES, slopes, swa_sinks[l], batch, seq, swa_q_w // LANES, n_kv)
        x2d = _out_proj(fox, swa, w_out[l].astype(BF16), x2d, 1024, 1024)

        kv = _norm_matmul(mem.reshape(batch * n_mem, d), g_mem[l], w_ckv[l].astype(BF16), BF16, 1024, 1024,
                          "mem_kv_proj")
        x2d = _cross_block(x2d, g_cross[l], w_cq[l].astype(BF16), kv, w_co[l].astype(BF16), seq, n_mem, 512)

        w_r = jnp.zeros((d, LANES), BF16).at[:, :n_exp].set(w_router[l].astype(BF16))
        b_r = jnp.full((1, LANES), NEG_BIG, F32).at[0, :n_exp].set(b_router[l])
        idx4, gate4, rank4, cnt, h3d = _router(x2d, g_moe[l], w_r, b_r, 512)
        pend, padded, item_e, item_start, item_nsub, n_rows = _routing_tables(cnt[0, :n_exp], t * TOP_K, n_exp)
        idx_flat = idx4[:, :TOP_K].reshape(-1)
        rank_flat = rank4[:, :TOP_K].reshape(-1)
        xs = _dispatch(idx_flat, rank_flat, pend, padded, h3d, n_rows, 256)
        two_i = w_mlp1.shape[3]
        ys = _moe_mlp(item_e, item_start, item_nsub, xs, w_mlp1[l], b_mlp1[l].reshape(n_exp, 1, two_i),
                      w_mlp2[l], b_mlp2[l].reshape(n_exp, 1, d))
        x2d = _combine(idx_flat, rank_flat, pend, padded, x2d, gate4, g_final, ys, 256,
                       final_norm=(l == depth - 1))
    return x2d.reshape(batch, seq, d)
```

```python
import functools

import numpy as np
import jax
import jax.numpy as jnp
from jax import lax
from jax.experimental import pallas as pl
from jax.experimental.pallas import tpu as pltpu

F32 = jnp.float32
BF16 = jnp.bfloat16

RMS_EPS = 1e-5
HEAD_DIM = 64
WINDOW = 128
SWA_GROUP = 4
N_CROSS_HEADS = 4
TOP_K = 4
SWIGLU_ALPHA = 1.702
SWIGLU_LIMIT = 7.0

LANES = 128
MXU_DIM = 256
VMEM_LIMIT_BYTES = 60 * 2**20
NEG_BIG = -1e30
LOG2E = 1.4426950408889634

FOX_TILE = 1024
SWA_WINDOWS_PER_STEP = 8
ROW_COPY_UNROLL = 8
MOE_ROW_TILE = 256
MOE_ROWS_PER_ITEM = 2304
MOE_COL_CHUNK = 512
MOE_GROUP = 4


def _cparams(*sem):
    return pltpu.CompilerParams(dimension_semantics=sem, vmem_limit_bytes=VMEM_LIMIT_BYTES)


def _rmsnorm(x, g):
    ms = jnp.mean(x * x, axis=-1, keepdims=True)
    return x * lax.rsqrt(ms + RMS_EPS) * g


def _for_rows(n, fn):
    def trip(o, carry):
        for u in range(ROW_COPY_UNROLL):
            fn(o * ROW_COPY_UNROLL + u)
        return carry
    lax.fori_loop(0, n // ROW_COPY_UNROLL, trip, 0)


def _dot_nt(a, b):
    return lax.dot_general(a, b, (((1,), (1,)), ((), ())), preferred_element_type=F32)


def _norm_matmul_body(x_ref, g_ref, w_ref, o_ref, h_ref, *, transposed, lead_tiles, lead_scale):
    @pl.when(pl.program_id(1) == 0)
    def _():
        h_ref[...] = _rmsnorm(x_ref[...], g_ref[...]).astype(h_ref.dtype)

    if transposed:
        o_ref[...] = _dot_nt(w_ref[...], h_ref[...]).astype(o_ref.dtype)
    else:
        acc = jnp.dot(h_ref[...], w_ref[...], preferred_element_type=F32)
        if lead_tiles:
            acc = acc * jnp.where(pl.program_id(1) < lead_tiles, lead_scale, 1.0)
        o_ref[...] = acc.astype(o_ref.dtype)


def _norm_matmul(x2d, g, w, out_dtype, bm, bn, name, transposed=False, lead_tiles=0, lead_scale=1.0):
    m, d = x2d.shape
    n = w.shape[0] if transposed else w.shape[1]
    bm, bn = min(bm, m), min(bn, n)
    if transposed:
        w_spec = pl.BlockSpec((bn, d), lambda i, j: (j, 0))
        o_spec = pl.BlockSpec((bn, bm), lambda i, j: (j, i))
        o_shape = (n, m)
    else:
        w_spec = pl.BlockSpec((d, bn), lambda i, j: (0, j))
        o_spec = pl.BlockSpec((bm, bn), lambda i, j: (i, j))
        o_shape = (m, n)
    return pl.pallas_call(
        functools.partial(_norm_matmul_body, transposed=transposed, lead_tiles=lead_tiles, lead_scale=lead_scale),
        grid=(m // bm, n // bn),
        in_specs=[pl.BlockSpec((bm, d), lambda i, j: (i, 0)), pl.BlockSpec((1, d), lambda i, j: (0, 0)), w_spec],
        out_specs=o_spec,
        out_shape=jax.ShapeDtypeStruct(o_shape, out_dtype),
        scratch_shapes=[pltpu.VMEM((bm, d), BF16)],
        compiler_params=_cparams("parallel", "arbitrary"),
        name=name,
    )(x2d, g.reshape(1, d), w)


def _fox_gate_body(x_ref, g_ref, w_ref, b_ref, c_ref, carry_ref):
    @pl.when(pl.program_id(1) == 0)
    def _():
        carry_ref[...] = jnp.zeros_like(carry_ref)

    bs = x_ref.shape[1]
    h = _rmsnorm(x_ref[0], g_ref[...]).astype(BF16)
    z = jnp.dot(h, w_ref[...], preferred_element_type=F32) + b_ref[...]
    logf = jnp.minimum(z, 0.0) - jnp.log1p(jnp.exp(-jnp.abs(z)))
    r = lax.broadcasted_iota(jnp.int32, (bs, bs), 0)
    c = lax.broadcasted_iota(jnp.int32, (bs, bs), 1)
    tri = (c <= r).astype(F32)
    cs = jnp.dot(tri, logf, preferred_element_type=F32, precision=lax.Precision.HIGHEST)
    cs = cs + carry_ref[0:1, :]
    c_ref[0] = cs * LOG2E
    carry_ref[...] = jnp.broadcast_to(cs[bs - 1:bs, :], carry_ref.shape)


def _fox_gate(x, g, w_pad, b_pad, bs):
    b, s, d = x.shape
    bs = min(bs, s)
    return pl.pallas_call(
        _fox_gate_body,
        grid=(b, s // bs),
        in_specs=[
            pl.BlockSpec((1, bs, d), lambda i, j: (i, j, 0)),
            pl.BlockSpec((1, d), lambda i, j: (0, 0)),
            pl.BlockSpec((d, LANES), lambda i, j: (0, 0)),
            pl.BlockSpec((1, LANES), lambda i, j: (0, 0)),
        ],
        out_specs=pl.BlockSpec((1, bs, LANES), lambda i, j: (i, j, 0)),
        out_shape=jax.ShapeDtypeStruct((b, s, LANES), F32),
        scratch_shapes=[pltpu.VMEM((8, LANES), F32)],
        compiler_params=_cparams("parallel", "arbitrary"),
        name="fox_gate",
    )(x, g.reshape(1, d), w_pad, b_pad)


def _fox_attn_body(itab_ref, jtab_ref, q_ref, k_ref, vt_ref, c_ref, o_ref, m_ref, l_ref, acc_ref):
    p = pl.program_id(1)
    st = pl.program_id(2)
    i = itab_ref[st]
    j = jtab_ref[st]
    tq = q_ref.shape[0]
    tk = k_ref.shape[0]
    lane = lax.broadcasted_iota(jnp.int32, (1, LANES), 1)

    @pl.when(j == 0)
    def _():
        m_ref[...] = jnp.full_like(m_ref, NEG_BIG)
        l_ref[...] = jnp.zeros_like(l_ref)
        acc_ref[...] = jnp.zeros_like(acc_ref)

    def step(masked):
        q = q_ref[...]
        k = k_ref[...]
        vt = vt_ref[...]
        cblk = c_ref[0]
        if masked:
            key = lax.broadcasted_iota(jnp.int32, (tk, tq), 0)
            qry = lax.broadcasted_iota(jnp.int32, (tk, tq), 1)
            causal = key <= qry
        scores = []
        for hh in range(2):
            qh = jnp.where(lane // HEAD_DIM == hh, q, jnp.zeros_like(q))
            scores.append(_dot_nt(k, qh))
        n_split = 2 if tq % (2 * MXU_DIM) == 0 else 1
        wq = tq // n_split
        for hh in range(2):
            c_col = jnp.sum(jnp.where(lane == 2 * p + hh, cblk, 0.0), axis=1, keepdims=True)
            for g in range(n_split):
                cols = slice(g * wq, (g + 1) * wq)
                s = scores[hh][:, cols] - c_col
                if masked:
                    s = jnp.where(causal[:, cols], s, NEG_BIG)
                m_prev = m_ref[hh, :, cols]
                m_new = jnp.maximum(m_prev, jnp.max(s, axis=0, keepdims=True))
                alpha = jnp.exp2(m_prev - m_new)
                pt = jnp.exp2(s - m_new)
                l_ref[hh, :, cols] = alpha * l_ref[hh, :, cols] + jnp.sum(pt, axis=0, keepdims=True)
                acc_ref[hh, :, cols] = alpha * acc_ref[hh, :, cols] + jnp.dot(
                    vt, pt.astype(BF16), preferred_element_type=F32)
                m_ref[hh, :, cols] = m_new

    @pl.when(j < i)
    def _():
        step(False)

    @pl.when(j == i)
    def _():
        step(True)
        row = lax.broadcasted_iota(jnp.int32, (LANES, 1), 0)
        ot = jnp.where(row < HEAD_DIM, acc_ref[0] / l_ref[0], acc_ref[1] / l_ref[1])
        o_ref[...] = ot.T.astype(o_ref.dtype)


def _fox_attention(qk, vt, c, batch, seq, n_pairs, tile):
    tile = min(tile, seq)
    nq = seq // tile
    pairs = [(i, j) for i in range(nq) for j in range(i + 1)]
    itab = jnp.asarray([ij[0] for ij in pairs], jnp.int32)
    jtab = jnp.asarray([ij[1] for ij in pairs], jnp.int32)
    grid_spec = pltpu.PrefetchScalarGridSpec(
        num_scalar_prefetch=2,
        grid=(batch, n_pairs, len(pairs)),
        in_specs=[
            pl.BlockSpec((tile, LANES), lambda b, p, s, it, jt: (b * nq + it[s], p)),
            pl.BlockSpec((tile, LANES), lambda b, p, s, it, jt: (b * nq + jt[s], n_pairs + p)),
            pl.BlockSpec((LANES, tile), lambda b, p, s, it, jt: (p, b * nq + jt[s])),
            pl.BlockSpec((1, tile, LANES), lambda b, p, s, it, jt: (b, jt[s], 0)),
        ],
        out_specs=pl.BlockSpec((tile, LANES), lambda b, p, s, it, jt: (b * nq + it[s], p)),
        scratch_shapes=[
            pltpu.VMEM((2, 1, tile), F32),
            pltpu.VMEM((2, 1, tile), F32),
            pltpu.VMEM((2, LANES, tile), F32),
        ],
    )
    return pl.pallas_call(
        _fox_attn_body,
        grid_spec=grid_spec,
        out_shape=jax.ShapeDtypeStruct((batch * seq, n_pairs * LANES), BF16),
        compiler_params=_cparams("parallel", "parallel", "arbitrary"),
        name="fox_attention",
    )(itab, jtab, qk, qk, vt, c)


def _swa_body(slopes_ref, sinks_ref, q_ref, kp_ref, kc_ref, vp_ref, vc_ref, o_ref):
    pr = pl.program_id(1)
    n = pl.program_id(2)
    w = WINDOW
    n_win = q_ref.shape[0] // w
    lane = lax.broadcasted_iota(jnp.int32, (1, LANES), 1)
    row = lax.broadcasted_iota(jnp.int32, (2 * w, 1), 0)
    head1 = row >= w
    slope = jnp.where(head1, slopes_ref[2 * pr + 1], slopes_ref[2 * pr])
    sink = jnp.where(head1, sinks_ref[2 * pr + 1], sinks_ref[2 * pr])
    qi = lax.broadcasted_iota(jnp.int32, (2 * w, 2 * w), 0) % w
    kj = lax.broadcasted_iota(jnp.int32, (2 * w, 2 * w), 1)
    rel = qi - kj + w
    in_window = (rel >= 0) & (rel < w)
    bias = slope * rel.astype(F32)
    for wi in range(n_win):
        qw = q_ref[wi * w:(wi + 1) * w, :]
        if wi == 0:
            k = jnp.concatenate([kp_ref[...], kc_ref[0:w, :]], axis=0)
            v = jnp.concatenate([vp_ref[...], vc_ref[0:w, :]], axis=0)
            valid = in_window & (kj >= jnp.where(n > 0, 0, w))
        else:
            k = kc_ref[(wi - 1) * w:(wi + 1) * w, :]
            v = vc_ref[(wi - 1) * w:(wi + 1) * w, :]
            valid = in_window
        q2 = jnp.concatenate([jnp.where(lane < HEAD_DIM, qw, jnp.zeros_like(qw)),
                              jnp.where(lane >= HEAD_DIM, qw, jnp.zeros_like(qw))], axis=0)
        s = _dot_nt(q2, k) * (HEAD_DIM ** -0.5) - bias
        s = jnp.where(valid, s, NEG_BIG)
        m = jnp.maximum(jnp.max(s, axis=-1, keepdims=True), sink)
        p = jnp.exp(s - m)
        denom = jnp.sum(p, axis=-1, keepdims=True) + jnp.exp(sink - m)
        o = jnp.dot((p / denom).astype(BF16), v, preferred_element_type=F32)
        o_ref[wi * w:(wi + 1) * w, :] = jnp.where(lane < HEAD_DIM, o[:w], o[w:]).astype(o_ref.dtype)


def _swa_attention(proj, col0, slopes, sinks, batch, seq, n_pairs, n_kv):
    n_win = min(SWA_WINDOWS_PER_STEP, seq // WINDOW)
    blk = n_win * WINDOW
    nb = seq // blk
    wins = seq // WINDOW

    def q_map(b, p, n):
        return (b * nb + n, p)

    def q_in_map(b, p, n):
        return (b * nb + n, col0 + p)

    def cur_map(base):
        return lambda b, p, n: (b * nb + n, col0 + base + p // 2)

    def prev_map(base):
        return lambda b, p, n: (b * wins + jnp.maximum(n * n_win - 1, 0), col0 + base + p // 2)

    smem = pl.BlockSpec(memory_space=pltpu.SMEM)
    return pl.pallas_call(
        _swa_body,
        grid=(batch, n_pairs, nb),
        in_specs=[
            smem, smem,
            pl.BlockSpec((blk, LANES), q_in_map),
            pl.BlockSpec((WINDOW, LANES), prev_map(n_pairs)),
            pl.BlockSpec((blk, LANES), cur_map(n_pairs)),
            pl.BlockSpec((WINDOW, LANES), prev_map(n_pairs + n_kv)),
            pl.BlockSpec((blk, LANES), cur_map(n_pairs + n_kv)),
        ],
        out_specs=pl.BlockSpec((blk, LANES), q_map),
        out_shape=jax.ShapeDtypeStruct((batch * seq, n_pairs * LANES), BF16),
        compiler_params=_cparams("parallel", "parallel", "parallel"),
        name="swa_attention",
    )(slopes, sinks, proj, proj, proj, proj, proj)


def _out_proj_body(a1_ref, a2_ref, w_ref, x_ref, o_ref):
    a = jnp.concatenate([a1_ref[...], a2_ref[...]], axis=1)
    o_ref[...] = x_ref[...] + jnp.dot(a, w_ref[...], preferred_element_type=F32)


def _out_proj(a1, a2, w, x2d, bm, bn):
    m, d = x2d.shape
    k1, k2 = a1.shape[1], a2.shape[1]
    bm, bn = min(bm, m), min(bn, d)
    return pl.pallas_call(
        _out_proj_body,
        grid=(m // bm, d // bn),
        in_specs=[
            pl.BlockSpec((bm, k1), lambda i, j: (i, 0)),
            pl.BlockSpec((bm, k2), lambda i, j: (i, 0)),
            pl.BlockSpec((k1 + k2, bn), lambda i, j: (0, j)),
            pl.BlockSpec((bm, bn), lambda i, j: (i, j)),
        ],
        out_specs=pl.BlockSpec((bm, bn), lambda i, j: (i, j)),
        out_shape=jax.ShapeDtypeStruct((m, d), F32),
        compiler_params=_cparams("parallel", "parallel"),
        name="out_proj",
    )(a1, a2, w, x2d)


def _cross_body(x_ref, g_ref, wq_ref, kv_ref, wo_ref, o_ref):
    x = x_ref[...]
    h = _rmsnorm(x, g_ref[...]).astype(BF16)
    q = jnp.dot(h, wq_ref[...], preferred_element_type=F32).astype(BF16)
    cw = wq_ref.shape[1]
    hd = cw // N_CROSS_HEADS
    outs = []
    for hh in range(N_CROSS_HEADS):
        qh = q[:, hh * hd:(hh + 1) * hd]
        kh = kv_ref[:, hh * hd:(hh + 1) * hd]
        vh = kv_ref[:, cw + hh * hd:cw + (hh + 1) * hd]
        s = _dot_nt(qh, kh) * (hd ** -0.5)
        e = jnp.exp(s - jnp.max(s, axis=-1, keepdims=True))
        p = (e / jnp.sum(e, axis=-1, keepdims=True)).astype(BF16)
        outs.append(jnp.dot(p, vh, preferred_element_type=F32))
    o = jnp.concatenate(outs, axis=1).astype(BF16)
    o_ref[...] = x + jnp.dot(o, wo_ref[...], preferred_element_type=F32)


def _cross_block(x2d, g, wq, kv, wo, seq, n_mem, bm):
    m, d = x2d.shape
    bm = min(bm, seq)
    per_b = seq // bm
    cw = wq.shape[1]
    return pl.pallas_call(
        _cross_body,
        grid=(m // bm,),
        in_specs=[
            pl.BlockSpec((bm, d), lambda i: (i, 0)),
            pl.BlockSpec((1, d), lambda i: (0, 0)),
            pl.BlockSpec((d, cw), lambda i: (0, 0)),
            pl.BlockSpec((n_mem, 2 * cw), lambda i: (i // per_b, 0)),
            pl.BlockSpec((cw, d), lambda i: (0, 0)),
        ],
        out_specs=pl.BlockSpec((bm, d), lambda i: (i, 0)),
        out_shape=jax.ShapeDtypeStruct((m, d), F32),
        compiler_params=_cparams("parallel"),
        name="cross_block",
    )(x2d, g.reshape(1, d), wq, kv, wo)


def _router_body(x_ref, g_ref, w_ref, b_ref, idx_ref, gate_ref, rank_ref, cnt_ref, h3_ref, hn_ref, carry_ref):
    @pl.when(pl.program_id(0) == 0)
    def _():
        carry_ref[...] = jnp.zeros_like(carry_ref)

    bm = x_ref.shape[0]
    h = _rmsnorm(x_ref[...], g_ref[...]).astype(BF16)
    bits = lax.bitcast_convert_type(h.astype(F32), jnp.int32)
    hw = bits.shape[1] // 2
    hn_ref[...] = lax.shift_right_logical(bits[:, :hw], 16) | (bits[:, hw:] & jnp.int32(-65536))
    h3_ref[...] = hn_ref[...].reshape(h3_ref.shape)
    logits = jnp.dot(h, w_ref[...], preferred_element_type=F32) + b_ref[...]
    lane = lax.broadcasted_iota(jnp.int32, (bm, LANES), 1)
    lane_f = lane.astype(F32)
    work = logits
    self32 = jnp.zeros((bm, LANES), F32)
    vals, picks, hots = [], [], []
    for _ in range(TOP_K):
        mk = jnp.max(work, axis=-1, keepdims=True)
        pick = jnp.min(jnp.where(work == mk, lane_f, float(LANES)), axis=-1, keepdims=True)
        hot = lane_f == pick
        vals.append(mk)
        picks.append(pick.astype(jnp.int32))
        hots.append(hot)
        self32 = jnp.where(hot, 1.0, self32)
        work = jnp.where(hot, -jnp.inf, work)
    es = [jnp.exp(v - vals[0]) for v in vals]
    denom = es[0] + es[1] + es[2] + es[3]

    r = lax.broadcasted_iota(jnp.int32, (bm, bm), 0)
    c = lax.broadcasted_iota(jnp.int32, (bm, bm), 1)
    tri = (c < r).astype(BF16)
    rank = jnp.dot(tri, self32.astype(BF16), preferred_element_type=F32) + carry_ref[0:1, :]
    carry_new = carry_ref[0:1, :] + jnp.sum(self32, axis=0, keepdims=True)
    carry_ref[...] = jnp.broadcast_to(carry_new, carry_ref.shape)

    idx_o = jnp.zeros((bm, LANES), jnp.int32)
    gate_o = jnp.zeros((bm, LANES), F32)
    rank_o = jnp.zeros((bm, LANES), jnp.int32)
    for kk in range(TOP_K):
        rk = jnp.sum(jnp.where(hots[kk], rank, 0.0), axis=-1, keepdims=True).astype(jnp.int32)
        idx_o = jnp.where(lane == kk, picks[kk], idx_o)
        gate_o = jnp.where(lane == kk, es[kk] / denom, gate_o)
        rank_o = jnp.where(lane == kk, rk, rank_o)
    idx_ref[...] = idx_o
    gate_ref[...] = gate_o
    rank_ref[...] = rank_o
    cnt_ref[...] = jnp.broadcast_to(carry_new, cnt_ref.shape).astype(jnp.int32)


def _router(x2d, g, w_pad, b_pad, bm):
    m, d = x2d.shape
    bm = min(bm, m)
    tok = pl.BlockSpec((bm, LANES), lambda i: (i, 0))
    return pl.pallas_call(
        _router_body,
        grid=(m // bm,),
        in_specs=[
            pl.BlockSpec((bm, d), lambda i: (i, 0)),
            pl.BlockSpec((1, d), lambda i: (0, 0)),
            pl.BlockSpec((d, LANES), lambda i: (0, 0)),
            pl.BlockSpec((1, LANES), lambda i: (0, 0)),
        ],
        out_specs=[tok, tok, tok, pl.BlockSpec((8, LANES), lambda i: (0, 0)),
                   pl.BlockSpec((bm, 1, d // 2), lambda i: (i, 0, 0))],
        out_shape=[
            jax.ShapeDtypeStruct((m, LANES), jnp.int32),
            jax.ShapeDtypeStruct((m, LANES), F32),
            jax.ShapeDtypeStruct((m, LANES), jnp.int32),
            jax.ShapeDtypeStruct((8, LANES), jnp.int32),
            jax.ShapeDtypeStruct((m, 1, d // 2), jnp.int32),
        ],
        scratch_shapes=[pltpu.VMEM((bm, d // 2), jnp.int32), pltpu.VMEM((8, LANES), F32)],
        compiler_params=_cparams("arbitrary"),
        name="router",
    )(x2d, g.reshape(1, d), w_pad, b_pad)


def _sorted_row(idx_ref, rank_ref, pend_ref, padded_ref, a):
    e = idx_ref[a]
    return pend_ref[e] - padded_ref[e] + rank_ref[a]


def _dispatch_body(idx_ref, rank_ref, pend_ref, padded_ref, x_ref, xs_ref, z_ref, sem, zsem):
    i = pl.program_id(0)
    bm = x_ref.shape[0]
    n_exp = pend_ref.shape[0]
    tm = z_ref.shape[0]

    def zero_copy(e):
        return pltpu.make_async_copy(z_ref, xs_ref.at[pl.ds(pend_ref[e] - tm, tm)], zsem)

    @pl.when(i == 0)
    def _():
        z_ref[...] = jnp.zeros_like(z_ref)
        for e in range(n_exp):
            @pl.when(padded_ref[e] > 0)
            def _():
                zero_copy(e).start()
        for e in range(n_exp):
            @pl.when(padded_ref[e] > 0)
            def _():
                zero_copy(e).wait()

    def row_copy(r, kk):
        d = _sorted_row(idx_ref, rank_ref, pend_ref, padded_ref, (i * bm + r) * TOP_K + kk)
        return pltpu.make_async_copy(x_ref.at[r], xs_ref.at[d], sem)

    def issue(r):
        for kk in range(TOP_K):
            row_copy(r, kk).start(priority=kk % 2)

    def drain(r):
        for kk in range(TOP_K):
            row_copy(r, kk).wait()

    _for_rows(bm, issue)
    _for_rows(bm, drain)


def _dispatch(idx_flat, rank_flat, pend, padded, x3d, n_rows, bm):
    m, _, d = x3d.shape
    bm = min(bm, m)
    grid_spec = pltpu.PrefetchScalarGridSpec(
        num_scalar_prefetch=4,
        grid=(m // bm,),
        in_specs=[pl.BlockSpec((bm, 1, d), lambda i, *_: (i, 0, 0))],
        out_specs=pl.BlockSpec(memory_space=pl.ANY),
        scratch_shapes=[
            pltpu.VMEM((MOE_ROW_TILE, 1, d), x3d.dtype),
            pltpu.SemaphoreType.DMA,
            pltpu.SemaphoreType.DMA,
        ],
    )
    return pl.pallas_call(
        _dispatch_body,
        grid_spec=grid_spec,
        out_shape=jax.ShapeDtypeStruct((n_rows, 1, d), x3d.dtype),
        compiler_params=_cparams("arbitrary"),
        name="moe_dispatch",
    )(idx_flat, rank_flat, pend, padded, x3d)


def _moe_body(ie_ref, istart_ref, insub_ref, xs_ref, w1_ref, b1_ref, w2_ref, b2_ref,
              ys_ref, xbf_ref, yacc_ref, st0_ref, st1_ref, st2_ref, st3_ref, ld0_ref, ld1_ref, land_ref,
              w1b_ref, w2m_ref, w2b_ref, sems):
    it = pl.program_id(0)
    c = pl.program_id(1)
    n_chunks = pl.num_programs(1)
    nsub = insub_ref[it]
    start = istart_ref[it]
    tm = st0_ref.shape[0]
    tc = w1_ref.shape[2]
    half = tc // 2
    quarter = tc // 4
    even_lane = lax.broadcasted_iota(jnp.int32, (1, half), 1) % 2 == 0
    stage = (st0_ref, st1_ref, st2_ref, st3_ref)
    assert len(stage) == MOE_GROUP
    load_stage = (ld0_ref, ld1_ref)
    n_pairs = (nsub + 1) // 2

    def rows(s):
        return pl.ds(pl.multiple_of(s * tm, tm), tm)

    def load_copy(s, k):
        return pltpu.make_async_copy(xs_ref.at[pl.ds(start + s * tm, tm)], load_stage[k], sems.at[k])

    def store_copy(s, k):
        return pltpu.make_async_copy(stage[k], ys_ref.at[pl.ds(start + s * tm, tm)], sems.at[k])

    def consume(s, k):
        land_ref[...] = load_stage[k][...].reshape(land_ref.shape)
        word = land_ref[...]
        hw = word.shape[1]
        lo = lax.bitcast_convert_type(lax.shift_left(word, 16), F32)
        hi = lax.bitcast_convert_type(word & jnp.int32(-65536), F32)
        xbf_ref[rows(s), :hw] = lo.astype(BF16)
        xbf_ref[rows(s), hw:] = hi.astype(BF16)
        yacc_ref[rows(s), :] = jnp.zeros((tm, yacc_ref.shape[1]), F32)

    def hidden(rs):
        return jnp.dot(xbf_ref[rs, :], w1b_ref[...], preferred_element_type=F32) + b1_ref[0]

    def accumulated(rs, u):
        glu = jnp.minimum(u, SWIGLU_LIMIT)
        act = glu * (1.0 / (1.0 + jnp.exp(-SWIGLU_ALPHA * glu)))
        lin1 = jnp.clip(u, -SWIGLU_LIMIT, SWIGLU_LIMIT) + 1.0
        a_even = act * pltpu.roll(lin1, tc - 1, 1)
        merged = jnp.where(even_lane, a_even[:, :half], pltpu.roll(a_even[:, half:], 1, 1))
        return yacc_ref[rs, :] + jnp.dot(merged.astype(BF16), w2b_ref[...], preferred_element_type=F32)

    def emit(s, k, wait_for):
        if wait_for is not None:
            store_copy(wait_for, k).wait()
        stage[k][...] = yacc_ref[rows(s), :].reshape(stage[k].shape)
        store_copy(s, k).start()

    @pl.when(nsub > 0)
    def _():
        @pl.when(c == 0)
        def _():
            load_copy(0, 0).start()

            def load_pair(sp, carry):
                s0 = 2 * sp
                s1 = s0 + 1

                @pl.when(s1 < nsub)
                def _():
                    load_copy(s1, 1).start()
                load_copy(s0, 0).wait()
                consume(s0, 0)

                @pl.when(s1 < nsub)
                def _():
                    @pl.when(s1 + 1 < nsub)
                    def _():
                        load_copy(s1 + 1, 0).start()
                    load_copy(s1, 1).wait()
                    consume(s1, 1)
                return carry
            lax.fori_loop(0, n_pairs, load_pair, 0)

        w1b_ref[...] = w1_ref[0].astype(BF16)
        for j in range(w2m_ref.shape[0]):
            cols = slice(j * LANES, (j + 1) * LANES)
            w2m_ref[j, pl.ds(0, quarter, stride=2), :] = w2_ref[0, 0:quarter, cols]
            w2m_ref[j, pl.ds(1, quarter, stride=2), :] = w2_ref[0, quarter:2 * quarter, cols]
            w2b_ref[:, cols] = w2m_ref[j].astype(BF16)

        def group_update(first, n, bias):
            rs = [rows(first + k) for k in range(n)]
            us = [hidden(r) for r in rs]
            for k in range(n):
                yacc_ref[rs[k], :] = accumulated(rs[k], us[k]) + bias

        def single_update(s, bias):
            group_update(s, 1, bias)

        @pl.when(c < n_chunks - 1)
        def _():
            n_groups = nsub // MOE_GROUP

            def group(gi, carry):
                group_update(gi * MOE_GROUP, MOE_GROUP, 0.0)
                return carry
            lax.fori_loop(0, n_groups, group, 0)

            def single(s, carry):
                single_update(s, 0.0)
                return carry
            lax.fori_loop(n_groups * MOE_GROUP, nsub, single, 0)

        @pl.when(c == n_chunks - 1)
        def _():
            n_groups = nsub // MOE_GROUP

            @pl.when(n_groups > 0)
            def _():
                group_update(0, MOE_GROUP, b2_ref[0])
                for k in range(MOE_GROUP):
                    emit(k, k, None)

                def group(gi, carry):
                    first = gi * MOE_GROUP
                    group_update(first, MOE_GROUP, b2_ref[0])
                    for k in range(MOE_GROUP):
                        emit(first + k, k, first + k - MOE_GROUP)
                    return carry
                lax.fori_loop(1, n_groups, group, 0)

            for r in range(MOE_GROUP - 1):
                s = n_groups * MOE_GROUP + r

                @pl.when(s < nsub)
                def _():
                    single_update(s, b2_ref[0])

                    @pl.when(s >= MOE_GROUP)
                    def _():
                        store_copy(s - MOE_GROUP, r).wait()
                    emit(s, r, None)

            for k in range(MOE_GROUP):
                @pl.when(nsub > k)
                def _():
                    store_copy(nsub - 1 - (nsub - 1 - k) % MOE_GROUP, k).wait()


def _moe_mlp(item_e, item_start, item_nsub, xs, w1, b1, w2, b2):
    n_rows, _, dw = xs.shape
    n_exp, d, two_i = w1.shape
    tc = min(MOE_COL_CHUNK, two_i)
    n_chunks = two_i // tc
    n_items = item_e.shape[0]
    last = n_chunks - 1

    def chunk(c, it, insub):
        return jnp.where(insub[it] > 0, c, last)

    grid_spec = pltpu.PrefetchScalarGridSpec(
        num_scalar_prefetch=3,
        grid=(n_items, n_chunks),
        in_specs=[
            pl.BlockSpec(memory_space=pl.ANY),
            pl.BlockSpec((1, d, tc), lambda it, c, ie, ist, ins: (ie[it], 0, chunk(c, it, ins))),
            pl.BlockSpec((1, 1, tc), lambda it, c, ie, ist, ins: (ie[it], 0, chunk(c, it, ins))),
            pl.BlockSpec((1, tc // 2, d), lambda it, c, ie, ist, ins: (ie[it], chunk(c, it, ins), 0)),
            pl.BlockSpec((1, 1, d), lambda it, c, ie, ist, ins: (ie[it], 0, 0)),
        ],
        out_specs=pl.BlockSpec(memory_space=pl.ANY),
        scratch_shapes=[
            pltpu.VMEM((MOE_ROWS_PER_ITEM, d), BF16),
            pltpu.VMEM((MOE_ROWS_PER_ITEM, d), F32),
            pltpu.VMEM((MOE_ROW_TILE, 1, d), F32),
            pltpu.VMEM((MOE_ROW_TILE, 1, d), F32),
            pltpu.VMEM((MOE_ROW_TILE, 1, d), F32),
            pltpu.VMEM((MOE_ROW_TILE, 1, d), F32),
            pltpu.VMEM((MOE_ROW_TILE, 1, dw), jnp.int32),
            pltpu.VMEM((MOE_ROW_TILE, 1, dw), jnp.int32),
            pltpu.VMEM((MOE_ROW_TILE, dw), jnp.int32),
            pltpu.VMEM((d, tc), BF16),
            pltpu.VMEM((d // LANES, tc // 2, LANES), F32),
            pltpu.VMEM((tc // 2, d), BF16),
            pltpu.SemaphoreType.DMA((4,)),
        ],
    )
    return pl.pallas_call(
        _moe_body,
        grid_spec=grid_spec,
        out_shape=jax.ShapeDtypeStruct((n_rows, 1, d), F32),
        compiler_params=_cparams("arbitrary", "arbitrary"),
        name="moe_mlp",
    )(item_e, item_start, item_nsub, xs, w1, b1, w2, b2)


def _combine_body(idx_ref, rank_ref, pend_ref, padded_ref, x_ref, gate_ref, g_ref, ys_ref, o_ref, b0, b1, b2, b3, flat_ref, sem, *, final_norm):
    i = pl.program_id(0)
    bm = x_ref.shape[0]
    bufs = (b0, b1, b2, b3)

    def row_copy(r, kk):
        d = _sorted_row(idx_ref, rank_ref, pend_ref, padded_ref, (i * bm + r) * TOP_K + kk)
        return pltpu.make_async_copy(ys_ref.at[d], bufs[kk].at[r], sem)

    def issue(r):
        for kk in range(TOP_K):
            row_copy(r, kk).start(priority=kk % 2)

    def drain(r):
        for kk in range(TOP_K):
            row_copy(r, kk).wait()

    _for_rows(bm, issue)
    _for_rows(bm, drain)
    acc = x_ref[...]
    for kk in range(TOP_K):
        flat_ref[...] = bufs[kk][...].reshape(flat_ref.shape)
        acc = acc + gate_ref[:, kk:kk + 1] * flat_ref[...]
    o_ref[...] = _rmsnorm(acc, g_ref[...]) if final_norm else acc


def _combine(idx_flat, rank_flat, pend, padded, x2d, gate4, g, ys, bm, final_norm):
    m, d = x2d.shape
    bm = min(bm, m)
    grid_spec = pltpu.PrefetchScalarGridSpec(
        num_scalar_prefetch=4,
        grid=(m // bm,),
        in_specs=[
            pl.BlockSpec((bm, d), lambda i, *_: (i, 0)),
            pl.BlockSpec((bm, LANES), lambda i, *_: (i, 0)),
            pl.BlockSpec((1, d), lambda i, *_: (0, 0)),
            pl.BlockSpec(memory_space=pl.ANY),
        ],
        out_specs=pl.BlockSpec((bm, d), lambda i, *_: (i, 0)),
        scratch_shapes=[pltpu.VMEM((bm, 1, d), F32)] * TOP_K + [pltpu.VMEM((bm, d), F32), pltpu.SemaphoreType.DMA],
    )
    return pl.pallas_call(
        functools.partial(_combine_body, final_norm=final_norm),
        grid_spec=grid_spec,
        out_shape=jax.ShapeDtypeStruct((m, d), F32),
        compiler_params=_cparams("arbitrary"),
        name="moe_combine",
    )(idx_flat, rank_flat, pend, padded, x2d, gate4, g.reshape(1, d), ys)


def _routing_tables(counts, n_assign, n_exp):
    tm, rpi = MOE_ROW_TILE, MOE_ROWS_PER_ITEM
    padded = (counts + tm - 1) // tm * tm
    pend = jnp.cumsum(padded)
    pstart = pend - padded

    n_rows = n_assign + n_exp * tm
    n_items = n_rows // rpi + n_exp
    per_e = (padded + rpi - 1) // rpi
    icum = jnp.cumsum(per_e)
    ioff = icum - per_e
    total = icum[-1]
    ii = jnp.arange(n_items, dtype=jnp.int32)
    ic = jnp.minimum(ii, total - 1)
    e_of = jnp.sum((icum[None, :] <= ic[:, None]).astype(jnp.int32), axis=1)
    e_of = jnp.minimum(e_of, n_exp - 1)
    local = ii - ioff[e_of]
    valid = ii < total
    start = jnp.where(valid, pstart[e_of] + local * rpi, 0).astype(jnp.int32)
    nsub = jnp.where(valid, jnp.clip((padded[e_of] - local * rpi) // tm, 0, rpi // tm), 0).astype(jnp.int32)
    return pend.astype(jnp.int32), padded.astype(jnp.int32), e_of, start, nsub, n_rows


def _alibi_slopes(n):
    start = 2.0 ** (-8.0 / n)
    return np.array([start ** (i + 1) for i in range(n)], dtype=np.float32)


def kernel(x, mem, g_mix, w_in, b_forget, swa_sinks, w_out, g_cross, g_mem, w_cq, w_ckv, w_co, g_moe,
           w_router, b_router, w_mlp1, b_mlp1, w_mlp2, b_mlp2, g_final):
    batch, seq, d = x.shape
    n_mem = mem.shape[1]
    depth = w_in.shape[0]
    n_fox = d // (2 * HEAD_DIM)
    n_swa = d // (2 * HEAD_DIM)
    n_kv = n_swa // SWA_GROUP
    fox_w = n_fox * HEAD_DIM
    swa_q_w = n_swa * HEAD_DIM
    swa_kv_w = n_kv * HEAD_DIM
    o2 = 2 * fox_w
    o3 = 3 * fox_w
    o4 = o3 + n_fox
    o5 = o4 + swa_q_w
    o6 = o5 + swa_kv_w
    n_exp = w_router.shape[2]
    t = batch * seq
    slopes = jnp.asarray(_alibi_slopes(n_swa))

    def dup_heads(wk):
        wk = wk.reshape(d, n_kv, 1, HEAD_DIM)
        return jnp.broadcast_to(wk, (d, n_kv, 2, HEAD_DIM)).reshape(d, n_kv * LANES)

    x2d = x.reshape(t, d)
    for l in range(depth):
        wl = w_in[l]
        w_qk = wl[:, :o2].astype(BF16)
        w_vt = wl[:, o2:o3].T.astype(BF16)
        w_ff = jnp.zeros((d, LANES), BF16).at[:, :n_fox].set(wl[:, o3:o4].astype(BF16))
        b_ff = jnp.zeros((1, LANES), F32).at[0, :n_fox].set(b_forget[l])
        w_swa = jnp.concatenate(
            [wl[:, o4:o5], dup_heads(wl[:, o5:o6]), dup_heads(wl[:, o6:])], axis=1).astype(BF16)

        proj = _norm_matmul(x2d, g_mix[l], jnp.concatenate([w_qk, w_swa], axis=1), BF16, 1024, fox_w,
                            "qk_swa_proj", lead_tiles=1, lead_scale=HEAD_DIM ** -0.5 * LOG2E)
        vt = _norm_matmul(x2d, g_mix[l], w_vt, BF16, 1024, 1024, "fox_vt_proj", transposed=True)
        c = _fox_gate(x2d.reshape(batch, seq, d), g_mix[l], w_ff, b_ff, 512)
        fox = _fox_attention(proj, vt, c, batch, seq, fox_w // LANES, FOX_TILE)
        swa = _swa_attention(proj, o2 // LANES, slopes, swa_sinks[l], batch, seq, swa_q_w // LANES, n_kv)
        x2d = _out_proj(fox, swa, w_out[l].astype(BF16), x2d, 1024, 1024)

        kv = _norm_matmul(mem.reshape(batch * n_mem, d), g_mem[l], w_ckv[l].astype(BF16), BF16, 1024, 1024,
                          "mem_kv_proj")
        x2d = _cross_block(x2d, g_cross[l], w_cq[l].astype(BF16), kv, w_co[l].astype(BF16), seq, n_mem, 512)

        w_r = jnp.zeros((d, LANES), BF16).at[:, :n_exp].set(w_router[l].astype(BF16))
        b_r = jnp.full((1, LANES), NEG_BIG, F32).at[0, :n_exp].set(b_router[l])
        idx4, gate4, rank4, cnt, h3d = _router(x2d, g_moe[l], w_r, b_r, 512)
        pend, padded, item_e, item_start, item_nsub, n_rows = _routing_tables(cnt[0, :n_exp], t * TOP_K, n_exp)
        idx_flat = idx4[:, :TOP_K].reshape(-1)
        rank_flat = rank4[:, :TOP_K].reshape(-1)
        xs = _dispatch(idx_flat, rank_flat, pend, padded, h3d, n_rows, 256)
        two_i = w_mlp1.shape[3]
        ys = _moe_mlp(item_e, item_start, item_nsub, xs, w_mlp1[l], b_mlp1[l].reshape(n_exp, 1, two_i),
                      w_mlp2[l], b_mlp2[l].reshape(n_exp, 1, d))
        x2d = _combine(idx_flat, rank_flat, pend, padded, x2d, gate4, g_final, ys, 256,
                       final_norm=(l == depth - 1))
    return x2d.reshape(batch, seq, d)
```

```python
import functools

import numpy as np
import jax
import jax.numpy as jnp
from jax import lax
from jax.experimental import pallas as pl
from jax.experimental.pallas import tpu as pltpu

F32 = jnp.float32
BF16 = jnp.bfloat16

RMS_EPS = 1e-5
HEAD_DIM = 64
WINDOW = 128
SWA_GROUP = 4
N_CROSS_HEADS = 4
TOP_K = 4
SWIGLU_ALPHA = 1.702
SWIGLU_LIMIT = 7.0

LANES = 128
MXU_DIM = 256
VMEM_LIMIT_BYTES = 56 * 2**20
NEG_BIG = -1e30
LOG2E = 1.4426950408889634

FOX_TILE = 1024
SWA_WINDOWS_PER_STEP = 8
ROW_COPY_UNROLL = 8
ROW_COPY_TOKENS = 512
MOE_ROW_TILE = 256
MOE_ROWS_PER_ITEM = 2304
MOE_COL_CHUNK = 512
MOE_GROUP = 4


def _cparams(*sem):
    return pltpu.CompilerParams(dimension_semantics=sem, vmem_limit_bytes=VMEM_LIMIT_BYTES)


def _rmsnorm(x, g):
    ms = jnp.mean(x * x, axis=-1, keepdims=True)
    return x * lax.rsqrt(ms + RMS_EPS) * g


def _for_rows(n, fn):
    def trip(o, carry):
        for u in range(ROW_COPY_UNROLL):
            fn(o * ROW_COPY_UNROLL + u)
        return carry
    lax.fori_loop(0, n // ROW_COPY_UNROLL, trip, 0)


def _dot_nt(a, b):
    return lax.dot_general(a, b, (((1,), (1,)), ((), ())), preferred_element_type=F32)


def _norm_matmul_body(x_ref, g_ref, w_ref, o_ref, h_ref, *, transposed, lead_tiles, lead_scale):
    @pl.when(pl.program_id(1) == 0)
    def _():
        h_ref[...] = _rmsnorm(x_ref[...], g_ref[...]).astype(h_ref.dtype)

    if transposed:
        o_ref[...] = _dot_nt(w_ref[...], h_ref[...]).astype(o_ref.dtype)
    else:
        acc = jnp.dot(h_ref[...], w_ref[...], preferred_element_type=F32)
        if lead_tiles:
            acc = acc * jnp.where(pl.program_id(1) < lead_tiles, lead_scale, 1.0)
        o_ref[...] = acc.astype(o_ref.dtype)


def _norm_matmul(x2d, g, w, out_dtype, bm, bn, name, transposed=False, lead_tiles=0, lead_scale=1.0):
    m, d = x2d.shape
    n = w.shape[0] if transposed else w.shape[1]
    bm, bn = min(bm, m), min(bn, n)
    if transposed:
        w_spec = pl.BlockSpec((bn, d), lambda i, j: (j, 0))
        o_spec = pl.BlockSpec((bn, bm), lambda i, j: (j, i))
        o_shape = (n, m)
    else:
        w_spec = pl.BlockSpec((d, bn), lambda i, j: (0, j))
        o_spec = pl.BlockSpec((bm, bn), lambda i, j: (i, j))
        o_shape = (m, n)
    return pl.pallas_call(
        functools.partial(_norm_matmul_body, transposed=transposed, lead_tiles=lead_tiles, lead_scale=lead_scale),
        grid=(m // bm, n // bn),
        in_specs=[pl.BlockSpec((bm, d), lambda i, j: (i, 0)), pl.BlockSpec((1, d), lambda i, j: (0, 0)), w_spec],
        out_specs=o_spec,
        out_shape=jax.ShapeDtypeStruct(o_shape, out_dtype),
        scratch_shapes=[pltpu.VMEM((bm, d), BF16)],
        compiler_params=_cparams("parallel", "arbitrary"),
        name=name,
    )(x2d, g.reshape(1, d), w)


def _fox_gate_body(x_ref, g_ref, w_ref, b_ref, c_ref, carry_ref):
    @pl.when(pl.program_id(1) == 0)
    def _():
        carry_ref[...] = jnp.zeros_like(carry_ref)

    bs = x_ref.shape[1]
    h = _rmsnorm(x_ref[0], g_ref[...]).astype(BF16)
    z = jnp.dot(h, w_ref[...], preferred_element_type=F32) + b_ref[...]
    logf = jnp.minimum(z, 0.0) - jnp.log1p(jnp.exp(-jnp.abs(z)))
    r = lax.broadcasted_iota(jnp.int32, (bs, bs), 0)
    c = lax.broadcasted_iota(jnp.int32, (bs, bs), 1)
    tri = (c <= r).astype(F32)
    cs = jnp.dot(tri, logf, preferred_element_type=F32, precision=lax.Precision.HIGHEST)
    cs = cs + carry_ref[0:1, :]
    c_ref[0] = cs * LOG2E
    carry_ref[...] = jnp.broadcast_to(cs[bs - 1:bs, :], carry_ref.shape)


def _fox_gate(x, g, w_pad, b_pad, bs):
    b, s, d = x.shape
    bs = min(bs, s)
    return pl.pallas_call(
        _fox_gate_body,
        grid=(b, s // bs),
        in_specs=[
            pl.BlockSpec((1, bs, d), lambda i, j: (i, j, 0)),
            pl.BlockSpec((1, d), lambda i, j: (0, 0)),
            pl.BlockSpec((d, LANES), lambda i, j: (0, 0)),
            pl.BlockSpec((1, LANES), lambda i, j: (0, 0)),
        ],
        out_specs=pl.BlockSpec((1, bs, LANES), lambda i, j: (i, j, 0)),
        out_shape=jax.ShapeDtypeStruct((b, s, LANES), F32),
        scratch_shapes=[pltpu.VMEM((8, LANES), F32)],
        compiler_params=_cparams("parallel", "arbitrary"),
        name="fox_gate",
    )(x, g.reshape(1, d), w_pad, b_pad)


def _fox_attn_body(itab_ref, jtab_ref, q_ref, k_ref, vt_ref, c_ref, o_ref, m_ref, l_ref, acc_ref):
    p = pl.program_id(1)
    st = pl.program_id(2)
    i = itab_ref[st]
    j = jtab_ref[st]
    tq = q_ref.shape[0]
    tk = k_ref.shape[0]
    lane = lax.broadcasted_iota(jnp.int32, (1, LANES), 1)

    @pl.when(j == 0)
    def _():
        m_ref[...] = jnp.full_like(m_ref, NEG_BIG)
        l_ref[...] = jnp.zeros_like(l_ref)
        acc_ref[...] = jnp.zeros_like(acc_ref)

    def step(masked):
        q = q_ref[...]
        k = k_ref[...]
        vt = vt_ref[...]
        cblk = c_ref[0]
        if masked:
            key = lax.broadcasted_iota(jnp.int32, (tk, tq), 0)
            qry = lax.broadcasted_iota(jnp.int32, (tk, tq), 1)
            causal = key <= qry
        scores = []
        for hh in range(2):
            qh = jnp.where(lane // HEAD_DIM == hh, q, jnp.zeros_like(q))
            scores.append(_dot_nt(k, qh))
        n_split = 2 if tq % (2 * MXU_DIM) == 0 else 1
        wq = tq // n_split
        for hh in range(2):
            c_col = jnp.sum(jnp.where(lane == 2 * p + hh, cblk, 0.0), axis=1, keepdims=True)
            for g in range(n_split):
                cols = slice(g * wq, (g + 1) * wq)
                s = scores[hh][:, cols] - c_col
                if masked:
                    s = jnp.where(causal[:, cols], s, NEG_BIG)
                m_prev = m_ref[hh, :, cols]
                m_new = jnp.maximum(m_prev, jnp.max(s, axis=0, keepdims=True))
                alpha = jnp.exp2(m_prev - m_new)
                pt = jnp.exp2(s - m_new)
                l_ref[hh, :, cols] = alpha * l_ref[hh, :, cols] + jnp.sum(pt, axis=0, keepdims=True)
                acc_ref[hh, :, cols] = alpha * acc_ref[hh, :, cols] + jnp.dot(
                    vt, pt.astype(BF16), preferred_element_type=F32)
                m_ref[hh, :, cols] = m_new

    @pl.when(j < i)
    def _():
        step(False)

    @pl.when(j == i)
    def _():
        step(True)
        row = lax.broadcasted_iota(jnp.int32, (LANES, 1), 0)
        ot = jnp.where(row < HEAD_DIM, acc_ref[0] / l_ref[0], acc_ref[1] / l_ref[1])
        o_ref[...] = ot.T.astype(o_ref.dtype)


def _fox_attention(qk, vt, c, batch, seq, n_pairs, tile):
    tile = min(tile, seq)
    nq = seq // tile
    pairs = [(i, j) for i in range(nq) for j in range(i + 1)]
    itab = jnp.asarray([ij[0] for ij in pairs], jnp.int32)
    jtab = jnp.asarray([ij[1] for ij in pairs], jnp.int32)
    grid_spec = pltpu.PrefetchScalarGridSpec(
        num_scalar_prefetch=2,
        grid=(batch, n_pairs, len(pairs)),
        in_specs=[
            pl.BlockSpec((tile, LANES), lambda b, p, s, it, jt: (b * nq + it[s], p)),
            pl.BlockSpec((tile, LANES), lambda b, p, s, it, jt: (b * nq + jt[s], n_pairs + p)),
            pl.BlockSpec((LANES, tile), lambda b, p, s, it, jt: (p, b * nq + jt[s])),
            pl.BlockSpec((1, tile, LANES), lambda b, p, s, it, jt: (b, jt[s], 0)),
        ],
        out_specs=pl.BlockSpec((tile, LANES), lambda b, p, s, it, jt: (b * nq + it[s], p)),
        scratch_shapes=[
            pltpu.VMEM((2, 1, tile), F32),
            pltpu.VMEM((2, 1, tile), F32),
            pltpu.VMEM((2, LANES, tile), F32),
        ],
    )
    return pl.pallas_call(
        _fox_attn_body,
        grid_spec=grid_spec,
        out_shape=jax.ShapeDtypeStruct((batch * seq, n_pairs * LANES), BF16),
        compiler_params=_cparams("parallel", "parallel", "arbitrary"),
        name="fox_attention",
    )(itab, jtab, qk, qk, vt, c)


def _swa_body(slopes_ref, sinks_ref, q_ref, kp_ref, kc_ref, vp_ref, vc_ref, o_ref):
    pr = pl.program_id(1)
    n = pl.program_id(2)
    w = WINDOW
    n_win = q_ref.shape[0] // w
    lane = lax.broadcasted_iota(jnp.int32, (1, LANES), 1)
    row = lax.broadcasted_iota(jnp.int32, (2 * w, 1), 0)
    head1 = row >= w
    slope = jnp.where(head1, slopes_ref[2 * pr + 1], slopes_ref[2 * pr])
    sink = jnp.where(head1, sinks_ref[2 * pr + 1], sinks_ref[2 * pr])
    qi = lax.broadcasted_iota(jnp.int32, (2 * w, 2 * w), 0) % w
    kj = lax.broadcasted_iota(jnp.int32, (2 * w, 2 * w), 1)
    rel = qi - kj + w
    in_window = (rel >= 0) & (rel < w)
    bias = slope * rel.astype(F32)
    for wi in range(n_win):
        qw = q_ref[wi * w:(wi + 1) * w, :]
        if wi == 0:
            k = jnp.concatenate([kp_ref[...], kc_ref[0:w, :]], axis=0)
            v = jnp.concatenate([vp_ref[...], vc_ref[0:w, :]], axis=0)
            valid = in_window & (kj >= jnp.where(n > 0, 0, w))
        else:
            k = kc_ref[(wi - 1) * w:(wi + 1) * w, :]
            v = vc_ref[(wi - 1) * w:(wi + 1) * w, :]
            valid = in_window
        q2 = jnp.concatenate([jnp.where(lane < HEAD_DIM, qw, jnp.zeros_like(qw)),
                              jnp.where(lane >= HEAD_DIM, qw, jnp.zeros_like(qw))], axis=0)
        s = _dot_nt(q2, k) * (HEAD_DIM ** -0.5) - bias
        s = jnp.where(valid, s, NEG_BIG)
        m = jnp.maximum(jnp.max(s, axis=-1, keepdims=True), sink)
        p = jnp.exp(s - m)
        denom = jnp.sum(p, axis=-1, keepdims=True) + jnp.exp(sink - m)
        o = jnp.dot((p / denom).astype(BF16), v, preferred_element_type=F32)
        o_ref[wi * w:(wi + 1) * w, :] = jnp.where(lane < HEAD_DIM, o[:w], o[w:]).astype(o_ref.dtype)


def _swa_attention(proj, col0, slopes, sinks, batch, seq, n_pairs, n_kv):
    n_win = min(SWA_WINDOWS_PER_STEP, seq // WINDOW)
    blk = n_win * WINDOW
    nb = seq // blk
    wins = seq // WINDOW

    def q_map(b, p, n):
        return (b * nb + n, p)

    def q_in_map(b, p, n):
        return (b * nb + n, col0 + p)

    def cur_map(base):
        return lambda b, p, n: (b * nb + n, col0 + base + p // 2)

    def prev_map(base):
        return lambda b, p, n: (b * wins + jnp.maximum(n * n_win - 1, 0), col0 + base + p // 2)

    smem = pl.BlockSpec(memory_space=pltpu.SMEM)
    return pl.pallas_call(
        _swa_body,
        grid=(batch, n_pairs, nb),
        in_specs=[
            smem, smem,
            pl.BlockSpec((blk, LANES), q_in_map),
            pl.BlockSpec((WINDOW, LANES), prev_map(n_pairs)),
            pl.BlockSpec((blk, LANES), cur_map(n_pairs)),
            pl.BlockSpec((WINDOW, LANES), prev_map(n_pairs + n_kv)),
            pl.BlockSpec((blk, LANES), cur_map(n_pairs + n_kv)),
        ],
        out_specs=pl.BlockSpec((blk, LANES), q_map),
        out_shape=jax.ShapeDtypeStruct((batch * seq, n_pairs * LANES), BF16),
        compiler_params=_cparams("parallel", "parallel", "parallel"),
        name="swa_attention",
    )(slopes, sinks, proj, proj, proj, proj, proj)


def _out_proj_body(a1_ref, a2_ref, w_ref, x_ref, o_ref):
    a = jnp.concatenate([a1_ref[...], a2_ref[...]], axis=1)
    o_ref[...] = x_ref[...] + jnp.dot(a, w_ref[...], preferred_element_type=F32)


def _out_proj(a1, a2, w, x2d, bm, bn):
    m, d = x2d.shape
    k1, k2 = a1.shape[1], a2.shape[1]
    bm, bn = min(bm, m), min(bn, d)
    return pl.pallas_call(
        _out_proj_body,
        grid=(m // bm, d // bn),
        in_specs=[
            pl.BlockSpec((bm, k1), lambda i, j: (i, 0)),
            pl.BlockSpec((bm, k2), lambda i, j: (i, 0)),
            pl.BlockSpec((k1 + k2, bn), lambda i, j: (0, j)),
            pl.BlockSpec((bm, bn), lambda i, j: (i, j)),
        ],
        out_specs=pl.BlockSpec((bm, bn), lambda i, j: (i, j)),
        out_shape=jax.ShapeDtypeStruct((m, d), F32),
        compiler_params=_cparams("parallel", "parallel"),
        name="out_proj",
    )(a1, a2, w, x2d)


def _cross_body(x_ref, g_ref, wq_ref, kv_ref, wo_ref, o_ref):
    x = x_ref[...]
    h = _rmsnorm(x, g_ref[...]).astype(BF16)
    q = jnp.dot(h, wq_ref[...], preferred_element_type=F32).astype(BF16)
    cw = wq_ref.shape[1]
    hd = cw // N_CROSS_HEADS
    outs = []
    for hh in range(N_CROSS_HEADS):
        qh = q[:, hh * hd:(hh + 1) * hd]
        kh = kv_ref[:, hh * hd:(hh + 1) * hd]
        vh = kv_ref[:, cw + hh * hd:cw + (hh + 1) * hd]
        s = _dot_nt(qh, kh) * (hd ** -0.5)
        e = jnp.exp(s - jnp.max(s, axis=-1, keepdims=True))
        p = (e / jnp.sum(e, axis=-1, keepdims=True)).astype(BF16)
        outs.append(jnp.dot(p, vh, preferred_element_type=F32))
    o = jnp.concatenate(outs, axis=1).astype(BF16)
    o_ref[...] = x + jnp.dot(o, wo_ref[...], preferred_element_type=F32)


def _cross_block(x2d, g, wq, kv, wo, seq, n_mem, bm):
    m, d = x2d.shape
    bm = min(bm, seq)
    per_b = seq // bm
    cw = wq.shape[1]
    return pl.pallas_call(
        _cross_body,
        grid=(m // bm,),
        in_specs=[
            pl.BlockSpec((bm, d), lambda i: (i, 0)),
            pl.BlockSpec((1, d), lambda i: (0, 0)),
            pl.BlockSpec((d, cw), lambda i: (0, 0)),
            pl.BlockSpec((n_mem, 2 * cw), lambda i: (i // per_b, 0)),
            pl.BlockSpec((cw, d), lambda i: (0, 0)),
        ],
        out_specs=pl.BlockSpec((bm, d), lambda i: (i, 0)),
        out_shape=jax.ShapeDtypeStruct((m, d), F32),
        compiler_params=_cparams("parallel"),
        name="cross_block",
    )(x2d, g.reshape(1, d), wq, kv, wo)


def _router_body(x_ref, g_ref, w_ref, b_ref, idx_ref, gate_ref, rank_ref, cnt_ref, h3_ref, hn_ref, carry_ref):
    @pl.when(pl.program_id(0) == 0)
    def _():
        carry_ref[...] = jnp.zeros_like(carry_ref)

    bm = x_ref.shape[0]
    h = _rmsnorm(x_ref[...], g_ref[...]).astype(BF16)
    bits = lax.bitcast_convert_type(h.astype(F32), jnp.int32)
    hw = bits.shape[1] // 2
    hn_ref[...] = lax.shift_right_logical(bits[:, :hw], 16) | (bits[:, hw:] & jnp.int32(-65536))
    h3_ref[...] = hn_ref[...].reshape(h3_ref.shape)
    logits = jnp.dot(h, w_ref[...], preferred_element_type=F32) + b_ref[...]
    lane = lax.broadcasted_iota(jnp.int32, (bm, LANES), 1)
    lane_f = lane.astype(F32)
    work = logits
    self32 = jnp.zeros((bm, LANES), F32)
    vals, picks, hots = [], [], []
    for _ in range(TOP_K):
        mk = jnp.max(work, axis=-1, keepdims=True)
        pick = jnp.min(jnp.where(work == mk, lane_f, float(LANES)), axis=-1, keepdims=True)
        hot = lane_f == pick
        vals.append(mk)
        picks.append(pick.astype(jnp.int32))
        hots.append(hot)
        self32 = jnp.where(hot, 1.0, self32)
        work = jnp.where(hot, -jnp.inf, work)
    es = [jnp.exp(v - vals[0]) for v in vals]
    denom = es[0] + es[1] + es[2] + es[3]

    r = lax.broadcasted_iota(jnp.int32, (bm, bm), 0)
    c = lax.broadcasted_iota(jnp.int32, (bm, bm), 1)
    tri = (c < r).astype(BF16)
    rank = jnp.dot(tri, self32.astype(BF16), preferred_element_type=F32) + carry_ref[0:1, :]
    carry_new = carry_ref[0:1, :] + jnp.sum(self32, axis=0, keepdims=True)
    carry_ref[...] = jnp.broadcast_to(carry_new, carry_ref.shape)

    idx_o = jnp.zeros((bm, LANES), jnp.int32)
    gate_o = jnp.zeros((bm, LANES), F32)
    rank_o = jnp.zeros((bm, LANES), jnp.int32)
    for kk in range(TOP_K):
        rk = jnp.sum(jnp.where(hots[kk], rank, 0.0), axis=-1, keepdims=True).astype(jnp.int32)
        idx_o = jnp.where(lane == kk, picks[kk], idx_o)
        gate_o = jnp.where(lane == kk, es[kk] / denom, gate_o)
        rank_o = jnp.where(lane == kk, rk, rank_o)
    idx_ref[...] = idx_o
    gate_ref[...] = gate_o
    rank_ref[...] = rank_o
    cnt_ref[...] = jnp.broadcast_to(carry_new, cnt_ref.shape).astype(jnp.int32)


def _router(x2d, g, w_pad, b_pad, bm):
    m, d = x2d.shape
    bm = min(bm, m)
    tok = pl.BlockSpec((bm, LANES), lambda i: (i, 0))
    return pl.pallas_call(
        _router_body,
        grid=(m // bm,),
        in_specs=[
            pl.BlockSpec((bm, d), lambda i: (i, 0)),
            pl.BlockSpec((1, d), lambda i: (0, 0)),
            pl.BlockSpec((d, LANES), lambda i: (0, 0)),
            pl.BlockSpec((1, LANES), lambda i: (0, 0)),
        ],
        out_specs=[tok, tok, tok, pl.BlockSpec((8, LANES), lambda i: (0, 0)),
                   pl.BlockSpec((bm, 1, d // 2), lambda i: (i, 0, 0))],
        out_shape=[
            jax.ShapeDtypeStruct((m, LANES), jnp.int32),
            jax.ShapeDtypeStruct((m, LANES), F32),
            jax.ShapeDtypeStruct((m, LANES), jnp.int32),
            jax.ShapeDtypeStruct((8, LANES), jnp.int32),
            jax.ShapeDtypeStruct((m, 1, d // 2), jnp.int32),
        ],
        scratch_shapes=[pltpu.VMEM((bm, d // 2), jnp.int32), pltpu.VMEM((8, LANES), F32)],
        compiler_params=_cparams("arbitrary"),
        name="router",
    )(x2d, g.reshape(1, d), w_pad, b_pad)


def _sorted_row(idx_ref, rank_ref, pend_ref, padded_ref, a):
    e = idx_ref[a]
    return pend_ref[e] - padded_ref[e] + rank_ref[a]


def _dispatch_body(idx_ref, rank_ref, pend_ref, padded_ref, x_ref, xs_ref, z_ref, sem, zsem):
    i = pl.program_id(0)
    bm = x_ref.shape[0]
    n_exp = pend_ref.shape[0]
    tm = z_ref.shape[0]

    def zero_copy(e):
        return pltpu.make_async_copy(z_ref, xs_ref.at[pl.ds(pend_ref[e] - tm, tm)], zsem)

    @pl.when(i == 0)
    def _():
        z_ref[...] = jnp.zeros_like(z_ref)
        for e in range(n_exp):
            @pl.when(padded_ref[e] > 0)
            def _():
                zero_copy(e).start()
        for e in range(n_exp):
            @pl.when(padded_ref[e] > 0)
            def _():
                zero_copy(e).wait()

    def row_copy(r, kk):
        d = _sorted_row(idx_ref, rank_ref, pend_ref, padded_ref, (i * bm + r) * TOP_K + kk)
        return pltpu.make_async_copy(x_ref.at[r], xs_ref.at[d], sem)

    def issue(r):
        for kk in range(TOP_K):
            row_copy(r, kk).start(priority=kk % 2)

    def drain(r):
        for kk in range(TOP_K):
            row_copy(r, kk).wait()

    _for_rows(bm, issue)
    _for_rows(bm, drain)


def _dispatch(idx_flat, rank_flat, pend, padded, x3d, n_rows, bm):
    m, _, d = x3d.shape
    bm = min(bm, m)
    grid_spec = pltpu.PrefetchScalarGridSpec(
        num_scalar_prefetch=4,
        grid=(m // bm,),
        in_specs=[pl.BlockSpec((bm, 1, d), lambda i, *_: (i, 0, 0))],
        out_specs=pl.BlockSpec(memory_space=pl.ANY),
        scratch_shapes=[
            pltpu.VMEM((MOE_ROW_TILE, 1, d), x3d.dtype),
            pltpu.SemaphoreType.DMA,
            pltpu.SemaphoreType.DMA,
        ],
    )
    return pl.pallas_call(
        _dispatch_body,
        grid_spec=grid_spec,
        out_shape=jax.ShapeDtypeStruct((n_rows, 1, d), x3d.dtype),
        compiler_params=_cparams("arbitrary"),
        name="moe_dispatch",
    )(idx_flat, rank_flat, pend, padded, x3d)


def _moe_body(ie_ref, istart_ref, insub_ref, xs_ref, w1_ref, b1_ref, w2_ref, b2_ref,
              ys_ref, xbf_ref, yacc_ref, st0_ref, st1_ref, ld0_ref, ld1_ref, land_ref,
              w1b_ref, w2m_ref, w2b_ref, sems):
    it = pl.program_id(0)
    c = pl.program_id(1)
    n_chunks = pl.num_programs(1)
    nsub = insub_ref[it]
    start = istart_ref[it]
    tm = st0_ref.shape[0]
    tc = w1_ref.shape[2]
    half = tc // 2
    quarter = tc // 4
    even_lane = lax.broadcasted_iota(jnp.int32, (1, half), 1) % 2 == 0
    stage = (st0_ref, st1_ref)
    load_stage = (ld0_ref, ld1_ref)
    n_pairs = (nsub + 1) // 2

    def rows(s):
        return pl.ds(pl.multiple_of(s * tm, tm), tm)

    def load_copy(s, k):
        return pltpu.make_async_copy(xs_ref.at[pl.ds(start + s * tm, tm)], load_stage[k], sems.at[k])

    def store_copy(s, k):
        return pltpu.make_async_copy(stage[k], ys_ref.at[pl.ds(start + s * tm, tm)], sems.at[k])

    def consume(s, k):
        land_ref[...] = load_stage[k][...].reshape(land_ref.shape)
        word = land_ref[...]
        hw = word.shape[1]
        lo = lax.bitcast_convert_type(lax.shift_left(word, 16), F32)
        hi = lax.bitcast_convert_type(word & jnp.int32(-65536), F32)
        xbf_ref[rows(s), :hw] = lo.astype(BF16)
        xbf_ref[rows(s), hw:] = hi.astype(BF16)
        yacc_ref[rows(s), :] = jnp.zeros((tm, yacc_ref.shape[1]), F32)

    def hidden(rs):
        return jnp.dot(xbf_ref[rs, :], w1b_ref[...], preferred_element_type=F32) + b1_ref[0]

    def accumulated(rs, u):
        glu = jnp.minimum(u, SWIGLU_LIMIT)
        act = glu * (1.0 / (1.0 + jnp.exp(-SWIGLU_ALPHA * glu)))
        lin1 = jnp.clip(u, -SWIGLU_LIMIT, SWIGLU_LIMIT) + 1.0
        a_even = act * pltpu.roll(lin1, tc - 1, 1)
        merged = jnp.where(even_lane, a_even[:, :half], pltpu.roll(a_even[:, half:], 1, 1))
        return yacc_ref[rs, :] + jnp.dot(merged.astype(BF16), w2b_ref[...], preferred_element_type=F32)

    def emit(s, k, wait_for):
        if wait_for is not None:
            store_copy(wait_for, k).wait()
        stage[k][...] = yacc_ref[rows(s), :].reshape(stage[k].shape)
        store_copy(s, k).start()

    @pl.when(nsub > 0)
    def _():
        @pl.when(c == 0)
        def _():
            load_copy(0, 0).start()

            def load_pair(sp, carry):
                s0 = 2 * sp
                s1 = s0 + 1

                @pl.when(s1 < nsub)
                def _():
                    load_copy(s1, 1).start()
                load_copy(s0, 0).wait()
                consume(s0, 0)

                @pl.when(s1 < nsub)
                def _():
                    @pl.when(s1 + 1 < nsub)
                    def _():
                        load_copy(s1 + 1, 0).start()
                    load_copy(s1, 1).wait()
                    consume(s1, 1)
                return carry
            lax.fori_loop(0, n_pairs, load_pair, 0)

        w1b_ref[...] = w1_ref[0].astype(BF16)
        for j in range(w2m_ref.shape[0]):
            cols = slice(j * LANES, (j + 1) * LANES)
            w2m_ref[j, pl.ds(0, quarter, stride=2), :] = w2_ref[0, 0:quarter, cols]
            w2m_ref[j, pl.ds(1, quarter, stride=2), :] = w2_ref[0, quarter:2 * quarter, cols]
            w2b_ref[:, cols] = w2m_ref[j].astype(BF16)

        def group_update(first, n, bias):
            rs = [rows(first + k) for k in range(n)]
            us = [hidden(r) for r in rs]
            for k in range(n):
                yacc_ref[rs[k], :] = accumulated(rs[k], us[k]) + bias

        def single_update(s, bias):
            group_update(s, 1, bias)

        @pl.when(c < n_chunks - 1)
        def _():
            n_groups = nsub // MOE_GROUP

            def group(gi, carry):
                group_update(gi * MOE_GROUP, MOE_GROUP, 0.0)
                return carry
            lax.fori_loop(0, n_groups, group, 0)

            def single(s, carry):
                single_update(s, 0.0)
                return carry
            lax.fori_loop(n_groups * MOE_GROUP, nsub, single, 0)

        @pl.when(c == n_chunks - 1)
        def _():
            sg = len(stage)
            n_groups = nsub // sg

            @pl.when(n_groups > 0)
            def _():
                group_update(0, sg, b2_ref[0])
                for k in range(sg):
                    emit(k, k, None)

                def group(gi, carry):
                    first = gi * sg
                    group_update(first, sg, b2_ref[0])
                    for k in range(sg):
                        emit(first + k, k, first + k - sg)
                    return carry
                lax.fori_loop(1, n_groups, group, 0)

            for r in range(sg - 1):
                s = n_groups * sg + r

                @pl.when(s < nsub)
                def _():
                    single_update(s, b2_ref[0])

                    @pl.when(s >= sg)
                    def _():
                        store_copy(s - sg, r).wait()
                    emit(s, r, None)

            for k in range(sg):
                @pl.when(nsub > k)
                def _():
                    store_copy(nsub - 1 - (nsub - 1 - k) % sg, k).wait()


def _moe_mlp(item_e, item_start, item_nsub, xs, w1, b1, w2, b2):
    n_rows, _, dw = xs.shape
    n_exp, d, two_i = w1.shape
    tc = min(MOE_COL_CHUNK, two_i)
    n_chunks = two_i // tc
    n_items = item_e.shape[0]
    last = n_chunks - 1

    def chunk(c, it, insub):
        return jnp.where(insub[it] > 0, c, last)

    grid_spec = pltpu.PrefetchScalarGridSpec(
        num_scalar_prefetch=3,
        grid=(n_items, n_chunks),
        in_specs=[
            pl.BlockSpec(memory_space=pl.ANY),
            pl.BlockSpec((1, d, tc), lambda it, c, ie, ist, ins: (ie[it], 0, chunk(c, it, ins))),
            pl.BlockSpec((1, 1, tc), lambda it, c, ie, ist, ins: (ie[it], 0, chunk(c, it, ins))),
            pl.BlockSpec((1, tc // 2, d), lambda it, c, ie, ist, ins: (ie[it], chunk(c, it, ins), 0)),
            pl.BlockSpec((1, 1, d), lambda it, c, ie, ist, ins: (ie[it], 0, 0)),
        ],
        out_specs=pl.BlockSpec(memory_space=pl.ANY),
        scratch_shapes=[
            pltpu.VMEM((MOE_ROWS_PER_ITEM, d), BF16),
            pltpu.VMEM((MOE_ROWS_PER_ITEM, d), F32),
            pltpu.VMEM((MOE_ROW_TILE, 1, d), F32),
            pltpu.VMEM((MOE_ROW_TILE, 1, d), F32),
            pltpu.VMEM((MOE_ROW_TILE, 1, dw), jnp.int32),
            pltpu.VMEM((MOE_ROW_TILE, 1, dw), jnp.int32),
            pltpu.VMEM((MOE_ROW_TILE, dw), jnp.int32),
            pltpu.VMEM((d, tc), BF16),
            pltpu.VMEM((d // LANES, tc // 2, LANES), F32),
            pltpu.VMEM((tc // 2, d), BF16),
            pltpu.SemaphoreType.DMA((2,)),
        ],
    )
    return pl.pallas_call(
        _moe_body,
        grid_spec=grid_spec,
        out_shape=jax.ShapeDtypeStruct((n_rows, 1, d), F32),
        compiler_params=_cparams("arbitrary", "arbitrary"),
        name="moe_mlp",
    )(item_e, item_start, item_nsub, xs, w1, b1, w2, b2)


def _combine_body(idx_ref, rank_ref, pend_ref, padded_ref, x_ref, gate_ref, g_ref, ys_ref, o_ref, b0, b1, b2, b3, flat_ref, sem, *, final_norm):
    i = pl.program_id(0)
    bm = x_ref.shape[0]
    bufs = (b0, b1, b2, b3)

    def row_copy(r, kk):
        d = _sorted_row(idx_ref, rank_ref, pend_ref, padded_ref, (i * bm + r) * TOP_K + kk)
        return pltpu.make_async_copy(ys_ref.at[d], bufs[kk].at[r], sem)

    def issue(r):
        for kk in range(TOP_K):
            row_copy(r, kk).start(priority=kk % 2)

    def drain(r):
        for kk in range(TOP_K):
            row_copy(r, kk).wait()

    _for_rows(bm, issue)
    _for_rows(bm, drain)
    acc = x_ref[...]
    for kk in range(TOP_K):
        flat_ref[...] = bufs[kk][...].reshape(flat_ref.shape)
        acc = acc + gate_ref[:, kk:kk + 1] * flat_ref[...]
    o_ref[...] = _rmsnorm(acc, g_ref[...]) if final_norm else acc


def _combine(idx_flat, rank_flat, pend, padded, x2d, gate4, g, ys, bm, final_norm):
    m, d = x2d.shape
    bm = min(bm, m)
    grid_spec = pltpu.PrefetchScalarGridSpec(
        num_scalar_prefetch=4,
        grid=(m // bm,),
        in_specs=[
            pl.BlockSpec((bm, d), lambda i, *_: (i, 0)),
            pl.BlockSpec((bm, LANES), lambda i, *_: (i, 0)),
            pl.BlockSpec((1, d), lambda i, *_: (0, 0)),
            pl.BlockSpec(memory_space=pl.ANY),
        ],
        out_specs=pl.BlockSpec((bm, d), lambda i, *_: (i, 0)),
        scratch_shapes=[pltpu.VMEM((bm, 1, d), F32)] * TOP_K + [pltpu.VMEM((bm, d), F32), pltpu.SemaphoreType.DMA],
    )
    return pl.pallas_call(
        functools.partial(_combine_body, final_norm=final_norm),
        grid_spec=grid_spec,
        out_shape=jax.ShapeDtypeStruct((m, d), F32),
        compiler_params=_cparams("arbitrary"),
        name="moe_combine",
    )(idx_flat, rank_flat, pend, padded, x2d, gate4, g.reshape(1, d), ys)


def _routing_tables(counts, n_assign, n_exp):
    tm, rpi = MOE_ROW_TILE, MOE_ROWS_PER_ITEM
    padded = (counts + tm - 1) // tm * tm
    pend = jnp.cumsum(padded)
    pstart = pend - padded

    n_rows = n_assign + n_exp * tm
    n_items = n_rows // rpi + n_exp
    per_e = (padded + rpi - 1) // rpi
    icum = jnp.cumsum(per_e)
    ioff = icum - per_e
    total = icum[-1]
    ii = jnp.arange(n_items, dtype=jnp.int32)
    ic = jnp.minimum(ii, total - 1)
    e_of = jnp.sum((icum[None, :] <= ic[:, None]).astype(jnp.int32), axis=1)
    e_of = jnp.minimum(e_of, n_exp - 1)
    local = ii - ioff[e_of]
    valid = ii < total
    start = jnp.where(valid, pstart[e_of] + local * rpi, 0).astype(jnp.int32)
    nsub = jnp.where(valid, jnp.clip((padded[e_of] - local * rpi) // tm, 0, rpi // tm), 0).astype(jnp.int32)
    return pend.astype(jnp.int32), padded.astype(jnp.int32), e_of, start, nsub, n_rows


def _alibi_slopes(n):
    start = 2.0 ** (-8.0 / n)
    return np.array([start ** (i + 1) for i in range(n)], dtype=np.float32)


def kernel(x, mem, g_mix, w_in, b_forget, swa_sinks, w_out, g_cross, g_mem, w_cq, w_ckv, w_co, g_moe,
           w_router, b_router, w_mlp1, b_mlp1, w_mlp2, b_mlp2, g_final):
    batch, seq, d = x.shape
    n_mem = mem.shape[1]
    depth = w_in.shape[0]
    n_fox = d // (2 * HEAD_DIM)
    n_swa = d // (2 * HEAD_DIM)
    n_kv = n_swa // SWA_GROUP
    fox_w = n_fox * HEAD_DIM
    swa_q_w = n_swa * HEAD_DIM
    swa_kv_w = n_kv * HEAD_DIM
    o2 = 2 * fox_w
    o3 = 3 * fox_w
    o4 = o3 + n_fox
    o5 = o4 + swa_q_w
    o6 = o5 + swa_kv_w
    n_exp = w_router.shape[2]
    t = batch * seq
    slopes = jnp.asarray(_alibi_slopes(n_swa))

    def dup_heads(wk):
        wk = wk.reshape(d, n_kv, 1, HEAD_DIM)
        return jnp.broadcast_to(wk, (d, n_kv, 2, HEAD_DIM)).reshape(d, n_kv * LANES)

    x2d = x.reshape(t, d)
    for l in range(depth):
        wl = w_in[l]
        w_qk = wl[:, :o2].astype(BF16)
        w_vt = wl[:, o2:o3].T.astype(BF16)
        w_ff = jnp.zeros((d, LANES), BF16).at[:, :n_fox].set(wl[:, o3:o4].astype(BF16))
        b_ff = jnp.zeros((1, LANES), F32).at[0, :n_fox].set(b_forget[l])
        w_swa = jnp.concatenate(
            [wl[:, o4:o5], dup_heads(wl[:, o5:o6]), dup_heads(wl[:, o6:])], axis=1).astype(BF16)

        proj = _norm_matmul(x2d, g_mix[l], jnp.concatenate([w_qk, w_swa], axis=1), BF16, 1024, fox_w,
                            "qk_swa_proj", lead_tiles=1, lead_scale=HEAD_DIM ** -0.5 * LOG2E)
        vt = _norm_matmul(x2d, g_mix[l], w_vt, BF16, 1024, 1024, "fox_vt_proj", transposed=True)
        c = _fox_gate(x2d.reshape(batch, seq, d), g_mix[l], w_ff, b_ff, 512)
        fox = _fox_attention(proj, vt, c, batch, seq, fox_w // LANES, FOX_TILE)
        swa = _swa_attention(proj, o2 // LANES, slopes, swa_sinks[l], batch, seq, swa_q_w // LANES, n_kv)
        x2d = _out_proj(fox, swa, w_out[l].astype(BF16), x2d, 1024, 1024)

        kv = _norm_matmul(mem.reshape(batch * n_mem, d), g_mem[l], w_ckv[l].astype(BF16), BF16, 1024, 1024,
                          "mem_kv_proj")
        x2d = _cross_block(x2d, g_cross[l], w_cq[l].astype(BF16), kv, w_co[l].astype(BF16), seq, n_mem, 512)

        w_r = jnp.zeros((d, LANES), BF16).at[:, :n_exp].set(w_router[l].astype(BF16))
        b_r = jnp.full((1, LANES), NEG_BIG, F32).at[0, :n_exp].set(b_router[l])
        idx4, gate4, rank4, cnt, h3d = _router(x2d, g_moe[l], w_r, b_r, 512)
        pend, padded, item_e, item_start, item_nsub, n_rows = _routing_tables(cnt[0, :n_exp], t * TOP_K, n_exp)
        idx_flat = idx4[:, :TOP_K].reshape(-1)
        rank_flat = rank4[:, :TOP_K].reshape(-1)
        xs = _dispatch(idx_flat, rank_flat, pend, padded, h3d, n_rows, ROW_COPY_TOKENS)
        two_i = w_mlp1.shape[3]
        ys = _moe_mlp(item_e, item_start, item_nsub, xs, w_mlp1[l], b_mlp1[l].reshape(n_exp, 1, two_i),
                      w_mlp2[l], b_mlp2[l].reshape(n_exp, 1, d))
        x2d = _combine(idx_flat, rank_flat, pend, padded, x2d, gate4, g_final, ys, ROW_COPY_TOKENS,
                       final_norm=(l == depth - 1))
    return x2d.reshape(batch, seq, d)
```

```python
import functools

import numpy as np
import jax
import jax.numpy as jnp
from jax import lax
from jax.experimental import pallas as pl
from jax.experimental.pallas import tpu as pltpu

F32 = jnp.float32
BF16 = jnp.bfloat16

RMS_EPS = 1e-5
HEAD_DIM = 64
WINDOW = 128
SWA_GROUP = 4
N_CROSS_HEADS = 4
TOP_K = 4
SWIGLU_ALPHA = 1.702
SWIGLU_LIMIT = 7.0

LANES = 128
MXU_DIM = 256
VMEM_LIMIT_BYTES = 56 * 2**20
NEG_BIG = -1e30
LOG2E = 1.4426950408889634

FOX_TILE = 1024
SWA_WINDOWS_PER_STEP = 8
ROW_COPY_UNROLL = 8
ROW_COPY_TOKENS = 512
MOE_ROW_TILE = 256
MOE_ROWS_PER_ITEM = 2304
MOE_COL_CHUNK = 512
MOE_GROUP = 4


def _cparams(*sem):
    return pltpu.CompilerParams(dimension_semantics=sem, vmem_limit_bytes=VMEM_LIMIT_BYTES)


def _rmsnorm(x, g):
    ms = jnp.mean(x * x, axis=-1, keepdims=True)
    return x * lax.rsqrt(ms + RMS_EPS) * g


def _for_rows(n, fn):
    def trip(o, carry):
        for u in range(ROW_COPY_UNROLL):
            fn(o * ROW_COPY_UNROLL + u)
        return carry
    lax.fori_loop(0, n // ROW_COPY_UNROLL, trip, 0)


def _dot_nt(a, b):
    return lax.dot_general(a, b, (((1,), (1,)), ((), ())), preferred_element_type=F32)


def _norm_matmul_body(x_ref, g_ref, w_ref, o_ref, h_ref, *, transposed, lead_tiles, lead_scale):
    @pl.when(pl.program_id(1) == 0)
    def _():
        h_ref[...] = _rmsnorm(x_ref[...], g_ref[...]).astype(h_ref.dtype)

    if transposed:
        o_ref[...] = _dot_nt(w_ref[...], h_ref[...]).astype(o_ref.dtype)
    else:
        acc = jnp.dot(h_ref[...], w_ref[...], preferred_element_type=F32)
        if lead_tiles:
            acc = acc * jnp.where(pl.program_id(1) < lead_tiles, lead_scale, 1.0)
        o_ref[...] = acc.astype(o_ref.dtype)


def _norm_matmul(x2d, g, w, out_dtype, bm, bn, name, transposed=False, lead_tiles=0, lead_scale=1.0):
    m, d = x2d.shape
    n = w.shape[0] if transposed else w.shape[1]
    bm, bn = min(bm, m), min(bn, n)
    if transposed:
        w_spec = pl.BlockSpec((bn, d), lambda i, j: (j, 0))
        o_spec = pl.BlockSpec((bn, bm), lambda i, j: (j, i))
        o_shape = (n, m)
    else:
        w_spec = pl.BlockSpec((d, bn), lambda i, j: (0, j))
        o_spec = pl.BlockSpec((bm, bn), lambda i, j: (i, j))
        o_shape = (m, n)
    return pl.pallas_call(
        functools.partial(_norm_matmul_body, transposed=transposed, lead_tiles=lead_tiles, lead_scale=lead_scale),
        grid=(m // bm, n // bn),
        in_specs=[pl.BlockSpec((bm, d), lambda i, j: (i, 0)), pl.BlockSpec((1, d), lambda i, j: (0, 0)), w_spec],
        out_specs=o_spec,
        out_shape=jax.ShapeDtypeStruct(o_shape, out_dtype),
        scratch_shapes=[pltpu.VMEM((bm, d), BF16)],
        compiler_params=_cparams("parallel", "arbitrary"),
        name=name,
    )(x2d, g.reshape(1, d), w)


def _fox_gate_body(x_ref, g_ref, w_ref, b_ref, c_ref, carry_ref):
    @pl.when(pl.program_id(1) == 0)
    def _():
        carry_ref[...] = jnp.zeros_like(carry_ref)

    bs = x_ref.shape[1]
    h = _rmsnorm(x_ref[0], g_ref[...]).astype(BF16)
    z = jnp.dot(h, w_ref[...], preferred_element_type=F32) + b_ref[...]
    logf = jnp.minimum(z, 0.0) - jnp.log1p(jnp.exp(-jnp.abs(z)))
    r = lax.broadcasted_iota(jnp.int32, (bs, bs), 0)
    c = lax.broadcasted_iota(jnp.int32, (bs, bs), 1)
    tri = (c <= r).astype(F32)
    cs = jnp.dot(tri, logf, preferred_element_type=F32, precision=lax.Precision.HIGHEST)
    cs = cs + carry_ref[0:1, :]
    c_ref[0] = cs * LOG2E
    carry_ref[...] = jnp.broadcast_to(cs[bs - 1:bs, :], carry_ref.shape)


def _fox_gate(x, g, w_pad, b_pad, bs):
    b, s, d = x.shape
    bs = min(bs, s)
    return pl.pallas_call(
        _fox_gate_body,
        grid=(b, s // bs),
        in_specs=[
            pl.BlockSpec((1, bs, d), lambda i, j: (i, j, 0)),
            pl.BlockSpec((1, d), lambda i, j: (0, 0)),
            pl.BlockSpec((d, LANES), lambda i, j: (0, 0)),
            pl.BlockSpec((1, LANES), lambda i, j: (0, 0)),
        ],
        out_specs=pl.BlockSpec((1, bs, LANES), lambda i, j: (i, j, 0)),
        out_shape=jax.ShapeDtypeStruct((b, s, LANES), F32),
        scratch_shapes=[pltpu.VMEM((8, LANES), F32)],
        compiler_params=_cparams("parallel", "arbitrary"),
        name="fox_gate",
    )(x, g.reshape(1, d), w_pad, b_pad)


def _fox_attn_body(itab_ref, jtab_ref, q_ref, k_ref, vt_ref, c_ref, o_ref, m_ref, l_ref, acc_ref):
    p = pl.program_id(1)
    st = pl.program_id(2)
    i = itab_ref[st]
    j = jtab_ref[st]
    tq = q_ref.shape[0]
    tk = k_ref.shape[0]
    lane = lax.broadcasted_iota(jnp.int32, (1, LANES), 1)

    @pl.when(j == 0)
    def _():
        m_ref[...] = jnp.full_like(m_ref, NEG_BIG)
        l_ref[...] = jnp.zeros_like(l_ref)
        acc_ref[...] = jnp.zeros_like(acc_ref)

    def step(masked):
        q = q_ref[...]
        k = k_ref[...]
        vt = vt_ref[...]
        cblk = c_ref[0]
        if masked:
            key = lax.broadcasted_iota(jnp.int32, (tk, tq), 0)
            qry = lax.broadcasted_iota(jnp.int32, (tk, tq), 1)
            causal = key <= qry
        scores = []
        for hh in range(2):
            qh = jnp.where(lane // HEAD_DIM == hh, q, jnp.zeros_like(q))
            scores.append(_dot_nt(k, qh))
        n_split = 2 if tq % (2 * MXU_DIM) == 0 else 1
        wq = tq // n_split
        for hh in range(2):
            c_col = jnp.sum(jnp.where(lane == 2 * p + hh, cblk, 0.0), axis=1, keepdims=True)
            for g in range(n_split):
                cols = slice(g * wq, (g + 1) * wq)
                s = scores[hh][:, cols] - c_col
                if masked:
                    s = jnp.where(causal[:, cols], s, NEG_BIG)
                m_prev = m_ref[hh, :, cols]
                m_new = jnp.maximum(m_prev, jnp.max(s, axis=0, keepdims=True))
                alpha = jnp.exp2(m_prev - m_new)
                pt = jnp.exp2(s - m_new)
                l_ref[hh, :, cols] = alpha * l_ref[hh, :, cols] + jnp.sum(pt, axis=0, keepdims=True)
                acc_ref[hh, :, cols] = alpha * acc_ref[hh, :, cols] + jnp.dot(
                    vt, pt.astype(BF16), preferred_element_type=F32)
                m_ref[hh, :, cols] = m_new

    @pl.when(j < i)
    def _():
        step(False)

    @pl.when(j == i)
    def _():
        step(True)
        row = lax.broadcasted_iota(jnp.int32, (LANES, 1), 0)
        ot = jnp.where(row < HEAD_DIM, acc_ref[0] / l_ref[0], acc_ref[1] / l_ref[1])
        o_ref[...] = ot.T.astype(o_ref.dtype)


def _fox_attention(qk, vt, c, batch, seq, n_pairs, tile):
    tile = min(tile, seq)
    nq = seq // tile
    pairs = [(i, j) for i in range(nq) for j in range(i + 1)]
    itab = jnp.asarray([ij[0] for ij in pairs], jnp.int32)
    jtab = jnp.asarray([ij[1] for ij in pairs], jnp.int32)
    grid_spec = pltpu.PrefetchScalarGridSpec(
        num_scalar_prefetch=2,
        grid=(batch, n_pairs, len(pairs)),
        in_specs=[
            pl.BlockSpec((tile, LANES), lambda b, p, s, it, jt: (b * nq + it[s], p)),
            pl.BlockSpec((tile, LANES), lambda b, p, s, it, jt: (b * nq + jt[s], n_pairs + p)),
            pl.BlockSpec((LANES, tile), lambda b, p, s, it, jt: (p, b * nq + jt[s])),
            pl.BlockSpec((1, tile, LANES), lambda b, p, s, it, jt: (b, jt[s], 0)),
        ],
        out_specs=pl.BlockSpec((tile, LANES), lambda b, p, s, it, jt: (b * nq + it[s], p)),
        scratch_shapes=[
            pltpu.VMEM((2, 1, tile), F32),
            pltpu.VMEM((2, 1, tile), F32),
            pltpu.VMEM((2, LANES, tile), F32),
        ],
    )
    return pl.pallas_call(
        _fox_attn_body,
        grid_spec=grid_spec,
        out_shape=jax.ShapeDtypeStruct((batch * seq, n_pairs * LANES), BF16),
        compiler_params=_cparams("parallel", "parallel", "arbitrary"),
        name="fox_attention",
    )(itab, jtab, qk, qk, vt, c)


def _swa_body(slopes_ref, sinks_ref, q_ref, kp_ref, kc_ref, vp_ref, vc_ref, o_ref):
    pr = pl.program_id(1)
    n = pl.program_id(2)
    w = WINDOW
    n_win = q_ref.shape[0] // w
    lane = lax.broadcasted_iota(jnp.int32, (1, LANES), 1)
    row = lax.broadcasted_iota(jnp.int32, (2 * w, 1), 0)
    head1 = row >= w
    slope = jnp.where(head1, slopes_ref[2 * pr + 1], slopes_ref[2 * pr])
    sink = jnp.where(head1, sinks_ref[2 * pr + 1], sinks_ref[2 * pr])
    qi = lax.broadcasted_iota(jnp.int32, (2 * w, 2 * w), 0) % w
    kj = lax.broadcasted_iota(jnp.int32, (2 * w, 2 * w), 1)
    rel = qi - kj + w
    in_window = (rel >= 0) & (rel < w)
    bias = slope * rel.astype(F32)
    for wi in range(n_win):
        qw = q_ref[wi * w:(wi + 1) * w, :]
        if wi == 0:
            k = jnp.concatenate([kp_ref[...], kc_ref[0:w, :]], axis=0)
            v = jnp.concatenate([vp_ref[...], vc_ref[0:w, :]], axis=0)
            valid = in_window & (kj >= jnp.where(n > 0, 0, w))
        else:
            k = kc_ref[(wi - 1) * w:(wi + 1) * w, :]
            v = vc_ref[(wi - 1) * w:(wi + 1) * w, :]
            valid = in_window
        q2 = jnp.concatenate([jnp.where(lane < HEAD_DIM, qw, jnp.zeros_like(qw)),
                              jnp.where(lane >= HEAD_DIM, qw, jnp.zeros_like(qw))], axis=0)
        s = _dot_nt(q2, k) * (HEAD_DIM ** -0.5) - bias
        s = jnp.where(valid, s, NEG_BIG)
        m = jnp.maximum(jnp.max(s, axis=-1, keepdims=True), sink)
        p = jnp.exp(s - m)
        denom = jnp.sum(p, axis=-1, keepdims=True) + jnp.exp(sink - m)
        o = jnp.dot((p / denom).astype(BF16), v, preferred_element_type=F32)
        o_ref[wi * w:(wi + 1) * w, :] = jnp.where(lane < HEAD_DIM, o[:w], o[w:]).astype(o_ref.dtype)


def _swa_attention(proj, col0, slopes, sinks, batch, seq, n_pairs, n_kv):
    n_win = min(SWA_WINDOWS_PER_STEP, seq // WINDOW)
    blk = n_win * WINDOW
    nb = seq // blk
    wins = seq // WINDOW

    def q_map(b, p, n):
        return (b * nb + n, p)

    def q_in_map(b, p, n):
        return (b * nb + n, col0 + p)

    def cur_map(base):
        return lambda b, p, n: (b * nb + n, col0 + base + p // 2)

    def prev_map(base):
        return lambda b, p, n: (b * wins + jnp.maximum(n * n_win - 1, 0), col0 + base + p // 2)

    smem = pl.BlockSpec(memory_space=pltpu.SMEM)
    return pl.pallas_call(
        _swa_body,
        grid=(batch, n_pairs, nb),
        in_specs=[
            smem, smem,
            pl.BlockSpec((blk, LANES), q_in_map),
            pl.BlockSpec((WINDOW, LANES), prev_map(n_pairs)),
            pl.BlockSpec((blk, LANES), cur_map(n_pairs)),
            pl.BlockSpec((WINDOW, LANES), prev_map(n_pairs + n_kv)),
            pl.BlockSpec((blk, LANES), cur_map(n_pairs + n_kv)),
        ],
        out_specs=pl.BlockSpec((blk, LANES), q_map),
        out_shape=jax.ShapeDtypeStruct((batch * seq, n_pairs * LANES), BF16),
        compiler_params=_cparams("parallel", "parallel", "parallel"),
        name="swa_attention",
    )(slopes, sinks, proj, proj, proj, proj, proj)


def _out_proj_body(a1_ref, a2_ref, w_ref, x_ref, o_ref):
    a = jnp.concatenate([a1_ref[...], a2_ref[...]], axis=1)
    o_ref[...] = x_ref[...] + jnp.dot(a, w_ref[...], preferred_element_type=F32)


def _out_proj(a1, a2, w, x2d, bm, bn):
    m, d = x2d.shape
    k1, k2 = a1.shape[1], a2.shape[1]
    bm, bn = min(bm, m), min(bn, d)
    return pl.pallas_call(
        _out_proj_body,
        grid=(m // bm, d // bn),
        in_specs=[
            pl.BlockSpec((bm, k1), lambda i, j: (i, 0)),
            pl.BlockSpec((bm, k2), lambda i, j: (i, 0)),
            pl.BlockSpec((k1 + k2, bn), lambda i, j: (0, j)),
            pl.BlockSpec((bm, bn), lambda i, j: (i, j)),
        ],
        out_specs=pl.BlockSpec((bm, bn), lambda i, j: (i, j)),
        out_shape=jax.ShapeDtypeStruct((m, d), F32),
        compiler_params=_cparams("parallel", "parallel"),
        name="out_proj",
    )(a1, a2, w, x2d)


def _cross_body(x_ref, g_ref, wq_ref, kv_ref, wo_ref, o_ref):
    x = x_ref[...]
    h = _rmsnorm(x, g_ref[...]).astype(BF16)
    q = jnp.dot(h, wq_ref[...], preferred_element_type=F32).astype(BF16)
    cw = wq_ref.shape[1]
    hd = cw // N_CROSS_HEADS
    outs = []
    for hh in range(N_CROSS_HEADS):
        qh = q[:, hh * hd:(hh + 1) * hd]
        kh = kv_ref[:, hh * hd:(hh + 1) * hd]
        vh = kv_ref[:, cw + hh * hd:cw + (hh + 1) * hd]
        s = _dot_nt(qh, kh) * (hd ** -0.5)
        e = jnp.exp(s - jnp.max(s, axis=-1, keepdims=True))
        p = (e / jnp.sum(e, axis=-1, keepdims=True)).astype(BF16)
        outs.append(jnp.dot(p, vh, preferred_element_type=F32))
    o = jnp.concatenate(outs, axis=1).astype(BF16)
    o_ref[...] = x + jnp.dot(o, wo_ref[...], preferred_element_type=F32)


def _router_body(x_ref, g_ref, w_ref, b_ref, idx_ref, gate_ref, rank_ref, cnt_ref, h3_ref, hn_ref, carry_ref):
    @pl.when(pl.program_id(0) == 0)
    def _():
        carry_ref[...] = jnp.zeros_like(carry_ref)

    bm = x_ref.shape[0]
    h = _rmsnorm(x_ref[...], g_ref[...]).astype(BF16)
    bits = lax.bitcast_convert_type(h.astype(F32), jnp.int32)
    hw = bits.shape[1] // 2
    hn_ref[...] = lax.shift_right_logical(bits[:, :hw], 16) | (bits[:, hw:] & jnp.int32(-65536))
    h3_ref[...] = hn_ref[...].reshape(h3_ref.shape)
    logits = jnp.dot(h, w_ref[...], preferred_element_type=F32) + b_ref[...]
    lane = lax.broadcasted_iota(jnp.int32, (bm, LANES), 1)
    lane_f = lane.astype(F32)
    work = logits
    self32 = jnp.zeros((bm, LANES), F32)
    vals, picks, hots = [], [], []
    for _ in range(TOP_K):
        mk = jnp.max(work, axis=-1, keepdims=True)
        pick = jnp.min(jnp.where(work == mk, lane_f, float(LANES)), axis=-1, keepdims=True)
        hot = lane_f == pick
        vals.append(mk)
        picks.append(pick.astype(jnp.int32))
        hots.append(hot)
        self32 = jnp.where(hot, 1.0, self32)
        work = jnp.where(hot, -jnp.inf, work)
    es = [jnp.exp(v - vals[0]) for v in vals]
    denom = es[0] + es[1] + es[2] + es[3]

    r = lax.broadcasted_iota(jnp.int32, (bm, bm), 0)
    c = lax.broadcasted_iota(jnp.int32, (bm, bm), 1)
    tri = (c < r).astype(BF16)
    rank = jnp.dot(tri, self32.astype(BF16), preferred_element_type=F32) + carry_ref[0:1, :]
    carry_new = carry_ref[0:1, :] + jnp.sum(self32, axis=0, keepdims=True)
    carry_ref[...] = jnp.broadcast_to(carry_new, carry_ref.shape)

    idx_o = jnp.zeros((bm, LANES), jnp.int32)
    gate_o = jnp.zeros((bm, LANES), F32)
    rank_o = jnp.zeros((bm, LANES), jnp.int32)
    for kk in range(TOP_K):
        rk = jnp.sum(jnp.where(hots[kk], rank, 0.0), axis=-1, keepdims=True).astype(jnp.int32)
        idx_o = jnp.where(lane == kk, picks[kk], idx_o)
        gate_o = jnp.where(lane == kk, es[kk] / denom, gate_o)
        rank_o = jnp.where(lane == kk, rk, rank_o)
    idx_ref[...] = idx_o
    gate_ref[...] = gate_o
    rank_ref[...] = rank_o
    cnt_ref[...] = jnp.broadcast_to(carry_new, cnt_ref.shape).astype(jnp.int32)


def _cross_router_body(x_ref, gc_ref, wq_ref, kv_ref, wo_ref, gm_ref, wr_ref, br_ref,
                       o_ref, idx_ref, gate_ref, rank_ref, cnt_ref, h3_ref, hn_ref, carry_ref):
    _cross_body(x_ref, gc_ref, wq_ref, kv_ref, wo_ref, o_ref)
    _router_body(o_ref, gm_ref, wr_ref, br_ref, idx_ref, gate_ref, rank_ref, cnt_ref, h3_ref, hn_ref, carry_ref)


def _cross_router(x2d, g_cross, wq, kv, wo, g_moe, w_pad, b_pad, seq, n_mem, bm):
    m, d = x2d.shape
    bm = min(bm, seq)
    per_b = seq // bm
    cw = wq.shape[1]
    tok = pl.BlockSpec((bm, LANES), lambda i: (i, 0))
    return pl.pallas_call(
        _cross_router_body,
        grid=(m // bm,),
        in_specs=[
            pl.BlockSpec((bm, d), lambda i: (i, 0)),
            pl.BlockSpec((1, d), lambda i: (0, 0)),
            pl.BlockSpec((d, cw), lambda i: (0, 0)),
            pl.BlockSpec((n_mem, 2 * cw), lambda i: (i // per_b, 0)),
            pl.BlockSpec((cw, d), lambda i: (0, 0)),
            pl.BlockSpec((1, d), lambda i: (0, 0)),
            pl.BlockSpec((d, LANES), lambda i: (0, 0)),
            pl.BlockSpec((1, LANES), lambda i: (0, 0)),
        ],
        out_specs=[pl.BlockSpec((bm, d), lambda i: (i, 0)), tok, tok, tok,
                   pl.BlockSpec((8, LANES), lambda i: (0, 0)),
                   pl.BlockSpec((bm, 1, d // 2), lambda i: (i, 0, 0))],
        out_shape=[
            jax.ShapeDtypeStruct((m, d), F32),
            jax.ShapeDtypeStruct((m, LANES), jnp.int32),
            jax.ShapeDtypeStruct((m, LANES), F32),
            jax.ShapeDtypeStruct((m, LANES), jnp.int32),
            jax.ShapeDtypeStruct((8, LANES), jnp.int32),
            jax.ShapeDtypeStruct((m, 1, d // 2), jnp.int32),
        ],
        scratch_shapes=[pltpu.VMEM((bm, d // 2), jnp.int32), pltpu.VMEM((8, LANES), F32)],
        compiler_params=_cparams("arbitrary"),
        name="cross_router",
    )(x2d, g_cross.reshape(1, d), wq, kv, wo, g_moe.reshape(1, d), w_pad, b_pad)


def _sorted_row(idx_ref, rank_ref, pend_ref, padded_ref, a):
    e = idx_ref[a]
    return pend_ref[e] - padded_ref[e] + rank_ref[a]


def _dispatch_body(idx_ref, rank_ref, pend_ref, padded_ref, x_ref, xs_ref, z_ref, sem, zsem):
    i = pl.program_id(0)
    bm = x_ref.shape[0]
    n_exp = pend_ref.shape[0]
    tm = z_ref.shape[0]

    def zero_copy(e):
        return pltpu.make_async_copy(z_ref, xs_ref.at[pl.ds(pend_ref[e] - tm, tm)], zsem)

    @pl.when(i == 0)
    def _():
        z_ref[...] = jnp.zeros_like(z_ref)
        for e in range(n_exp):
            @pl.when(padded_ref[e] > 0)
            def _():
                zero_copy(e).start()
        for e in range(n_exp):
            @pl.when(padded_ref[e] > 0)
            def _():
                zero_copy(e).wait()

    def row_copy(r, kk):
        d = _sorted_row(idx_ref, rank_ref, pend_ref, padded_ref, (i * bm + r) * TOP_K + kk)
        return pltpu.make_async_copy(x_ref.at[r], xs_ref.at[d], sem)

    def issue(r):
        for kk in range(TOP_K):
            row_copy(r, kk).start(priority=kk % 2)

    def drain(r):
        for kk in range(TOP_K):
            row_copy(r, kk).wait()

    _for_rows(bm, issue)
    _for_rows(bm, drain)


def _dispatch(idx_flat, rank_flat, pend, padded, x3d, n_rows, bm):
    m, _, d = x3d.shape
    bm = min(bm, m)
    grid_spec = pltpu.PrefetchScalarGridSpec(
        num_scalar_prefetch=4,
        grid=(m // bm,),
        in_specs=[pl.BlockSpec((bm, 1, d), lambda i, *_: (i, 0, 0))],
        out_specs=pl.BlockSpec(memory_space=pl.ANY),
        scratch_shapes=[
            pltpu.VMEM((MOE_ROW_TILE, 1, d), x3d.dtype),
            pltpu.SemaphoreType.DMA,
            pltpu.SemaphoreType.DMA,
        ],
    )
    return pl.pallas_call(
        _dispatch_body,
        grid_spec=grid_spec,
        out_shape=jax.ShapeDtypeStruct((n_rows, 1, d), x3d.dtype),
        compiler_params=_cparams("arbitrary"),
        name="moe_dispatch",
    )(idx_flat, rank_flat, pend, padded, x3d)


def _moe_body(ie_ref, istart_ref, insub_ref, xs_ref, w1_ref, b1_ref, w2_ref, b2_ref,
              ys_ref, xbf_ref, yacc_ref, st0_ref, st1_ref, ld0_ref, ld1_ref, land_ref,
              w1b_ref, w2m_ref, w2b_ref, sems):
    it = pl.program_id(0)
    c = pl.program_id(1)
    n_chunks = pl.num_programs(1)
    nsub = insub_ref[it]
    start = istart_ref[it]
    tm = st0_ref.shape[0]
    tc = w1_ref.shape[2]
    half = tc // 2
    quarter = tc // 4
    even_lane = lax.broadcasted_iota(jnp.int32, (1, half), 1) % 2 == 0
    stage = (st0_ref, st1_ref)
    load_stage = (ld0_ref, ld1_ref)
    n_pairs = (nsub + 1) // 2

    def rows(s):
        return pl.ds(pl.multiple_of(s * tm, tm), tm)

    def load_copy(s, k):
        return pltpu.make_async_copy(xs_ref.at[pl.ds(start + s * tm, tm)], load_stage[k], sems.at[k])

    def store_copy(s, k):
        return pltpu.make_async_copy(stage[k], ys_ref.at[pl.ds(start + s * tm, tm)], sems.at[k])

    def consume(s, k):
        land_ref[...] = load_stage[k][...].reshape(land_ref.shape)
        word = land_ref[...]
        hw = word.shape[1]
        lo = lax.bitcast_convert_type(lax.shift_left(word, 16), F32)
        hi = lax.bitcast_convert_type(word & jnp.int32(-65536), F32)
        xbf_ref[rows(s), :hw] = lo.astype(BF16)
        xbf_ref[rows(s), hw:] = hi.astype(BF16)
        yacc_ref[rows(s), :] = jnp.zeros((tm, yacc_ref.shape[1]), F32)

    def hidden(rs):
        return jnp.dot(xbf_ref[rs, :], w1b_ref[...], preferred_element_type=F32) + b1_ref[0]

    def accumulated(rs, u):
        glu = jnp.minimum(u, SWIGLU_LIMIT)
        act = glu * (1.0 / (1.0 + jnp.exp(-SWIGLU_ALPHA * glu)))
        lin1 = jnp.clip(u, -SWIGLU_LIMIT, SWIGLU_LIMIT) + 1.0
        a_even = act * pltpu.roll(lin1, tc - 1, 1)
        merged = jnp.where(even_lane, a_even[:, :half], pltpu.roll(a_even[:, half:], 1, 1))
        return yacc_ref[rs, :] + jnp.dot(merged.astype(BF16), w2b_ref[...], preferred_element_type=F32)

    def emit(s, k, wait_for):
        if wait_for is not None:
            store_copy(wait_for, k).wait()
        stage[k][...] = yacc_ref[rows(s), :].reshape(stage[k].shape)
        store_copy(s, k).start()

    @pl.when(nsub > 0)
    def _():
        @pl.when(c == 0)
        def _():
            load_copy(0, 0).start()

            def load_pair(sp, carry):
                s0 = 2 * sp
                s1 = s0 + 1

                @pl.when(s1 < nsub)
                def _():
                    load_copy(s1, 1).start()
                load_copy(s0, 0).wait()
                consume(s0, 0)

                @pl.when(s1 < nsub)
                def _():
                    @pl.when(s1 + 1 < nsub)
                    def _():
                        load_copy(s1 + 1, 0).start()
                    load_copy(s1, 1).wait()
                    consume(s1, 1)
                return carry
            lax.fori_loop(0, n_pairs, load_pair, 0)

        w1b_ref[...] = w1_ref[0].astype(BF16)
        for j in range(w2m_ref.shape[0]):
            cols = slice(j * LANES, (j + 1) * LANES)
            w2m_ref[j, pl.ds(0, quarter, stride=2), :] = w2_ref[0, 0:quarter, cols]
            w2m_ref[j, pl.ds(1, quarter, stride=2), :] = w2_ref[0, quarter:2 * quarter, cols]
            w2b_ref[:, cols] = w2m_ref[j].astype(BF16)

        def group_update(first, n, bias):
            rs = [rows(first + k) for k in range(n)]
            us = [hidden(r) for r in rs]
            for k in range(n):
                yacc_ref[rs[k], :] = accumulated(rs[k], us[k]) + bias

        def single_update(s, bias):
            group_update(s, 1, bias)

        @pl.when(c < n_chunks - 1)
        def _():
            n_groups = nsub // MOE_GROUP

            def group(gi, carry):
                group_update(gi * MOE_GROUP, MOE_GROUP, 0.0)
                return carry
            lax.fori_loop(0, n_groups, group, 0)

            def single(s, carry):
                single_update(s, 0.0)
                return carry
            lax.fori_loop(n_groups * MOE_GROUP, nsub, single, 0)

        @pl.when(c == n_chunks - 1)
        def _():
            sg = len(stage)
            n_groups = nsub // sg

            @pl.when(n_groups > 0)
            def _():
                group_update(0, sg, b2_ref[0])
                for k in range(sg):
                    emit(k, k, None)

                def group(gi, carry):
                    first = gi * sg
                    group_update(first, sg, b2_ref[0])
                    for k in range(sg):
                        emit(first + k, k, first + k - sg)
                    return carry
                lax.fori_loop(1, n_groups, group, 0)

            for r in range(sg - 1):
                s = n_groups * sg + r

                @pl.when(s < nsub)
                def _():
                    single_update(s, b2_ref[0])

                    @pl.when(s >= sg)
                    def _():
                        store_copy(s - sg, r).wait()
                    emit(s, r, None)

            for k in range(sg):
                @pl.when(nsub > k)
                def _():
                    store_copy(nsub - 1 - (nsub - 1 - k) % sg, k).wait()


def _moe_mlp(item_e, item_start, item_nsub, xs, w1, b1, w2, b2):
    n_rows, _, dw = xs.shape
    n_exp, d, two_i = w1.shape
    tc = min(MOE_COL_CHUNK, two_i)
    n_chunks = two_i // tc
    n_items = item_e.shape[0]
    last = n_chunks - 1

    def chunk(c, it, insub):
        return jnp.where(insub[it] > 0, c, last)

    grid_spec = pltpu.PrefetchScalarGridSpec(
        num_scalar_prefetch=3,
        grid=(n_items, n_chunks),
        in_specs=[
            pl.BlockSpec(memory_space=pl.ANY),
            pl.BlockSpec((1, d, tc), lambda it, c, ie, ist, ins: (ie[it], 0, chunk(c, it, ins))),
            pl.BlockSpec((1, 1, tc), lambda it, c, ie, ist, ins: (ie[it], 0, chunk(c, it, ins))),
            pl.BlockSpec((1, tc // 2, d), lambda it, c, ie, ist, ins: (ie[it], chunk(c, it, ins), 0)),
            pl.BlockSpec((1, 1, d), lambda it, c, ie, ist, ins: (ie[it], 0, 0)),
        ],
        out_specs=pl.BlockSpec(memory_space=pl.ANY),
        scratch_shapes=[
            pltpu.VMEM((MOE_ROWS_PER_ITEM, d), BF16),
            pltpu.VMEM((MOE_ROWS_PER_ITEM, d), F32),
            pltpu.VMEM((MOE_ROW_TILE, 1, d), F32),
            pltpu.VMEM((MOE_ROW_TILE, 1, d), F32),
            pltpu.VMEM((MOE_ROW_TILE, 1, dw), jnp.int32),
            pltpu.VMEM((MOE_ROW_TILE, 1, dw), jnp.int32),
            pltpu.VMEM((MOE_ROW_TILE, dw), jnp.int32),
            pltpu.VMEM((d, tc), BF16),
            pltpu.VMEM((d // LANES, tc // 2, LANES), F32),
            pltpu.VMEM((tc // 2, d), BF16),
            pltpu.SemaphoreType.DMA((2,)),
        ],
    )
    return pl.pallas_call(
        _moe_body,
        grid_spec=grid_spec,
        out_shape=jax.ShapeDtypeStruct((n_rows, 1, d), F32),
        compiler_params=_cparams("arbitrary", "arbitrary"),
        name="moe_mlp",
    )(item_e, item_start, item_nsub, xs, w1, b1, w2, b2)


def _combine_body(idx_ref, rank_ref, pend_ref, padded_ref, x_ref, gate_ref, g_ref, ys_ref, o_ref, b0, b1, b2, b3, flat_ref, sem, *, final_norm):
    i = pl.program_id(0)
    bm = x_ref.shape[0]
    bufs = (b0, b1, b2, b3)

    def row_copy(r, kk):
        d = _sorted_row(idx_ref, rank_ref, pend_ref, padded_ref, (i * bm + r) * TOP_K + kk)
        return pltpu.make_async_copy(ys_ref.at[d], bufs[kk].at[r], sem)

    def issue(r):
        for kk in range(TOP_K):
            row_copy(r, kk).start(priority=kk % 2)

    def drain(r):
        for kk in range(TOP_K):
            row_copy(r, kk).wait()

    _for_rows(bm, issue)
    _for_rows(bm, drain)
    acc = x_ref[...]
    for kk in range(TOP_K):
        flat_ref[...] = bufs[kk][...].reshape(flat_ref.shape)
        acc = acc + gate_ref[:, kk:kk + 1] * flat_ref[...]
    o_ref[...] = _rmsnorm(acc, g_ref[...]) if final_norm else acc


def _combine(idx_flat, rank_flat, pend, padded, x2d, gate4, g, ys, bm, final_norm):
    m, d = x2d.shape
    bm = min(bm, m)
    grid_spec = pltpu.PrefetchScalarGridSpec(
        num_scalar_prefetch=4,
        grid=(m // bm,),
        in_specs=[
            pl.BlockSpec((bm, d), lambda i, *_: (i, 0)),
            pl.BlockSpec((bm, LANES), lambda i, *_: (i, 0)),
            pl.BlockSpec((1, d), lambda i, *_: (0, 0)),
            pl.BlockSpec(memory_space=pl.ANY),
        ],
        out_specs=pl.BlockSpec((bm, d), lambda i, *_: (i, 0)),
        scratch_shapes=[pltpu.VMEM((bm, 1, d), F32)] * TOP_K + [pltpu.VMEM((bm, d), F32), pltpu.SemaphoreType.DMA],
    )
    return pl.pallas_call(
        functools.partial(_combine_body, final_norm=final_norm),
        grid_spec=grid_spec,
        out_shape=jax.ShapeDtypeStruct((m, d), F32),
        compiler_params=_cparams("arbitrary"),
        name="moe_combine",
    )(idx_flat, rank_flat, pend, padded, x2d, gate4, g.reshape(1, d), ys)


def _routing_tables(counts, n_assign, n_exp):
    tm, rpi = MOE_ROW_TILE, MOE_ROWS_PER_ITEM
    padded = (counts + tm - 1) // tm * tm
    pend = jnp.cumsum(padded)
    pstart = pend - padded

    n_rows = n_assign + n_exp * tm
    n_items = n_rows // rpi + n_exp
    per_e = (padded + rpi - 1) // rpi
    icum = jnp.cumsum(per_e)
    ioff = icum - per_e
    total = icum[-1]
    ii = jnp.arange(n_items, dtype=jnp.int32)
    ic = jnp.minimum(ii, total - 1)
    e_of = jnp.sum((icum[None, :] <= ic[:, None]).astype(jnp.int32), axis=1)
    e_of = jnp.minimum(e_of, n_exp - 1)
    local = ii - ioff[e_of]
    valid = ii < total
    start = jnp.where(valid, pstart[e_of] + local * rpi, 0).astype(jnp.int32)
    nsub = jnp.where(valid, jnp.clip((padded[e_of] - local * rpi) // tm, 0, rpi // tm), 0).astype(jnp.int32)
    return pend.astype(jnp.int32), padded.astype(jnp.int32), e_of, start, nsub, n_rows


def _alibi_slopes(n):
    start = 2.0 ** (-8.0 / n)
    return np.array([start ** (i + 1) for i in range(n)], dtype=np.float32)


def kernel(x, mem, g_mix, w_in, b_forget, swa_sinks, w_out, g_cross, g_mem, w_cq, w_ckv, w_co, g_moe,
           w_router, b_router, w_mlp1, b_mlp1, w_mlp2, b_mlp2, g_final):
    batch, seq, d = x.shape
    n_mem = mem.shape[1]
    depth = w_in.shape[0]
    n_fox = d // (2 * HEAD_DIM)
    n_swa = d // (2 * HEAD_DIM)
    n_kv = n_swa // SWA_GROUP
    fox_w = n_fox * HEAD_DIM
    swa_q_w = n_swa * HEAD_DIM
    swa_kv_w = n_kv * HEAD_DIM
    o2 = 2 * fox_w
    o3 = 3 * fox_w
    o4 = o3 + n_fox
    o5 = o4 + swa_q_w
    o6 = o5 + swa_kv_w
    n_exp = w_router.shape[2]
    t = batch * seq
    slopes = jnp.asarray(_alibi_slopes(n_swa))

    def dup_heads(wk):
        wk = wk.reshape(d, n_kv, 1, HEAD_DIM)
        return jnp.broadcast_to(wk, (d, n_kv, 2, HEAD_DIM)).reshape(d, n_kv * LANES)

    x2d = x.reshape(t, d)
    for l in range(depth):
        wl = w_in[l]
        w_qk = wl[:, :o2].astype(BF16)
        w_vt = wl[:, o2:o3].T.astype(BF16)
        w_ff = jnp.zeros((d, LANES), BF16).at[:, :n_fox].set(wl[:, o3:o4].astype(BF16))
        b_ff = jnp.zeros((1, LANES), F32).at[0, :n_fox].set(b_forget[l])
        w_swa = jnp.concatenate(
            [wl[:, o4:o5], dup_heads(wl[:, o5:o6]), dup_heads(wl[:, o6:])], axis=1).astype(BF16)

        proj = _norm_matmul(x2d, g_mix[l], jnp.concatenate([w_qk, w_swa], axis=1), BF16, 1024, fox_w,
                            "qk_swa_proj", lead_tiles=1, lead_scale=HEAD_DIM ** -0.5 * LOG2E)
        vt = _norm_matmul(x2d, g_mix[l], w_vt, BF16, 1024, 1024, "fox_vt_proj", transposed=True)
        c = _fox_gate(x2d.reshape(batch, seq, d), g_mix[l], w_ff, b_ff, 512)
        fox = _fox_attention(proj, vt, c, batch, seq, fox_w // LANES, FOX_TILE)
        swa = _swa_attention(proj, o2 // LANES, slopes, swa_sinks[l], batch, seq, swa_q_w // LANES, n_kv)
        x2d = _out_proj(fox, swa, w_out[l].astype(BF16), x2d, 1024, 1024)

        kv = _norm_matmul(mem.reshape(batch * n_mem, d), g_mem[l], w_ckv[l].astype(BF16), BF16, 1024, 1024,
                          "mem_kv_proj")
        w_r = jnp.zeros((d, LANES), BF16).at[:, :n_exp].set(w_router[l].astype(BF16))
        b_r = jnp.full((1, LANES), NEG_BIG, F32).at[0, :n_exp].set(b_router[l])
        x2d, idx4, gate4, rank4, cnt, h3d = _cross_router(
            x2d, g_cross[l], w_cq[l].astype(BF16), kv, w_co[l].astype(BF16), g_moe[l], w_r, b_r, seq, n_mem, 512)
        pend, padded, item_e, item_start, item_nsub, n_rows = _routing_tables(cnt[0, :n_exp], t * TOP_K, n_exp)
        idx_flat = idx4[:, :TOP_K].reshape(-1)
        rank_flat = rank4[:, :TOP_K].reshape(-1)
        xs = _dispatch(idx_flat, rank_flat, pend, padded, h3d, n_rows, ROW_COPY_TOKENS)
        two_i = w_mlp1.shape[3]
        ys = _moe_mlp(item_e, item_start, item_nsub, xs, w_mlp1[l], b_mlp1[l].reshape(n_exp, 1, two_i),
                      w_mlp2[l], b_mlp2[l].reshape(n_exp, 1, d))
        x2d = _combine(idx_flat, rank_flat, pend, padded, x2d, gate4, g_final, ys, ROW_COPY_TOKENS,
                       final_norm=(l == depth - 1))
    return x2d.reshape(batch, seq, d)
```
